```python
import math
import jax, jax.numpy as jnp
from jax import lax
import numpy as np

D_MODEL = 1024
BATCH = 4
SEQ = 8192
DEPTH = 2

HEAD_DIM = 64
SWA_Q_HEADS = 8
SWA_KV_HEADS = 2
SWA_WINDOW = 128
FOX_HEADS = 8
Q_BLOCK = 128
N_BUCKETS = 32
MAX_DISTANCE = 128
N_EXPERTS = 256
TOP_K = 8
N_GROUPS = 8
TOPK_GROUPS = 4
D_EXPERT = 256
D_SHARED = 256
ROUTED_SCALE = 2.5
EXPERT_BLOCK = 256
LN_EPS = 1e-5
ALPHA = (2 * DEPTH) ** 0.25
BETA = (8 * DEPTH) ** -0.25

SWA_Q = SWA_Q_HEADS * HEAD_DIM
SWA_KV = SWA_KV_HEADS * HEAD_DIM
FOX_W = FOX_HEADS * HEAD_DIM
IN_SIZES = (SWA_Q, SWA_KV, SWA_KV, FOX_W, FOX_W, FOX_W, FOX_HEADS, D_MODEL, D_MODEL)
IN_COLS = SWA_Q + 2 * SWA_KV + 3 * FOX_W + FOX_HEADS + 2 * D_MODEL

kernel_name = "hybrid_swa_fox_moe_deepnorm"


def layer_norm(x, g, b):
    xf = x.astype(jnp.float32)
    mu = xf.mean(-1, keepdims=True)
    var = jnp.square(xf - mu).mean(-1, keepdims=True)
    y = (xf - mu) * lax.rsqrt(var + LN_EPS)
    return (y * g.astype(jnp.float32) + b.astype(jnp.float32)).astype(x.dtype)


def t5_bucket(dist):
    n = jnp.maximum(dist, 0)
    max_exact = N_BUCKETS // 2
    nf = jnp.maximum(n, 1).astype(jnp.float32)
    large = max_exact + (jnp.log(nf / max_exact) / math.log(MAX_DISTANCE / max_exact)
                         * (N_BUCKETS - max_exact)).astype(jnp.int32)
    large = jnp.minimum(large, N_BUCKETS - 1)
    return jnp.where(n < max_exact, n, large)


def swa_attention(q, k, v, sinks, rel_bias):
    B, S = q.shape[:2]
    nb = S // Q_BLOCK
    G = SWA_Q_HEADS // SWA_KV_HEADS
    qb = q.reshape(B, nb, Q_BLOCK, SWA_KV_HEADS, G, HEAD_DIM)

    def band(t):
        tb = t.reshape(B, nb, Q_BLOCK, SWA_KV_HEADS, HEAD_DIM)
        prev = jnp.pad(tb[:, :-1], ((0, 0), (1, 0), (0, 0), (0, 0), (0, 0)))
        return jnp.concatenate([prev, tb], axis=2)

    kb, vb = band(k), band(v)
    s = jnp.einsum('bnqkgd,bnskd->bnkgqs', qb, kb).astype(jnp.float32) * (HEAD_DIM ** -0.5)
    i = jnp.arange(Q_BLOCK)[:, None]
    j = jnp.arange(2 * Q_BLOCK)[None, :]
    dist = i + Q_BLOCK - j
    blk = jnp.arange(nb)[:, None, None]
    key_pos = (blk - 1) * Q_BLOCK + j[None]
    valid = (dist >= 0)[None] & (dist < SWA_WINDOW)[None] & (key_pos >= 0)
    bias = rel_bias[t5_bucket(dist)].astype(jnp.float32)
    bias = bias.transpose(2, 0, 1).reshape(SWA_KV_HEADS, G, Q_BLOCK, 2 * Q_BLOCK)
    s = jnp.where(valid[None, :, None, None], s + bias[None, None], -jnp.inf)
    sink = sinks.astype(jnp.float32).reshape(SWA_KV_HEADS, G)[None, None, :, :, None, None]
    m = jnp.maximum(s.max(-1, keepdims=True), sink)
    p = jnp.exp(s - m)
    denom = p.sum(-1, keepdims=True) + jnp.exp(sink - m)
    o = jnp.einsum('bnkgqs,bnskd->bnqkgd', (p / denom).astype(v.dtype), vb)
    return o.reshape(B, S, SWA_Q)


def forgetting_attention(q, k, v, log_f):
    B, S = q.shape[:2]
    nb = S // Q_BLOCK
    c = jnp.cumsum(log_f, axis=1)
    c_keys = c.transpose(0, 2, 1)
    q_blocks = q.reshape(B, nb, Q_BLOCK, FOX_HEADS, HEAD_DIM).transpose(1, 0, 2, 3, 4)
    c_blocks = c.reshape(B, nb, Q_BLOCK, FOX_HEADS).transpose(1, 0, 3, 2)
    key_pos = jnp.arange(S)
    scale = HEAD_DIM ** -0.5

    def one_block(args):
        qb, cb, n = args
        s = jnp.einsum('bqhd,bshd->bhqs', qb, k).astype(jnp.float32) * scale
        s = s + cb[..., None] - c_keys[:, :, None, :]
        q_pos = n * Q_BLOCK + jnp.arange(Q_BLOCK)
        s = jnp.where(key_pos[None, :] <= q_pos[:, None], s, -jnp.inf)
        p = jax.nn.softmax(s, axis=-1)
        return jnp.einsum('bhqs,bshd->bqhd', p.astype(v.dtype), v)

    o = lax.map(one_block, (q_blocks, c_blocks, jnp.arange(nb)))
    return o.transpose(1, 0, 2, 3, 4).reshape(B, S, FOX_W)


def moe(x2, w_router, router_bias, w_gate_e, w_up_e, w_down_e, w_gate_s, w_up_s, w_down_s):
    T, D = x2.shape
    scores = jax.nn.sigmoid((x2 @ w_router).astype(jnp.float32))
    biased = scores + router_bias.astype(jnp.float32)
    grouped = biased.reshape(T, N_GROUPS, N_EXPERTS // N_GROUPS)
    group_score = lax.top_k(grouped, 2)[0].sum(-1)
    _, top_groups = lax.top_k(group_score, TOPK_GROUPS)
    group_mask = (top_groups[..., None] == jnp.arange(N_GROUPS)).any(1)
    expert_mask = jnp.repeat(group_mask, N_EXPERTS // N_GROUPS, axis=1)
    _, top_e = lax.top_k(jnp.where(expert_mask, biased, -jnp.inf), TOP_K)
    w = jnp.take_along_axis(scores, top_e, axis=1)
    w = w / w.sum(-1, keepdims=True) * ROUTED_SCALE

    N = T * TOP_K
    flat_e = top_e.reshape(-1)
    flat_tok = jnp.arange(N, dtype=jnp.int32) // TOP_K
    flat_w = w.reshape(-1)
    order = jnp.argsort(flat_e)
    e_sorted = flat_e[order]
    counts = jnp.zeros((N_EXPERTS,), jnp.int32).at[flat_e].add(1)
    padded = (counts + EXPERT_BLOCK - 1) // EXPERT_BLOCK * EXPERT_BLOCK
    start = jnp.cumsum(counts) - counts
    pend = jnp.cumsum(padded)
    pstart = pend - padded
    dest = pstart[e_sorted] + jnp.arange(N, dtype=jnp.int32) - start[e_sorted]
    P = (N + N_EXPERTS * (EXPERT_BLOCK - 1) + EXPERT_BLOCK - 1) // EXPERT_BLOCK * EXPERT_BLOCK
    n_blocks = P // EXPERT_BLOCK
    row_tok = jnp.zeros((P,), jnp.int32).at[dest].set(flat_tok[order])
    row_w = jnp.zeros((P,), jnp.float32).at[dest].set(flat_w[order])
    block_e = jnp.minimum(
        jnp.searchsorted(pend, jnp.arange(n_blocks, dtype=jnp.int32) * EXPERT_BLOCK, side='right'),
        N_EXPERTS - 1)

    def step(acc, blk):
        tok, rw, e = blk
        xb = x2[tok]
        h = jax.nn.silu(xb @ w_gate_e[e]) * (xb @ w_up_e[e])
        yb = (h @ w_down_e[e]).astype(jnp.float32) * rw[:, None]
        return acc.at[tok].add(yb), None

    routed, _ = lax.scan(step, jnp.zeros((T, D), jnp.float32),
                         (row_tok.reshape(n_blocks, EXPERT_BLOCK),
                          row_w.reshape(n_blocks, EXPERT_BLOCK), block_e))
    shared = (jax.nn.silu(x2 @ w_gate_s) * (x2 @ w_up_s)) @ w_down_s
    return shared + routed.astype(x2.dtype)


def setup_inputs(seed: int = 0) -> dict:
    key = jax.random.key(seed)
    ks = jax.random.split(key, 24)
    f32 = jnp.float32
    nrm = lambda k, shape, s: jax.random.normal(k, shape, f32) * s
    col_scale = jnp.concatenate([
        jnp.full((n,), BETA if idx in (2, 5) else 1.0, f32) for idx, n in enumerate(IN_SIZES)])
    return {
        "x": nrm(ks[0], (BATCH, SEQ, D_MODEL), 1.0),
        "w_in": nrm(ks[1], (DEPTH, D_MODEL, IN_COLS), D_MODEL ** -0.5) * col_scale,
        "b_forget": jax.random.uniform(ks[2], (DEPTH, FOX_HEADS), f32, 1.0, 5.0),
        "attn_sinks": nrm(ks[3], (DEPTH, SWA_Q_HEADS), 0.5),
        "rel_bias": nrm(ks[4], (N_BUCKETS, SWA_Q_HEADS), 0.5),
        "w_proj_a": nrm(ks[5], (DEPTH, SWA_Q, D_MODEL), SWA_Q ** -0.5),
        "w_proj_b": nrm(ks[6], (DEPTH, FOX_W, D_MODEL), FOX_W ** -0.5),
        "w_out": nrm(ks[7], (DEPTH, D_MODEL, D_MODEL), D_MODEL ** -0.5 * BETA),
        "ln1_g": 1.0 + nrm(ks[8], (DEPTH, D_MODEL), 0.02),
        "ln1_b": nrm(ks[9], (DEPTH, D_MODEL), 0.02),
        "w_router": nrm(ks[10], (DEPTH, D_MODEL, N_EXPERTS), D_MODEL ** -0.5),
        "router_bias": nrm(ks[11], (DEPTH, N_EXPERTS), 0.01),
        "w_gate_e": nrm(ks[12], (DEPTH, N_EXPERTS, D_MODEL, D_EXPERT), D_MODEL ** -0.5),
        "w_up_e": nrm(ks[13], (DEPTH, N_EXPERTS, D_MODEL, D_EXPERT), D_MODEL ** -0.5),
        "w_down_e": nrm(ks[14], (DEPTH, N_EXPERTS, D_EXPERT, D_MODEL), D_EXPERT ** -0.5 * BETA),
        "w_gate_s": nrm(ks[15], (DEPTH, D_MODEL, D_SHARED), D_MODEL ** -0.5),
        "w_up_s": nrm(ks[16], (DEPTH, D_MODEL, D_SHARED), D_MODEL ** -0.5),
        "w_down_s": nrm(ks[17], (DEPTH, D_SHARED, D_MODEL), D_SHARED ** -0.5 * BETA),
        "ln2_g": 1.0 + nrm(ks[18], (DEPTH, D_MODEL), 0.02),
        "ln2_b": nrm(ks[19], (DEPTH, D_MODEL), 0.02),
    }


def reference(x, w_in, b_forget, attn_sinks, rel_bias, w_proj_a, w_proj_b, w_out, ln1_g, ln1_b,
              w_router, router_bias, w_gate_e, w_up_e, w_down_e, w_gate_s, w_up_s, w_down_s,
              ln2_g, ln2_b):
    B, S, D = x.shape
    split_at = np.cumsum(np.array(IN_SIZES))[:-1].tolist()
    for l in range(DEPTH):
        proj = x @ w_in[l]
        qa, ka, va, qf, kf, vf, zf, ga, gb = jnp.split(proj, split_at, axis=-1)
        ya = swa_attention(qa.reshape(B, S, SWA_Q_HEADS, HEAD_DIM),
                           ka.reshape(B, S, SWA_KV_HEADS, HEAD_DIM),
                           va.reshape(B, S, SWA_KV_HEADS, HEAD_DIM),
                           attn_sinks[l], rel_bias)
        log_f = jax.nn.log_sigmoid((zf + b_forget[l]).astype(jnp.float32))
        yb = forgetting_attention(qf.reshape(B, S, FOX_HEADS, HEAD_DIM),
                                  kf.reshape(B, S, FOX_HEADS, HEAD_DIM),
                                  vf.reshape(B, S, FOX_HEADS, HEAD_DIM), log_f)
        mix = jax.nn.sigmoid(ga) * (ya @ w_proj_a[l]) + jax.nn.sigmoid(gb) * (yb @ w_proj_b[l])
        x = layer_norm(ALPHA * x + mix @ w_out[l], ln1_g[l], ln1_b[l])
        y = moe(x.reshape(B * S, D), w_router[l], router_bias[l], w_gate_e[l], w_up_e[l],
                w_down_e[l], w_gate_s[l], w_up_s[l], w_down_s[l]).reshape(B, S, D)
        x = layer_norm(ALPHA * x + y, ln2_g[l], ln2_b[l])
    return x
```

```python
import functools
import math

import jax
import jax.numpy as jnp
import numpy as np
from jax import lax
from jax.experimental import pallas as pl
from jax.experimental.pallas import tpu as pltpu

D_MODEL = 1024
HEAD_DIM = 64
SWA_Q_HEADS = 8
SWA_KV_HEADS = 2
SWA_WINDOW = 128
FOX_HEADS = 8
Q_BLOCK = 128
N_BUCKETS = 32
MAX_DISTANCE = 128
N_EXPERTS = 256
TOP_K = 8
N_GROUPS = 8
TOPK_GROUPS = 4
GROUP_SIZE = N_EXPERTS // N_GROUPS
D_EXPERT = 256
D_SHARED = 256
ROUTED_SCALE = 2.5
LN_EPS = 1e-5

SWA_Q = SWA_Q_HEADS * HEAD_DIM
SWA_KV = SWA_KV_HEADS * HEAD_DIM
FOX_W = FOX_HEADS * HEAD_DIM
IN_SIZES = (SWA_Q, SWA_KV, SWA_KV, FOX_W, FOX_W, FOX_W, FOX_HEADS, D_MODEL, D_MODEL)

LANES = 128
SUBLANES = 8
ROW_CHUNKS = D_MODEL // LANES
VMEM_LIMIT = 56 * 1024 * 1024

TM_PROJ = 512
TM_MERGE = 256
TM_ROUTE = 256
TM_MOVE = 256
EXPERT_BLOCK = 256
FOX_TQ = 256
FOX_TK = 256

NEG_BIG = -1e30
AUG = HEAD_DIM


def _cparams(sem):
    return pltpu.CompilerParams(dimension_semantics=sem, vmem_limit_bytes=VMEM_LIMIT)


def _sigmoid(x):
    return 1.0 / (1.0 + jnp.exp(-x))


def _split3(c):
    hi = c.astype(jnp.bfloat16).astype(jnp.float32)
    r = c - hi
    mid = r.astype(jnp.bfloat16).astype(jnp.float32)
    lo = (r - mid).astype(jnp.bfloat16).astype(jnp.float32)
    return hi, mid, lo


_C_QA = 0
_C_KA = _C_QA + SWA_Q
_C_VA = _C_KA + SWA_KV
_C_QF = _C_VA + SWA_KV
_C_KF = _C_QF + FOX_W
_C_VF = _C_KF + FOX_W
_C_GA = _C_VF + FOX_W
_C_GB = _C_GA + D_MODEL
_C_ZF = _C_GB + D_MODEL
_N_PROJ = _C_ZF + LANES


def _in_proj_kernel(tiles_per_seq, x_ref, w_ref, bf_ref, tri_ref,
                    qa_ref, ka_ref, va_ref, qf_ref, kf_ref, vf_ref, ga_ref, gb_ref, carry_ref):
    i = pl.program_id(0)
    tm = x_ref.shape[0]
    xb = x_ref[...].astype(jnp.bfloat16)

    def proj(c0, n):
        return jnp.dot(xb, w_ref[:, c0:c0 + n], preferred_element_type=jnp.float32)

    scale = HEAD_DIM ** -0.5
    qa_ref[...] = (proj(_C_QA, SWA_Q) * scale).astype(jnp.bfloat16)
    ka_ref[...] = proj(_C_KA, SWA_KV).astype(jnp.bfloat16)
    va_ref[...] = proj(_C_VA, SWA_KV).astype(jnp.bfloat16)
    vf_ref[...] = proj(_C_VF, FOX_W).astype(jnp.bfloat16)
    ga_ref[...] = proj(_C_GA, D_MODEL)
    gb_ref[...] = proj(_C_GB, D_MODEL)

    z = proj(_C_ZF, LANES) + bf_ref[...]
    log_f = jnp.minimum(z, 0.0) - jnp.log(1.0 + jnp.exp(-jnp.abs(z)))

    @pl.when(i % tiles_per_seq == 0)
    def _():
        carry_ref[...] = jnp.zeros_like(carry_ref)

    tri = tri_ref[...]
    h3 = _split3(log_f)
    c = carry_ref[...]
    for part in h3:
        c = c + jnp.dot(tri, part.astype(jnp.bfloat16), preferred_element_type=jnp.float32)
    carry_ref[...] = c[tm - 1:tm, :]

    qf = proj(_C_QF, FOX_W) * scale
    kf = proj(_C_KF, FOX_W)
    lane = lax.broadcasted_iota(jnp.int32, (tm, LANES), 1)
    for h in range(FOX_HEADS):
        pair = (h // 2) * LANES
        qp = qf[:, pair:pair + LANES]
        kp = kf[:, pair:pair + LANES]
        if h % 2 == 1:
            qp = pltpu.roll(qp, HEAD_DIM, axis=1)
            kp = pltpu.roll(kp, HEAD_DIM, axis=1)
        ch = jnp.broadcast_to(c[:, h:h + 1], (tm, LANES))
        hi, mid, lo = _split3(ch)
        one = jnp.float32(1.0)
        zero = jnp.float32(0.0)
        q_aug = jnp.where(lane == AUG, hi, jnp.where(lane == AUG + 1, mid, jnp.where(
            lane == AUG + 2, lo, jnp.where(lane < AUG + 6, one, zero))))
        k_aug = jnp.where(lane < AUG + 3, one, jnp.where(lane == AUG + 3, -hi, jnp.where(
            lane == AUG + 4, -mid, jnp.where(lane == AUG + 5, -lo, zero))))
        qf_ref[h] = jnp.where(lane < AUG, qp, q_aug).astype(jnp.bfloat16)
        kf_ref[h] = jnp.where(lane < AUG, kp, k_aug).astype(jnp.bfloat16)


def _in_proj(x2, w_main, bf_row, seq_len):
    t = x2.shape[0]
    tm = min(TM_PROJ, seq_len)
    tri = jnp.tril(jnp.ones((tm, tm), jnp.bfloat16))
    row = lambda n: pl.BlockSpec((tm, n), lambda i: (i, 0))
    full = lambda a: pl.BlockSpec(a.shape, lambda i: (0,) * a.ndim)
    slab = pl.BlockSpec((FOX_HEADS, tm, LANES), lambda i: (0, i, 0))
    bf16, f32 = jnp.bfloat16, jnp.float32
    out_shape = (
        jax.ShapeDtypeStruct((t, SWA_Q), bf16), jax.ShapeDtypeStruct((t, SWA_KV), bf16),
        jax.ShapeDtypeStruct((t, SWA_KV), bf16),
        jax.ShapeDtypeStruct((FOX_HEADS, t, LANES), bf16), jax.ShapeDtypeStruct((FOX_HEADS, t, LANES), bf16),
        jax.ShapeDtypeStruct((t, FOX_W), bf16),
        jax.ShapeDtypeStruct((t, D_MODEL), f32), jax.ShapeDtypeStruct((t, D_MODEL), f32))
    return pl.pallas_call(
        functools.partial(_in_proj_kernel, seq_len // tm),
        grid=(t // tm,),
        in_specs=[row(D_MODEL), full(w_main), full(bf_row), full(tri)],
        out_specs=(row(SWA_Q), row(SWA_KV), row(SWA_KV), slab, slab, row(FOX_W), row(D_MODEL), row(D_MODEL)),
        out_shape=out_shape,
        scratch_shapes=[pltpu.VMEM((1, LANES), f32)],
        compiler_params=_cparams(("arbitrary",)),
        name="in_proj",
    )(x2, w_main, bf_row, tri)


def _swa_kernel(blocks_per_seq, sink_ref, q_ref, kp_ref, kc_ref, vp_ref, vc_ref, bias_ref, o_ref):
    n = pl.program_id(0)
    first_key = jnp.where((n % blocks_per_seq) > 0, 0, Q_BLOCK)
    kk = jnp.concatenate([kp_ref[...], kc_ref[...]], axis=0)
    vv = jnp.concatenate([vp_ref[...], vc_ref[...]], axis=0)
    qi = lax.broadcasted_iota(jnp.int32, (Q_BLOCK, 2 * Q_BLOCK), 0)
    kj = lax.broadcasted_iota(jnp.int32, (Q_BLOCK, 2 * Q_BLOCK), 1)
    dist = qi + Q_BLOCK - kj
    valid = (dist >= 0) & (dist < SWA_WINDOW) & (kj >= first_key)
    lane = lax.broadcasted_iota(jnp.int32, (Q_BLOCK, LANES), 1)
    lo_half = lane < HEAD_DIM
    group = SWA_Q_HEADS // SWA_KV_HEADS
    for j in range(group):
        qs = q_ref[:, j * LANES:(j + 1) * LANES]
        outs = []
        for kv in range(SWA_KV_HEADS):
            head = kv * group + j
            qm = jnp.where(lo_half if kv == 0 else ~lo_half, qs, jnp.zeros_like(qs))
            s = lax.dot_general(qm, kk, (((1,), (1,)), ((), ())), preferred_element_type=jnp.float32)
            s = jnp.where(valid, s + bias_ref[head], -jnp.inf)
            sink = sink_ref[head]
            m = jnp.maximum(jnp.max(s, axis=-1, keepdims=True), sink)
            p = jnp.exp(s - m)
            denom = jnp.sum(p, axis=-1, keepdims=True) + jnp.exp(sink - m)
            pn = (p / denom).astype(jnp.bfloat16)
            outs.append(jnp.dot(pn, vv, preferred_element_type=jnp.float32))
        o_ref[:, j * LANES:(j + 1) * LANES] = jnp.where(lo_half, outs[0], outs[1]).astype(o_ref.dtype)


def _swa(qa, ka, va, sinks, bias_tab, seq_len):
    t = qa.shape[0]
    bps = seq_len // Q_BLOCK
    cur = lambda n, s: (n, 0)
    prev = lambda n, s: (jnp.maximum(n - 1, 0), 0)
    grid_spec = pltpu.PrefetchScalarGridSpec(
        num_scalar_prefetch=1,
        grid=(t // Q_BLOCK,),
        in_specs=[pl.BlockSpec((Q_BLOCK, SWA_Q), cur),
                  pl.BlockSpec((Q_BLOCK, SWA_KV), prev), pl.BlockSpec((Q_BLOCK, SWA_KV), cur),
                  pl.BlockSpec((Q_BLOCK, SWA_KV), prev), pl.BlockSpec((Q_BLOCK, SWA_KV), cur),
                  pl.BlockSpec(bias_tab.shape, lambda n, s: (0, 0, 0))],
        out_specs=pl.BlockSpec((Q_BLOCK, SWA_Q), cur))
    return pl.pallas_call(
        functools.partial(_swa_kernel, bps),
        grid_spec=grid_spec,
        out_shape=jax.ShapeDtypeStruct((t, SWA_Q), jnp.bfloat16),
        compiler_params=_cparams(("arbitrary",)),
        name="swa",
    )(sinks, qa, ka, ka, va, va, bias_tab)


def _t5_bias_table(rel_bias):
    i = np.arange(Q_BLOCK)[:, None]
    j = np.arange(2 * Q_BLOCK)[None, :]
    n = np.maximum(i + Q_BLOCK - j, 0)
    max_exact = N_BUCKETS // 2
    nf = jnp.maximum(jnp.asarray(n), 1).astype(jnp.float32)
    large = max_exact + (jnp.log(nf / max_exact) / math.log(MAX_DISTANCE / max_exact)
                         * (N_BUCKETS - max_exact)).astype(jnp.int32)
    large = jnp.minimum(large, N_BUCKETS - 1)
    bucket = jnp.where(jnp.asarray(n) < max_exact, jnp.asarray(n), large)
    return rel_bias[bucket].astype(jnp.float32).transpose(2, 0, 1)


def _fox_kernel(q_ref, k_ref, v_ref, o_ref):
    qi = pl.program_id(2)
    tq = q_ref.shape[1]
    tk = FOX_TK
    lane = lax.broadcasted_iota(jnp.int32, (tq, LANES), 1)
    row = lax.broadcasted_iota(jnp.int32, (tq, tk), 0)
    col = lax.broadcasted_iota(jnp.int32, (tq, tk), 1)
    outs = []
    for hh in range(2):
        q = q_ref[hh]

        def step(j, carry, masked):
            m, l, acc = carry
            start = pl.multiple_of(j * tk, tk)
            k = k_ref[hh, pl.ds(start, tk), :]
            v = v_ref[pl.ds(start, tk), :]
            s = lax.dot_general(q, k, (((1,), (1,)), ((), ())), preferred_element_type=jnp.float32)
            if masked:
                s = jnp.where(col <= row, s, -jnp.inf)
            m_new = jnp.maximum(m, jnp.max(s, axis=-1, keepdims=True))
            a = jnp.exp(m - m_new)
            p = jnp.exp(s - m_new)
            l = a * l + jnp.sum(p, axis=-1, keepdims=True)
            acc = a * acc + jnp.dot(p.astype(jnp.bfloat16), v, preferred_element_type=jnp.float32)
            return m_new, l, acc

        init = (jnp.full((tq, 1), -jnp.inf, jnp.float32), jnp.zeros((tq, 1), jnp.float32),
                jnp.zeros((tq, LANES), jnp.float32))
        carry = lax.fori_loop(0, qi, functools.partial(step, masked=False), init)
        m, l, acc = step(qi, carry, masked=True)
        outs.append(acc / l)
    o_ref[...] = jnp.where(lane < HEAD_DIM, outs[0], outs[1]).astype(o_ref.dtype)


def _fox(qf, kf, vf, batch, seq_len):
    t = vf.shape[0]
    tq = min(FOX_TQ, seq_len)
    assert tq == FOX_TK or seq_len == tq
    nq = seq_len // tq
    pairs = FOX_HEADS // 2
    return pl.pallas_call(
        _fox_kernel,
        grid=(batch, pairs, nq),
        in_specs=[pl.BlockSpec((2, tq, LANES), lambda b, hp, i: (hp, b * nq + i, 0)),
                  pl.BlockSpec((2, seq_len, LANES), lambda b, hp, i: (hp, b, 0)),
                  pl.BlockSpec((seq_len, LANES), lambda b, hp, i: (b, hp))],
        out_specs=pl.BlockSpec((tq, LANES), lambda b, hp, i: (b * nq + i, hp)),
        out_shape=jax.ShapeDtypeStruct((t, FOX_W), jnp.bfloat16),
        compiler_params=_cparams(("arbitrary", "arbitrary", "arbitrary")),
        name="fox",
    )(qf, kf, vf)


def _layer_norm(h, g, b):
    mu = jnp.mean(h, axis=-1, keepdims=True)
    d = h - mu
    var = jnp.mean(d * d, axis=-1, keepdims=True)
    return d * lax.rsqrt(var + LN_EPS) * g + b


def _store_token_tiles(ref, x):
    rows = x.shape[0]
    for c in range(ROW_CHUNKS):
        ref[pl.ds(c, rows, stride=ROW_CHUNKS), :] = x[:, c * LANES:(c + 1) * LANES]


def _load_token_tiles(ref, rows, lead=()):
    return jnp.concatenate(
        [ref[lead + (pl.ds(c, rows, stride=ROW_CHUNKS), slice(None))] for c in range(ROW_CHUNKS)], axis=1)


def _merge_kernel(alpha, x_ref, ya_ref, yb_ref, ga_ref, gb_ref, wa_ref, wb_ref, wo_ref, g_ref, b_ref,
                  x1_ref, x1t_ref):
    pa = jnp.dot(ya_ref[...], wa_ref[...], preferred_element_type=jnp.float32)
    pb = jnp.dot(yb_ref[...], wb_ref[...], preferred_element_type=jnp.float32)
    mix = _sigmoid(ga_ref[...]) * pa + _sigmoid(gb_ref[...]) * pb
    out = jnp.dot(mix.astype(jnp.bfloat16), wo_ref[...], preferred_element_type=jnp.float32)
    x1 = _layer_norm(alpha * x_ref[...] + out, g_ref[...], b_ref[...])
    x1_ref[...] = x1
    _store_token_tiles(x1t_ref, x1)


def _merge(alpha, x2, ya, yb, ga, gb, wa, wb, wo, g, b):
    t = x2.shape[0]
    tm = TM_MERGE
    row = lambda n: pl.BlockSpec((tm, n), lambda i: (i, 0))
    full = lambda a: pl.BlockSpec(a.shape, lambda i: (0,) * a.ndim)
    return pl.pallas_call(
        functools.partial(_merge_kernel, alpha),
        grid=(t // tm,),
        in_specs=[row(D_MODEL), row(SWA_Q), row(FOX_W), row(D_MODEL), row(D_MODEL),
                  full(wa), full(wb), full(wo), full(g), full(b)],
        out_specs=(row(D_MODEL), pl.BlockSpec((tm * ROW_CHUNKS, LANES), lambda i: (i, 0))),
        out_shape=(jax.ShapeDtypeStruct((t, D_MODEL), jnp.float32),
                   jax.ShapeDtypeStruct((t * ROW_CHUNKS, LANES), jnp.float32)),
        compiler_params=_cparams(("arbitrary",)),
        name="merge",
    )(x2, ya, yb, ga, gb, wa, wb, wo, g, b)


def _first_argmax(v, idx):
    m = jnp.max(v, axis=0, keepdims=True)
    first = jnp.min(jnp.where(v == m, idx, v.shape[0]), axis=0, keepdims=True)
    return m, first


def _router_kernel(x_ref, wr_ref, rb_ref, tri_ref, e_ref, w_ref, r_ref, cnt_ref, carry_ref):
    i = pl.program_id(0)
    tm = x_ref.shape[0]

    @pl.when(i == 0)
    def _():
        carry_ref[...] = jnp.zeros_like(carry_ref)

    xb = x_ref[...].astype(jnp.bfloat16)
    logits = lax.dot_general(wr_ref[...], xb, (((1,), (1,)), ((), ())),
                             preferred_element_type=jnp.float32)
    scores = _sigmoid(logits)
    biased = scores + rb_ref[...]

    in_group = lax.broadcasted_iota(jnp.int32, (GROUP_SIZE, tm), 0)
    grows = []
    for g in range(N_GROUPS):
        blk = biased[g * GROUP_SIZE:(g + 1) * GROUP_SIZE, :]
        m1, f1 = _first_argmax(blk, in_group)
        m2 = jnp.max(jnp.where(in_group == f1, -jnp.inf, blk), axis=0, keepdims=True)
        grows.append(m1 + m2)
    gscore = jnp.concatenate(grows, axis=0)

    gidx = lax.broadcasted_iota(jnp.int32, (N_GROUPS, tm), 0)
    gwork = gscore
    gsel = jnp.zeros((N_GROUPS, tm), jnp.float32)
    for _ in range(TOPK_GROUPS):
        _, first = _first_argmax(gwork, gidx)
        hit = gidx == first
        gsel = jnp.where(hit, 1.0, gsel)
        gwork = jnp.where(hit, -jnp.inf, gwork)

    eidx = lax.broadcasted_iota(jnp.int32, (N_EXPERTS, tm), 0)
    work = jnp.concatenate(
        [jnp.where(jnp.broadcast_to(gsel[g:g + 1, :], (GROUP_SIZE, tm)) > 0.0,
                   biased[g * GROUP_SIZE:(g + 1) * GROUP_SIZE, :], -jnp.inf) for g in range(N_GROUPS)], axis=0)
    picks = []
    sel = jnp.zeros((N_EXPERTS, tm), jnp.bool_)
    for _ in range(TOP_K):
        _, first = _first_argmax(work, eidx)
        hit = eidx == first
        picks.append((first, hit))
        sel = sel | hit
        work = jnp.where(hit, -jnp.inf, work)

    wsum = jnp.sum(jnp.where(sel, scores, 0.0), axis=0, keepdims=True)
    sel_b = jnp.where(sel, 1.0, 0.0).astype(jnp.bfloat16)
    before = carry_ref[...] + jnp.dot(sel_b, tri_ref[...], preferred_element_type=jnp.float32)
    e_rows, w_rows, r_rows = [], [], []
    for first, hit in picks:
        e_rows.append(first)
        sc = jnp.sum(jnp.where(hit, scores, 0.0), axis=0, keepdims=True)
        w_rows.append(sc / wsum * ROUTED_SCALE)
        r_rows.append(jnp.sum(jnp.where(hit, before, 0.0), axis=0, keepdims=True).astype(jnp.int32))
    e_ref[...] = jnp.concatenate(e_rows, axis=0)
    w_ref[...] = jnp.concatenate(w_rows, axis=0)
    r_ref[...] = jnp.concatenate(r_rows, axis=0)
    total = carry_ref[...] + jnp.sum(sel_b.astype(jnp.float32), axis=1, keepdims=True)
    carry_ref[...] = total
    cnt_ref[...] = jnp.broadcast_to(total, cnt_ref.shape).astype(jnp.int32)


def _router(x1, wr_t, rb_col):
    t = x1.shape[0]
    tm = TM_ROUTE
    tri = jnp.triu(jnp.ones((tm, tm), jnp.bfloat16), k=1)
    full = lambda a: pl.BlockSpec(a.shape, lambda i: (0,) * a.ndim)
    kt = pl.BlockSpec((TOP_K, tm), lambda i: (0, i))
    return pl.pallas_call(
        _router_kernel,
        grid=(t // tm,),
        in_specs=[pl.BlockSpec((tm, D_MODEL), lambda i: (i, 0)), full(wr_t), full(rb_col), full(tri)],
        out_specs=(kt, kt, kt, pl.BlockSpec((N_EXPERTS, LANES), lambda i: (0, 0))),
        out_shape=(jax.ShapeDtypeStruct((TOP_K, t), jnp.int32), jax.ShapeDtypeStruct((TOP_K, t), jnp.float32),
                   jax.ShapeDtypeStruct((TOP_K, t), jnp.int32),
                   jax.ShapeDtypeStruct((N_EXPERTS, LANES), jnp.int32)),
        scratch_shapes=[pltpu.VMEM((N_EXPERTS, 1), jnp.float32)],
        compiler_params=_cparams(("arbitrary",)),
        name="router",
    )(x1, wr_t, rb_col, tri)


def _row_copy(src, src_row, dst, dst_row, sem):
    return pltpu.make_async_copy(src.at[pl.ds(src_row * ROW_CHUNKS, ROW_CHUNKS), :],
                                 dst.at[pl.ds(dst_row * ROW_CHUNKS, ROW_CHUNKS), :], sem)


def _dispatch_kernel(pend_ref, padded_ref, dest_hbm, x_hbm, xs_hbm, idx_smem, zero_vmem, sem, zsem):
    i = pl.program_id(0)
    blk = EXPERT_BLOCK * ROW_CHUNKS

    def zero_copy(e):
        start = pl.multiple_of((pend_ref[e] - EXPERT_BLOCK) * ROW_CHUNKS, blk)
        return pltpu.make_async_copy(zero_vmem, xs_hbm.at[pl.ds(start, blk), :], zsem)

    @pl.when(i == 0)
    def _():
        zero_vmem[...] = jnp.zeros_like(zero_vmem)

        def fill(e, c):
            @pl.when(padded_ref[e] > 0)
            def _():
                zero_copy(e).start()
            return c
        lax.fori_loop(0, N_EXPERTS, fill, 0)

        def drain(e, c):
            @pl.when(padded_ref[e] > 0)
            def _():
                zero_copy(e).wait()
            return c
        lax.fori_loop(0, N_EXPERTS, drain, 0)

    n_idx = TM_MOVE * TOP_K
    pltpu.sync_copy(dest_hbm.at[pl.ds(i * n_idx, n_idx)], idx_smem)

    def issue(t, c):
        for k in range(TOP_K):
            _row_copy(x_hbm, i * TM_MOVE + t, xs_hbm, idx_smem[t * TOP_K + k], sem).start()
        return c
    lax.fori_loop(0, TM_MOVE, issue, 0)

    def drain_rows(t, c):
        for k in range(TOP_K):
            _row_copy(x_hbm, 0, xs_hbm, 0, sem).wait()
        return c
    lax.fori_loop(0, TM_MOVE, drain_rows, 0)


def _dispatch(dest_flat, x1t, pend, padded, n_rows):
    t = x1t.shape[0] // ROW_CHUNKS
    grid_spec = pltpu.PrefetchScalarGridSpec(
        num_scalar_prefetch=2,
        grid=(t // TM_MOVE,),
        in_specs=[pl.BlockSpec(memory_space=pl.ANY), pl.BlockSpec(memory_space=pl.ANY)],
        out_specs=pl.BlockSpec(memory_space=pl.ANY),
        scratch_shapes=[pltpu.SMEM((TM_MOVE * TOP_K,), jnp.int32),
                        pltpu.VMEM((EXPERT_BLOCK * ROW_CHUNKS, LANES), jnp.float32),
                        pltpu.SemaphoreType.DMA, pltpu.SemaphoreType.DMA])
    return pl.pallas_call(
        _dispatch_kernel,
        grid_spec=grid_spec,
        out_shape=jax.ShapeDtypeStruct((n_rows * ROW_CHUNKS, LANES), jnp.float32),
        compiler_params=_cparams(("arbitrary",)),
        name="dispatch",
    )(pend, padded, dest_flat, x1t)


def _experts_kernel(be_ref, nu_ref, xs_ref, wg_ref, wu_ref, wd_ref, y_ref):
    i = pl.program_id(0)

    @pl.when(i < nu_ref[0])
    def _():
        xb = _load_token_tiles(xs_ref, EXPERT_BLOCK).astype(jnp.bfloat16)
        g = jnp.dot(xb, wg_ref[...].astype(jnp.bfloat16), preferred_element_type=jnp.float32)
        u = jnp.dot(xb, wu_ref[...].astype(jnp.bfloat16), preferred_element_type=jnp.float32)
        h = g * _sigmoid(g) * u
        y = jnp.dot(h.astype(jnp.bfloat16), wd_ref[...].astype(jnp.bfloat16),
                    preferred_element_type=jnp.float32)
        _store_token_tiles(y_ref, y)


def _experts(layer, block_e, n_used, xs, w_gate_e, w_up_e, w_down_e):
    n_blocks = xs.shape[0] // (EXPERT_BLOCK * ROW_CHUNKS)
    rows = lambda i, be, nu: (jnp.minimum(i, nu[0] - 1), 0)
    wsel = lambda i, be, nu: (layer, be[i], 0, 0)
    grid_spec = pltpu.PrefetchScalarGridSpec(
        num_scalar_prefetch=2,
        grid=(n_blocks,),
        in_specs=[pl.BlockSpec((EXPERT_BLOCK * ROW_CHUNKS, LANES), rows),
                  pl.BlockSpec((None, None, D_MODEL, D_EXPERT), wsel),
                  pl.BlockSpec((None, None, D_MODEL, D_EXPERT), wsel),
                  pl.BlockSpec((None, None, D_EXPERT, D_MODEL), wsel)],
        out_specs=pl.BlockSpec((EXPERT_BLOCK * ROW_CHUNKS, LANES), rows))
    return pl.pallas_call(
        _experts_kernel,
        grid_spec=grid_spec,
        out_shape=jax.ShapeDtypeStruct(xs.shape, jnp.float32),
        compiler_params=_cparams(("arbitrary",)),
        name="experts",
    )(block_e, n_used, xs, w_gate_e, w_up_e, w_down_e)


def _combine_kernel(alpha, dest_hbm, y_hbm, w_ref, x1_ref, wg_ref, wu_ref, wd_ref, g_ref, b_ref,
                    o_ref, idx_smem, buf, sem):
    i = pl.program_id(0)
    n = pl.num_programs(0)
    n_idx = TM_MOVE * TOP_K

    def start_gather(step, slot):
        pltpu.sync_copy(dest_hbm.at[pl.ds(step * n_idx, n_idx)], idx_smem.at[slot])

        def issue(t, c):
            for k in range(TOP_K):
                pltpu.make_async_copy(
                    y_hbm.at[pl.ds(idx_smem[slot, t * TOP_K + k] * ROW_CHUNKS, ROW_CHUNKS), :],
                    buf.at[slot, k, pl.ds(t * ROW_CHUNKS, ROW_CHUNKS), :], sem.at[slot]).start()
            return c
        lax.fori_loop(0, TM_MOVE, issue, 0)

    @pl.when(i == 0)
    def _():
        start_gather(0, 0)

    @pl.when(i + 1 < n)
    def _():
        start_gather(i + 1, (i + 1) % 2)

    slot = i % 2

    def drain(t, c):
        for k in range(TOP_K):
            pltpu.make_async_copy(y_hbm.at[pl.ds(0, ROW_CHUNKS), :],
                                  buf.at[slot, k, pl.ds(0, ROW_CHUNKS), :], sem.at[slot]).wait()
        return c
    lax.fori_loop(0, TM_MOVE, drain, 0)

    w = w_ref[...]
    routed = jnp.zeros((TM_MOVE, D_MODEL), jnp.float32)
    for k in range(TOP_K):
        routed = routed + w[:, k:k + 1] * _load_token_tiles(buf, TM_MOVE, lead=(slot, k))

    x1 = x1_ref[...]
    xb = x1.astype(jnp.bfloat16)
    gs = jnp.dot(xb, wg_ref[...], preferred_element_type=jnp.float32)
    us = jnp.dot(xb, wu_ref[...], preferred_element_type=jnp.float32)
    hs = gs * _sigmoid(gs) * us
    shared = jnp.dot(hs.astype(jnp.bfloat16), wd_ref[...], preferred_element_type=jnp.float32)
    o_ref[...] = _layer_norm(alpha * x1 + (shared + routed), g_ref[...], b_ref[...])


def _combine(alpha, dest_flat, y, w_tk, x1, wg, wu, wd, g, b):
    t = x1.shape[0]
    tm = TM_MOVE
    full = lambda a: pl.BlockSpec(a.shape, lambda i: (0,) * a.ndim)
    return pl.pallas_call(
        functools.partial(_combine_kernel, alpha),
        grid=(t // tm,),
        in_specs=[pl.BlockSpec(memory_space=pl.ANY), pl.BlockSpec(memory_space=pl.ANY),
                  pl.BlockSpec((tm, TOP_K), lambda i: (i, 0)), pl.BlockSpec((tm, D_MODEL), lambda i: (i, 0)),
                  full(wg), full(wu), full(wd), full(g), full(b)],
        out_specs=pl.BlockSpec((tm, D_MODEL), lambda i: (i, 0)),
        out_shape=jax.ShapeDtypeStruct((t, D_MODEL), jnp.float32),
        scratch_shapes=[pltpu.SMEM((2, tm * TOP_K), jnp.int32),
                        pltpu.VMEM((2, TOP_K, tm * ROW_CHUNKS, LANES), jnp.float32),
                        pltpu.SemaphoreType.DMA((2,))],
        compiler_params=_cparams(("arbitrary",)),
        name="combine",
    )(dest_flat, y, w_tk, x1, wg, wu, wd, g, b)


def _prep_in_weights(w, b_forget):
    o = np.cumsum((0,) + IN_SIZES)
    group = SWA_Q_HEADS // SWA_KV_HEADS
    qa = w[:, o[0]:o[1]].reshape(D_MODEL, SWA_KV_HEADS, group, HEAD_DIM)
    qa = qa.transpose(0, 2, 1, 3).reshape(D_MODEL, SWA_Q)
    zf = jnp.pad(w[:, o[6]:o[7]], ((0, 0), (0, LANES - FOX_HEADS)))
    w_main = jnp.concatenate([qa, w[:, o[1]:o[6]], w[:, o[7]:o[9]], zf], axis=1).astype(jnp.bfloat16)
    bf_row = jnp.pad(b_forget, (0, LANES - FOX_HEADS)).reshape(1, LANES).astype(jnp.float32)
    return w_main, bf_row


def _prep_proj_a(w):
    group = SWA_Q_HEADS // SWA_KV_HEADS
    w4 = w.reshape(SWA_KV_HEADS, group, HEAD_DIM, D_MODEL).transpose(1, 0, 2, 3)
    return w4.reshape(SWA_Q, D_MODEL).astype(jnp.bfloat16)


def _route_plan(counts, top_e, rank):
    t = top_e.shape[1]
    n_assign = t * TOP_K
    n_rows = (n_assign + N_EXPERTS * (EXPERT_BLOCK - 1) + EXPERT_BLOCK - 1) // EXPERT_BLOCK * EXPERT_BLOCK
    n_blocks = n_rows // EXPERT_BLOCK
    padded = (counts + EXPERT_BLOCK - 1) // EXPERT_BLOCK * EXPERT_BLOCK
    pend = jnp.cumsum(padded)
    pstart = pend - padded
    dest = pstart[top_e] + rank
    n_used = pend[-1] // EXPERT_BLOCK
    blk = jnp.minimum(jnp.arange(n_blocks, dtype=jnp.int32), n_used - 1) * EXPERT_BLOCK
    block_e = jnp.minimum(jnp.searchsorted(pend, blk, side='right'), N_EXPERTS - 1).astype(jnp.int32)
    return (dest.T.reshape(-1).astype(jnp.int32), block_e, n_used.reshape(1).astype(jnp.int32),
            pend.astype(jnp.int32), padded.astype(jnp.int32), n_rows)


def kernel(x, w_in, b_forget, attn_sinks, rel_bias, w_proj_a, w_proj_b, w_out, ln1_g, ln1_b,
           w_router, router_bias, w_gate_e, w_up_e, w_down_e, w_gate_s, w_up_s, w_down_s, ln2_g, ln2_b):
    batch, seq_len, d = x.shape
    depth = w_in.shape[0]
    alpha = (2 * depth) ** 0.25
    t = batch * seq_len
    bf16 = jnp.bfloat16
    x2 = x.reshape(t, d)
    bias_tab = _t5_bias_table(rel_bias)
    group = SWA_Q_HEADS // SWA_KV_HEADS
    for l in range(depth):
        w_main, bf_row = _prep_in_weights(w_in[l], b_forget[l])
        qa, ka, va, qf, kf, vf, ga, gb = _in_proj(x2, w_main, bf_row, seq_len)
        ya = _swa(qa, ka, va, attn_sinks[l].astype(jnp.float32), bias_tab, seq_len)
        yb = _fox(qf, kf, vf, batch, seq_len)
        x1, x1t = _merge(alpha, x2, ya, yb, ga, gb, _prep_proj_a(w_proj_a[l]), w_proj_b[l].astype(bf16),
                         w_out[l].astype(bf16), ln1_g[l].reshape(1, d), ln1_b[l].reshape(1, d))
        top_e, w_kt, rank, counts = _router(x1, w_router[l].T.astype(bf16),
                                            router_bias[l].reshape(N_EXPERTS, 1).astype(jnp.float32))
        dest_flat, block_e, n_used, pend, padded, n_rows = _route_plan(counts[:, 0], top_e, rank)
        xs = _dispatch(dest_flat, x1t, pend, padded, n_rows)
        y = _experts(l, block_e, n_used, xs, w_gate_e, w_up_e, w_down_e)
        x2 = _combine(alpha, dest_flat, y, w_kt.T, x1, w_gate_s[l].astype(bf16), w_up_s[l].astype(bf16),
                      w_down_s[l].astype(bf16), ln2_g[l].reshape(1, d), ln2_b[l].reshape(1, d))
    return x2.reshape(batch, seq_len, d)
```

```python
import functools
import math

import jax
import jax.numpy as jnp
import numpy as np
from jax import lax
from jax.experimental import pallas as pl
from jax.experimental.pallas import tpu as pltpu

D_MODEL = 1024
HEAD_DIM = 64
SWA_Q_HEADS = 8
SWA_KV_HEADS = 2
SWA_WINDOW = 128
FOX_HEADS = 8
Q_BLOCK = 128
N_BUCKETS = 32
MAX_DISTANCE = 128
N_EXPERTS = 256
TOP_K = 8
N_GROUPS = 8
TOPK_GROUPS = 4
GROUP_SIZE = N_EXPERTS // N_GROUPS
D_EXPERT = 256
D_SHARED = 256
ROUTED_SCALE = 2.5
LN_EPS = 1e-5

SWA_Q = SWA_Q_HEADS * HEAD_DIM
SWA_KV = SWA_KV_HEADS * HEAD_DIM
FOX_W = FOX_HEADS * HEAD_DIM
IN_SIZES = (SWA_Q, SWA_KV, SWA_KV, FOX_W, FOX_W, FOX_W, FOX_HEADS, D_MODEL, D_MODEL)

LANES = 128
SUBLANES = 8
ROW_CHUNKS = D_MODEL // LANES
VMEM_LIMIT = 56 * 1024 * 1024

TM_PROJ = 512
TM_MERGE = 256
TM_ROUTE = 256
TM_MOVE = 256
TM_PLAN = 2048
EXPERT_BLOCK = 256
FOX_TQ = 512
FOX_TK = 512

NEG_BIG = -1e30
AUG = HEAD_DIM


def _cparams(sem):
    return pltpu.CompilerParams(dimension_semantics=sem, vmem_limit_bytes=VMEM_LIMIT)


def _sigmoid(x):
    return 1.0 / (1.0 + jnp.exp(-x))


def _split3(c):
    hi = c.astype(jnp.bfloat16).astype(jnp.float32)
    r = c - hi
    mid = r.astype(jnp.bfloat16).astype(jnp.float32)
    lo = (r - mid).astype(jnp.bfloat16).astype(jnp.float32)
    return hi, mid, lo


_C_QA = 0
_C_KA = _C_QA + SWA_Q
_C_VA = _C_KA + SWA_KV
_C_QF = _C_VA + SWA_KV
_C_KF = _C_QF + FOX_W
_C_VF = _C_KF + FOX_W
_C_GA = _C_VF + FOX_W
_C_GB = _C_GA + D_MODEL
_C_ZF = _C_GB + D_MODEL
_N_PROJ = _C_ZF + LANES


def _in_proj_kernel(tiles_per_seq, x_ref, w_ref, bf_ref, tri_ref,
                    qa_ref, ka_ref, va_ref, qf_ref, kf_ref, vf_ref, ga_ref, gb_ref, carry_ref):
    i = pl.program_id(0)
    tm = x_ref.shape[0]
    xb = x_ref[...].astype(jnp.bfloat16)

    def proj(c0, n):
        return jnp.dot(xb, w_ref[:, c0:c0 + n], preferred_element_type=jnp.float32)

    scale = HEAD_DIM ** -0.5
    qa_ref[...] = (proj(_C_QA, SWA_Q) * scale).astype(jnp.bfloat16)
    ka_ref[...] = proj(_C_KA, SWA_KV).astype(jnp.bfloat16)
    va_ref[...] = proj(_C_VA, SWA_KV).astype(jnp.bfloat16)
    vf_ref[...] = proj(_C_VF, FOX_W).astype(jnp.bfloat16)
    ga_ref[...] = proj(_C_GA, D_MODEL)
    gb_ref[...] = proj(_C_GB, D_MODEL)

    z = proj(_C_ZF, LANES) + bf_ref[...]
    log_f = jnp.minimum(z, 0.0) - jnp.log(1.0 + jnp.exp(-jnp.abs(z)))

    @pl.when(i % tiles_per_seq == 0)
    def _():
        carry_ref[...] = jnp.zeros_like(carry_ref)

    tri = tri_ref[...]
    h3 = _split3(log_f)
    c = carry_ref[...]
    for part in h3:
        c = c + jnp.dot(tri, part.astype(jnp.bfloat16), preferred_element_type=jnp.float32)
    carry_ref[...] = c[tm - 1:tm, :]

    qf = proj(_C_QF, FOX_W) * scale
    kf = proj(_C_KF, FOX_W)
    lane = lax.broadcasted_iota(jnp.int32, (tm, LANES), 1)
    for h in range(FOX_HEADS):
        pair = (h // 2) * LANES
        qp = qf[:, pair:pair + LANES]
        kp = kf[:, pair:pair + LANES]
        if h % 2 == 1:
            qp = pltpu.roll(qp, HEAD_DIM, axis=1)
            kp = pltpu.roll(kp, HEAD_DIM, axis=1)
        ch = jnp.broadcast_to(c[:, h:h + 1], (tm, LANES))
        hi, mid, lo = _split3(ch)
        one = jnp.float32(1.0)
        zero = jnp.float32(0.0)
        q_aug = jnp.where(lane == AUG, hi, jnp.where(lane == AUG + 1, mid, jnp.where(
            lane == AUG + 2, lo, jnp.where(lane < AUG + 6, one, zero))))
        k_aug = jnp.where(lane < AUG + 3, one, jnp.where(lane == AUG + 3, -hi, jnp.where(
            lane == AUG + 4, -mid, jnp.where(lane == AUG + 5, -lo, zero))))
        qf_ref[h] = jnp.where(lane < AUG, qp, q_aug).astype(jnp.bfloat16)
        kf_ref[h] = jnp.where(lane < AUG, kp, k_aug).astype(jnp.bfloat16)


def _in_proj(x2, w_main, bf_row, seq_len):
    t = x2.shape[0]
    tm = min(TM_PROJ, seq_len)
    tri = jnp.tril(jnp.ones((tm, tm), jnp.bfloat16))
    row = lambda n: pl.BlockSpec((tm, n), lambda i: (i, 0))
    full = lambda a: pl.BlockSpec(a.shape, lambda i: (0,) * a.ndim)
    slab = pl.BlockSpec((FOX_HEADS, tm, LANES), lambda i: (0, i, 0))
    bf16, f32 = jnp.bfloat16, jnp.float32
    out_shape = (
        jax.ShapeDtypeStruct((t, SWA_Q), bf16), jax.ShapeDtypeStruct((t, SWA_KV), bf16),
        jax.ShapeDtypeStruct((t, SWA_KV), bf16),
        jax.ShapeDtypeStruct((FOX_HEADS, t, LANES), bf16), jax.ShapeDtypeStruct((FOX_HEADS, t, LANES), bf16),
        jax.ShapeDtypeStruct((t, FOX_W), bf16),
        jax.ShapeDtypeStruct((t, D_MODEL), f32), jax.ShapeDtypeStruct((t, D_MODEL), f32))
    return pl.pallas_call(
        functools.partial(_in_proj_kernel, seq_len // tm),
        grid=(t // tm,),
        in_specs=[row(D_MODEL), full(w_main), full(bf_row), full(tri)],
        out_specs=(row(SWA_Q), row(SWA_KV), row(SWA_KV), slab, slab, row(FOX_W), row(D_MODEL), row(D_MODEL)),
        out_shape=out_shape,
        scratch_shapes=[pltpu.VMEM((1, LANES), f32)],
        compiler_params=_cparams(("arbitrary",)),
        name="in_proj",
    )(x2, w_main, bf_row, tri)


def _swa_kernel(blocks_per_seq, sink_ref, q_ref, kp_ref, kc_ref, vp_ref, vc_ref, bias_ref, o_ref):
    n = pl.program_id(0)
    first_key = jnp.where((n % blocks_per_seq) > 0, 0, Q_BLOCK)
    kk = jnp.concatenate([kp_ref[...], kc_ref[...]], axis=0)
    vv = jnp.concatenate([vp_ref[...], vc_ref[...]], axis=0)
    qi = lax.broadcasted_iota(jnp.int32, (Q_BLOCK, 2 * Q_BLOCK), 0)
    kj = lax.broadcasted_iota(jnp.int32, (Q_BLOCK, 2 * Q_BLOCK), 1)
    dist = qi + Q_BLOCK - kj
    valid = (dist >= 0) & (dist < SWA_WINDOW) & (kj >= first_key)
    lane = lax.broadcasted_iota(jnp.int32, (Q_BLOCK, LANES), 1)
    lo_half = lane < HEAD_DIM
    group = SWA_Q_HEADS // SWA_KV_HEADS
    for j in range(group):
        qs = q_ref[:, j * LANES:(j + 1) * LANES]
        outs = []
        for kv in range(SWA_KV_HEADS):
            head = kv * group + j
            qm = jnp.where(lo_half if kv == 0 else ~lo_half, qs, jnp.zeros_like(qs))
            s = lax.dot_general(qm, kk, (((1,), (1,)), ((), ())), preferred_element_type=jnp.float32)
            s = jnp.where(valid, s + bias_ref[head], -jnp.inf)
            sink = sink_ref[head]
            m = jnp.maximum(jnp.max(s, axis=-1, keepdims=True), sink)
            p = jnp.exp(s - m)
            denom = jnp.sum(p, axis=-1, keepdims=True) + jnp.exp(sink - m)
            pn = (p / denom).astype(jnp.bfloat16)
            outs.append(jnp.dot(pn, vv, preferred_element_type=jnp.float32))
        o_ref[:, j * LANES:(j + 1) * LANES] = jnp.where(lo_half, outs[0], outs[1]).astype(o_ref.dtype)


def _swa(qa, ka, va, sinks, bias_tab, seq_len):
    t = qa.shape[0]
    bps = seq_len // Q_BLOCK
    cur = lambda n, s: (n, 0)
    prev = lambda n, s: (jnp.maximum(n - 1, 0), 0)
    grid_spec = pltpu.PrefetchScalarGridSpec(
        num_scalar_prefetch=1,
        grid=(t // Q_BLOCK,),
        in_specs=[pl.BlockSpec((Q_BLOCK, SWA_Q), cur),
                  pl.BlockSpec((Q_BLOCK, SWA_KV), prev), pl.BlockSpec((Q_BLOCK, SWA_KV), cur),
                  pl.BlockSpec((Q_BLOCK, SWA_KV), prev), pl.BlockSpec((Q_BLOCK, SWA_KV), cur),
                  pl.BlockSpec(bias_tab.shape, lambda n, s: (0, 0, 0))],
        out_specs=pl.BlockSpec((Q_BLOCK, SWA_Q), cur))
    return pl.pallas_call(
        functools.partial(_swa_kernel, bps),
        grid_spec=grid_spec,
        out_shape=jax.ShapeDtypeStruct((t, SWA_Q), jnp.bfloat16),
        compiler_params=_cparams(("arbitrary",)),
        name="swa",
    )(sinks, qa, ka, ka, va, va, bias_tab)


def _t5_bias_table(rel_bias):
    i = np.arange(Q_BLOCK)[:, None]
    j = np.arange(2 * Q_BLOCK)[None, :]
    n = np.maximum(i + Q_BLOCK - j, 0)
    max_exact = N_BUCKETS // 2
    nf = jnp.maximum(jnp.asarray(n), 1).astype(jnp.float32)
    large = max_exact + (jnp.log(nf / max_exact) / math.log(MAX_DISTANCE / max_exact)
                         * (N_BUCKETS - max_exact)).astype(jnp.int32)
    large = jnp.minimum(large, N_BUCKETS - 1)
    bucket = jnp.where(jnp.asarray(n) < max_exact, jnp.asarray(n), large)
    return rel_bias[bucket].astype(jnp.float32).transpose(2, 0, 1)


def _fox_kernel(tk, q_ref, k_ref, v_ref, o_ref):
    qi = pl.program_id(2)
    tq = q_ref.shape[1]
    lane = lax.broadcasted_iota(jnp.int32, (tq, LANES), 1)
    row = lax.broadcasted_iota(jnp.int32, (tq, tk), 0) + qi * tq
    col = lax.broadcasted_iota(jnp.int32, (tq, tk), 1)
    qs = (q_ref[0], q_ref[1])

    def step(j, carry, masked):
        start = pl.multiple_of(j * tk, tk)
        v = v_ref[pl.ds(start, tk), :]
        new = []
        for hh in range(2):
            m, l, acc = carry[hh]
            k = k_ref[hh, pl.ds(start, tk), :]
            s = lax.dot_general(qs[hh], k, (((1,), (1,)), ((), ())), preferred_element_type=jnp.float32)
            if masked:
                s = jnp.where(col + j * tk <= row, s, -jnp.inf)
            m_new = jnp.maximum(m, jnp.max(s, axis=-1, keepdims=True))
            a = jnp.exp(m - m_new)
            p = jnp.exp(s - m_new)
            l = a * l + jnp.sum(p, axis=-1, keepdims=True)
            acc = a * acc + jnp.dot(p.astype(jnp.bfloat16), v, preferred_element_type=jnp.float32)
            new.append((m_new, l, acc))
        return tuple(new)

    init1 = (jnp.full((tq, 1), -jnp.inf, jnp.float32), jnp.zeros((tq, 1), jnp.float32),
             jnp.zeros((tq, LANES), jnp.float32))
    n_full = (qi * tq) // tk
    carry = lax.fori_loop(0, n_full, functools.partial(step, masked=False), (init1, init1))
    for d in range(max(1, tq // tk)):
        carry = step(n_full + d, carry, masked=True)
    outs = [acc / l for _, l, acc in carry]
    o_ref[...] = jnp.where(lane < HEAD_DIM, outs[0], outs[1]).astype(o_ref.dtype)


def _fox(qf, kf, vf, batch, seq_len):
    t = vf.shape[0]
    tq = min(FOX_TQ, seq_len)
    tk = min(FOX_TK, seq_len)
    assert tq % tk == 0 or tk % tq == 0
    nq = seq_len // tq
    pairs = FOX_HEADS // 2
    return pl.pallas_call(
        functools.partial(_fox_kernel, tk),
        grid=(batch, pairs, nq),
        in_specs=[pl.BlockSpec((2, tq, LANES), lambda b, hp, i: (hp, b * nq + i, 0)),
                  pl.BlockSpec((2, seq_len, LANES), lambda b, hp, i: (hp, b, 0)),
                  pl.BlockSpec((seq_len, LANES), lambda b, hp, i: (b, hp))],
        out_specs=pl.BlockSpec((tq, LANES), lambda b, hp, i: (b * nq + i, hp)),
        out_shape=jax.ShapeDtypeStruct((t, FOX_W), jnp.bfloat16),
        compiler_params=_cparams(("arbitrary", "arbitrary", "arbitrary")),
        name="fox",
    )(qf, kf, vf)


def _layer_norm(h, g, b):
    mu = jnp.mean(h, axis=-1, keepdims=True)
    d = h - mu
    var = jnp.mean(d * d, axis=-1, keepdims=True)
    return d * lax.rsqrt(var + LN_EPS) * g + b


def _store_token_tiles(ref, x):
    rows = x.shape[0]
    for c in range(ROW_CHUNKS):
        ref[pl.ds(c, rows, stride=ROW_CHUNKS), :] = x[:, c * LANES:(c + 1) * LANES]


def _load_token_tiles(ref, rows, lead=()):
    return jnp.concatenate(
        [ref[lead + (pl.ds(c, rows, stride=ROW_CHUNKS), slice(None))] for c in range(ROW_CHUNKS)], axis=1)


def _merge_kernel(alpha, x_ref, ya_ref, yb_ref, ga_ref, gb_ref, wa_ref, wb_ref, wo_ref, g_ref, b_ref,
                  x1_ref, x1t_ref):
    pa = jnp.dot(ya_ref[...], wa_ref[...], preferred_element_type=jnp.float32)
    pb = jnp.dot(yb_ref[...], wb_ref[...], preferred_element_type=jnp.float32)
    mix = _sigmoid(ga_ref[...]) * pa + _sigmoid(gb_ref[...]) * pb
    out = jnp.dot(mix.astype(jnp.bfloat16), wo_ref[...], preferred_element_type=jnp.float32)
    x1 = _layer_norm(alpha * x_ref[...] + out, g_ref[...], b_ref[...])
    x1_ref[...] = x1
    _store_token_tiles(x1t_ref, x1)


def _merge(alpha, x2, ya, yb, ga, gb, wa, wb, wo, g, b):
    t = x2.shape[0]
    tm = TM_MERGE
    row = lambda n: pl.BlockSpec((tm, n), lambda i: (i, 0))
    full = lambda a: pl.BlockSpec(a.shape, lambda i: (0,) * a.ndim)
    return pl.pallas_call(
        functools.partial(_merge_kernel, alpha),
        grid=(t // tm,),
        in_specs=[row(D_MODEL), row(SWA_Q), row(FOX_W), row(D_MODEL), row(D_MODEL),
                  full(wa), full(wb), full(wo), full(g), full(b)],
        out_specs=(row(D_MODEL), pl.BlockSpec((tm * ROW_CHUNKS, LANES), lambda i: (i, 0))),
        out_shape=(jax.ShapeDtypeStruct((t, D_MODEL), jnp.float32),
                   jax.ShapeDtypeStruct((t * ROW_CHUNKS, LANES), jnp.float32)),
        compiler_params=_cparams(("arbitrary",)),
        name="merge",
    )(x2, ya, yb, ga, gb, wa, wb, wo, g, b)


def _first_argmax(v, idx):
    m = jnp.max(v, axis=0, keepdims=True)
    first = jnp.min(jnp.where(v == m, idx, v.shape[0]), axis=0, keepdims=True)
    return m, first


def _router_kernel(x_ref, wr_ref, rb_ref, tri_ref, e_ref, w_ref, r_ref, cnt_ref, carry_ref):
    i = pl.program_id(0)
    tm = x_ref.shape[0]

    @pl.when(i == 0)
    def _():
        carry_ref[...] = jnp.zeros_like(carry_ref)

    xb = x_ref[...].astype(jnp.bfloat16)
    logits = lax.dot_general(wr_ref[...], xb, (((1,), (1,)), ((), ())),
                             preferred_element_type=jnp.float32)
    scores = _sigmoid(logits)
    biased = scores + rb_ref[...]

    in_group = lax.broadcasted_iota(jnp.int32, (GROUP_SIZE, tm), 0)
    grows = []
    for g in range(N_GROUPS):
        blk = biased[g * GROUP_SIZE:(g + 1) * GROUP_SIZE, :]
        m1, f1 = _first_argmax(blk, in_group)
        m2 = jnp.max(jnp.where(in_group == f1, -jnp.inf, blk), axis=0, keepdims=True)
        grows.append(m1 + m2)
    gscore = jnp.concatenate(grows, axis=0)

    gidx = lax.broadcasted_iota(jnp.int32, (N_GROUPS, tm), 0)
    gwork = gscore
    gsel = jnp.zeros((N_GROUPS, tm), jnp.float32)
    for _ in range(TOPK_GROUPS):
        _, first = _first_argmax(gwork, gidx)
        hit = gidx == first
        gsel = jnp.where(hit, 1.0, gsel)
        gwork = jnp.where(hit, -jnp.inf, gwork)

    eidx = lax.broadcasted_iota(jnp.int32, (N_EXPERTS, tm), 0)
    work = jnp.concatenate(
        [jnp.where(jnp.broadcast_to(gsel[g:g + 1, :], (GROUP_SIZE, tm)) > 0.0,
                   biased[g * GROUP_SIZE:(g + 1) * GROUP_SIZE, :], -jnp.inf) for g in range(N_GROUPS)], axis=0)
    picks = []
    sel = jnp.zeros((N_EXPERTS, tm), jnp.bool_)
    for _ in range(TOP_K):
        _, first = _first_argmax(work, eidx)
        hit = eidx == first
        picks.append((first, hit))
        sel = sel | hit
        work = jnp.where(hit, -jnp.inf, work)

    wsum = jnp.sum(jnp.where(sel, scores, 0.0), axis=0, keepdims=True)
    sel_b = jnp.where(sel, 1.0, 0.0).astype(jnp.bfloat16)
    before = carry_ref[...] + jnp.dot(sel_b, tri_ref[...], preferred_element_type=jnp.float32)
    e_rows, w_rows, r_rows = [], [], []
    for first, hit in picks:
        e_rows.append(first)
        sc = jnp.sum(jnp.where(hit, scores, 0.0), axis=0, keepdims=True)
        w_rows.append(sc / wsum * ROUTED_SCALE)
        r_rows.append(jnp.sum(jnp.where(hit, before, 0.0), axis=0, keepdims=True).astype(jnp.int32))
    e_ref[...] = jnp.concatenate(e_rows, axis=0)
    w_ref[...] = jnp.concatenate(w_rows, axis=0)
    r_ref[...] = jnp.concatenate(r_rows, axis=0)
    total = carry_ref[...] + jnp.sum(sel_b.astype(jnp.float32), axis=1, keepdims=True)
    carry_ref[...] = total
    cnt_ref[...] = jnp.broadcast_to(total, cnt_ref.shape).astype(jnp.int32)


def _router(x1, wr_t, rb_col):
    t = x1.shape[0]
    tm = TM_ROUTE
    tri = jnp.triu(jnp.ones((tm, tm), jnp.bfloat16), k=1)
    full = lambda a: pl.BlockSpec(a.shape, lambda i: (0,) * a.ndim)
    kt = pl.BlockSpec((TOP_K, tm), lambda i: (0, i))
    return pl.pallas_call(
        _router_kernel,
        grid=(t // tm,),
        in_specs=[pl.BlockSpec((tm, D_MODEL), lambda i: (i, 0)), full(wr_t), full(rb_col), full(tri)],
        out_specs=(kt, kt, kt, pl.BlockSpec((N_EXPERTS, LANES), lambda i: (0, 0))),
        out_shape=(jax.ShapeDtypeStruct((TOP_K, t), jnp.int32), jax.ShapeDtypeStruct((TOP_K, t), jnp.float32),
                   jax.ShapeDtypeStruct((TOP_K, t), jnp.int32),
                   jax.ShapeDtypeStruct((N_EXPERTS, LANES), jnp.int32)),
        scratch_shapes=[pltpu.VMEM((N_EXPERTS, 1), jnp.float32)],
        compiler_params=_cparams(("arbitrary",)),
        name="router",
    )(x1, wr_t, rb_col, tri)


def _plan_kernel(ps_ref, e_ref, r_ref, d_ref):
    tm = e_ref.shape[1]
    eidx = lax.broadcasted_iota(jnp.int32, (N_EXPERTS, tm), 0)
    rows = []
    for k in range(TOP_K):
        hit = eidx == e_ref[k:k + 1, :]
        base = jnp.sum(jnp.where(hit, ps_ref[...], 0.0), axis=0, keepdims=True)
        rows.append(base.astype(jnp.int32) + r_ref[k:k + 1, :])
    d_ref[...] = jnp.concatenate(rows, axis=0)


def _plan(pstart_col, top_e, rank):
    t = top_e.shape[1]
    tm = min(TM_PLAN, t)
    kt = pl.BlockSpec((TOP_K, tm), lambda i: (0, i))
    return pl.pallas_call(
        _plan_kernel,
        grid=(t // tm,),
        in_specs=[pl.BlockSpec(pstart_col.shape, lambda i: (0, 0)), kt, kt],
        out_specs=kt,
        out_shape=jax.ShapeDtypeStruct((TOP_K, t), jnp.int32),
        compiler_params=_cparams(("arbitrary",)),
        name="plan",
    )(pstart_col, top_e, rank)


def _dispatch_kernel(pend_ref, padded_ref, dest_hbm, x_ref, xs_hbm, idx_smem, zero_vmem, sem, zsem):
    i = pl.program_id(0)
    blk = EXPERT_BLOCK * ROW_CHUNKS

    def zero_copy(e):
        start = pl.multiple_of((pend_ref[e] - EXPERT_BLOCK) * ROW_CHUNKS, blk)
        return pltpu.make_async_copy(zero_vmem, xs_hbm.at[pl.ds(start, blk), :], zsem)

    @pl.when(i == 0)
    def _():
        zero_vmem[...] = jnp.zeros_like(zero_vmem)

        def fill(e, c):
            @pl.when(padded_ref[e] > 0)
            def _():
                zero_copy(e).start()
            return c
        lax.fori_loop(0, N_EXPERTS, fill, 0)

        def drain(e, c):
            @pl.when(padded_ref[e] > 0)
            def _():
                zero_copy(e).wait()
            return c
        lax.fori_loop(0, N_EXPERTS, drain, 0)

    n_idx = TM_MOVE * TOP_K
    pltpu.sync_copy(dest_hbm.at[pl.ds(i * n_idx, n_idx)], idx_smem)

    def issue(t, c):
        src = x_ref.at[pl.ds(pl.multiple_of(t * ROW_CHUNKS, ROW_CHUNKS), ROW_CHUNKS), :]
        for k in range(TOP_K):
            row = idx_smem[t * TOP_K + k]
            pltpu.make_async_copy(
                src, xs_hbm.at[pl.ds(pl.multiple_of(row * ROW_CHUNKS, ROW_CHUNKS), ROW_CHUNKS), :], sem).start()
        return c
    lax.fori_loop(0, TM_MOVE, issue, 0)

    for k in range(TOP_K):
        pltpu.make_async_copy(x_ref, xs_hbm.at[pl.ds(0, TM_MOVE * ROW_CHUNKS), :], sem).wait()


def _dispatch(dest_flat, x1t, pend, padded, n_rows):
    t = x1t.shape[0] // ROW_CHUNKS
    grid_spec = pltpu.PrefetchScalarGridSpec(
        num_scalar_prefetch=2,
        grid=(t // TM_MOVE,),
        in_specs=[pl.BlockSpec(memory_space=pl.ANY),
                  pl.BlockSpec((TM_MOVE * ROW_CHUNKS, LANES), lambda i, pe, pa: (i, 0))],
        out_specs=pl.BlockSpec(memory_space=pl.ANY),
        scratch_shapes=[pltpu.SMEM((TM_MOVE * TOP_K,), jnp.int32),
                        pltpu.VMEM((EXPERT_BLOCK * ROW_CHUNKS, LANES), jnp.float32),
                        pltpu.SemaphoreType.DMA, pltpu.SemaphoreType.DMA])
    return pl.pallas_call(
        _dispatch_kernel,
        grid_spec=grid_spec,
        out_shape=jax.ShapeDtypeStruct((n_rows * ROW_CHUNKS, LANES), jnp.float32),
        compiler_params=_cparams(("arbitrary",)),
        name="dispatch",
    )(pend, padded, dest_flat, x1t)


def _experts_kernel(be_ref, nu_ref, xs_ref, wg_ref, wu_ref, wd_ref, y_ref):
    i = pl.program_id(0)

    @pl.when(i < nu_ref[0])
    def _():
        xb = _load_token_tiles(xs_ref, EXPERT_BLOCK).astype(jnp.bfloat16)
        g = jnp.dot(xb, wg_ref[...].astype(jnp.bfloat16), preferred_element_type=jnp.float32)
        u = jnp.dot(xb, wu_ref[...].astype(jnp.bfloat16), preferred_element_type=jnp.float32)
        h = g * _sigmoid(g) * u
        y = jnp.dot(h.astype(jnp.bfloat16), wd_ref[...].astype(jnp.bfloat16),
                    preferred_element_type=jnp.float32)
        _store_token_tiles(y_ref, y)


def _experts(layer, block_e, n_used, xs, w_gate_e, w_up_e, w_down_e):
    n_blocks = xs.shape[0] // (EXPERT_BLOCK * ROW_CHUNKS)
    rows = lambda i, be, nu: (jnp.minimum(i, nu[0] - 1), 0)
    wsel = lambda i, be, nu: (layer, be[i], 0, 0)
    grid_spec = pltpu.PrefetchScalarGridSpec(
        num_scalar_prefetch=2,
        grid=(n_blocks,),
        in_specs=[pl.BlockSpec((EXPERT_BLOCK * ROW_CHUNKS, LANES), rows),
                  pl.BlockSpec((None, None, D_MODEL, D_EXPERT), wsel),
                  pl.BlockSpec((None, None, D_MODEL, D_EXPERT), wsel),
                  pl.BlockSpec((None, None, D_EXPERT, D_MODEL), wsel)],
        out_specs=pl.BlockSpec((EXPERT_BLOCK * ROW_CHUNKS, LANES), rows))
    return pl.pallas_call(
        _experts_kernel,
        grid_spec=grid_spec,
        out_shape=jax.ShapeDtypeStruct(xs.shape, jnp.float32),
        compiler_params=_cparams(("arbitrary",)),
        name="experts",
    )(block_e, n_used, xs, w_gate_e, w_up_e, w_down_e)


def _combine_kernel(alpha, dest_hbm, y_hbm, w_ref, x1_ref, wg_ref, wu_ref, wd_ref, g_ref, b_ref,
                    o_ref, idx_smem, buf, sem):
    i = pl.program_id(0)
    n = pl.num_programs(0)
    n_idx = TM_MOVE * TOP_K

    def start_gather(step, slot):
        pltpu.sync_copy(dest_hbm.at[pl.ds(step * n_idx, n_idx)], idx_smem.at[slot])

        def issue(t, c):
            for k in range(TOP_K):
                pltpu.make_async_copy(
                    y_hbm.at[pl.ds(idx_smem[slot, t * TOP_K + k] * ROW_CHUNKS, ROW_CHUNKS), :],
                    buf.at[slot, k, pl.ds(t * ROW_CHUNKS, ROW_CHUNKS), :], sem.at[slot]).start()
            return c
        lax.fori_loop(0, TM_MOVE, issue, 0)

    @pl.when(i == 0)
    def _():
        start_gather(0, 0)

    @pl.when(i + 1 < n)
    def _():
        start_gather(i + 1, (i + 1) % 2)

    slot = i % 2

    for k in range(TOP_K):
        pltpu.make_async_copy(y_hbm.at[pl.ds(0, TM_MOVE * ROW_CHUNKS), :], buf.at[slot, k], sem.at[slot]).wait()

    w = w_ref[...]
    routed = jnp.zeros((TM_MOVE, D_MODEL), jnp.float32)
    for k in range(TOP_K):
        routed = routed + w[:, k:k + 1] * _load_token_tiles(buf, TM_MOVE, lead=(slot, k))

    x1 = x1_ref[...]
    xb = x1.astype(jnp.bfloat16)
    gs = jnp.dot(xb, wg_ref[...], preferred_element_type=jnp.float32)
    us = jnp.dot(xb, wu_ref[...], preferred_element_type=jnp.float32)
    hs = gs * _sigmoid(gs) * us
    shared = jnp.dot(hs.astype(jnp.bfloat16), wd_ref[...], preferred_element_type=jnp.float32)
    o_ref[...] = _layer_norm(alpha * x1 + (shared + routed), g_ref[...], b_ref[...])


def _combine(alpha, dest_flat, y, w_tk, x1, wg, wu, wd, g, b):
    t = x1.shape[0]
    tm = TM_MOVE
    full = lambda a: pl.BlockSpec(a.shape, lambda i: (0,) * a.ndim)
    return pl.pallas_call(
        functools.partial(_combine_kernel, alpha),
        grid=(t // tm,),
        in_specs=[pl.BlockSpec(memory_space=pl.ANY), pl.BlockSpec(memory_space=pl.ANY),
                  pl.BlockSpec((tm, TOP_K), lambda i: (i, 0)), pl.BlockSpec((tm, D_MODEL), lambda i: (i, 0)),
                  full(wg), full(wu), full(wd), full(g), full(b)],
        out_specs=pl.BlockSpec((tm, D_MODEL), lambda i: (i, 0)),
        out_shape=jax.ShapeDtypeStruct((t, D_MODEL), jnp.float32),
        scratch_shapes=[pltpu.SMEM((2, tm * TOP_K), jnp.int32),
                        pltpu.VMEM((2, TOP_K, tm * ROW_CHUNKS, LANES), jnp.float32),
                        pltpu.SemaphoreType.DMA((2,))],
        compiler_params=_cparams(("arbitrary",)),
        name="combine",
    )(dest_flat, y, w_tk, x1, wg, wu, wd, g, b)


def _prep_in_weights(w, b_forget):
    o = np.cumsum((0,) + IN_SIZES)
    group = SWA_Q_HEADS // SWA_KV_HEADS
    qa = w[:, o[0]:o[1]].reshape(D_MODEL, SWA_KV_HEADS, group, HEAD_DIM)
    qa = qa.transpose(0, 2, 1, 3).reshape(D_MODEL, SWA_Q)
    zf = jnp.pad(w[:, o[6]:o[7]], ((0, 0), (0, LANES - FOX_HEADS)))
    w_main = jnp.concatenate([qa, w[:, o[1]:o[6]], w[:, o[7]:o[9]], zf], axis=1).astype(jnp.bfloat16)
    bf_row = jnp.pad(b_forget, (0, LANES - FOX_HEADS)).reshape(1, LANES).astype(jnp.float32)
    return w_main, bf_row


def _prep_proj_a(w):
    group = SWA_Q_HEADS // SWA_KV_HEADS
    w4 = w.reshape(SWA_KV_HEADS, group, HEAD_DIM, D_MODEL).transpose(1, 0, 2, 3)
    return w4.reshape(SWA_Q, D_MODEL).astype(jnp.bfloat16)


def _route_plan(counts, n_tokens):
    n_assign = n_tokens * TOP_K
    n_rows = (n_assign + N_EXPERTS * (EXPERT_BLOCK - 1) + EXPERT_BLOCK - 1) // EXPERT_BLOCK * EXPERT_BLOCK
    n_blocks = n_rows // EXPERT_BLOCK
    padded = (counts + EXPERT_BLOCK - 1) // EXPERT_BLOCK * EXPERT_BLOCK
    pend = jnp.cumsum(padded)
    pstart = pend - padded
    n_used = pend[-1] // EXPERT_BLOCK
    blk = jnp.minimum(jnp.arange(n_blocks, dtype=jnp.int32), n_used - 1) * EXPERT_BLOCK
    block_e = jnp.minimum(jnp.searchsorted(pend, blk, side='right'), N_EXPERTS - 1).astype(jnp.int32)
    return (pstart.astype(jnp.float32).reshape(N_EXPERTS, 1), block_e, n_used.reshape(1).astype(jnp.int32),
            pend.astype(jnp.int32), padded.astype(jnp.int32), n_rows)


def kernel(x, w_in, b_forget, attn_sinks, rel_bias, w_proj_a, w_proj_b, w_out, ln1_g, ln1_b,
           w_router, router_bias, w_gate_e, w_up_e, w_down_e, w_gate_s, w_up_s, w_down_s, ln2_g, ln2_b):
    batch, seq_len, d = x.shape
    depth = w_in.shape[0]
    alpha = (2 * depth) ** 0.25
    t = batch * seq_len
    bf16 = jnp.bfloat16
    x2 = x.reshape(t, d)
    bias_tab = _t5_bias_table(rel_bias)
    group = SWA_Q_HEADS // SWA_KV_HEADS
    for l in range(depth):
        w_main, bf_row = _prep_in_weights(w_in[l], b_forget[l])
        qa, ka, va, qf, kf, vf, ga, gb = _in_proj(x2, w_main, bf_row, seq_len)
        ya = _swa(qa, ka, va, attn_sinks[l].astype(jnp.float32), bias_tab, seq_len)
        yb = _fox(qf, kf, vf, batch, seq_len)
        x1, x1t = _merge(alpha, x2, ya, yb, ga, gb, _prep_proj_a(w_proj_a[l]), w_proj_b[l].astype(bf16),
                         w_out[l].astype(bf16), ln1_g[l].reshape(1, d), ln1_b[l].reshape(1, d))
        top_e, w_kt, rank, counts = _router(x1, w_router[l].T.astype(bf16),
                                            router_bias[l].reshape(N_EXPERTS, 1).astype(jnp.float32))
        pstart_col, block_e, n_used, pend, padded, n_rows = _route_plan(counts[:, 0], t)
        dest_flat = _plan(pstart_col, top_e, rank).T.reshape(-1)
        xs = _dispatch(dest_flat, x1t, pend, padded, n_rows)
        y = _experts(l, block_e, n_used, xs, w_gate_e, w_up_e, w_down_e)
        x2 = _combine(alpha, dest_flat, y, w_kt.T, x1, w_gate_s[l].astype(bf16), w_up_s[l].astype(bf16),
                      w_down_s[l].astype(bf16), ln2_g[l].reshape(1, d), ln2_b[l].reshape(1, d))
    return x2.reshape(batch, seq_len, d)
```

```python
import functools
import math

import jax
import jax.numpy as jnp
import numpy as np
from jax import lax
from jax.experimental import pallas as pl
from jax.experimental.pallas import tpu as pltpu

D_MODEL = 1024
HEAD_DIM = 64
SWA_Q_HEADS = 8
SWA_KV_HEADS = 2
SWA_WINDOW = 128
FOX_HEADS = 8
Q_BLOCK = 128
N_BUCKETS = 32
MAX_DISTANCE = 128
N_EXPERTS = 256
TOP_K = 8
N_GROUPS = 8
TOPK_GROUPS = 4
GROUP_SIZE = N_EXPERTS // N_GROUPS
D_EXPERT = 256
D_SHARED = 256
ROUTED_SCALE = 2.5
LN_EPS = 1e-5

SWA_Q = SWA_Q_HEADS * HEAD_DIM
SWA_KV = SWA_KV_HEADS * HEAD_DIM
FOX_W = FOX_HEADS * HEAD_DIM
IN_SIZES = (SWA_Q, SWA_KV, SWA_KV, FOX_W, FOX_W, FOX_W, FOX_HEADS, D_MODEL, D_MODEL)

LANES = 128
HALF_D = D_MODEL // 2
ROW_WORDS = HALF_D // LANES
VMEM_LIMIT = 56 * 1024 * 1024
HI_MASK = 0xFFFF0000

TM_PROJ = 512
TM_MERGE = 256
TM_ROUTE = 256
TM_MOVE = 256
TM_PLAN = 2048
EXPERT_BLOCK = 256
FOX_TQ = 256
FOX_TK = 1024

AUG = HEAD_DIM


def _cparams(sem):
    return pltpu.CompilerParams(dimension_semantics=sem, vmem_limit_bytes=VMEM_LIMIT)


def _sigmoid(x):
    return 1.0 / (1.0 + jnp.exp(-x))


def _split3(c):
    hi = c.astype(jnp.bfloat16).astype(jnp.float32)
    r = c - hi
    mid = r.astype(jnp.bfloat16).astype(jnp.float32)
    lo = (r - mid).astype(jnp.bfloat16).astype(jnp.float32)
    return hi, mid, lo


_C_QA = 0
_C_KA = _C_QA + SWA_Q
_C_VA = _C_KA + SWA_KV
_C_QF = _C_VA + SWA_KV
_C_KF = _C_QF + FOX_W
_C_VF = _C_KF + FOX_W
_C_GA = _C_VF + FOX_W
_C_GB = _C_GA + D_MODEL
_C_ZF = _C_GB + D_MODEL
_N_PROJ = _C_ZF + LANES


def _in_proj_kernel(tiles_per_seq, x_ref, w_ref, bf_ref, tri_ref,
                    qa_ref, ka_ref, va_ref, qf_ref, kf_ref, vf_ref, ga_ref, gb_ref, carry_ref):
    i = pl.program_id(0)
    tm = x_ref.shape[0]
    xb = x_ref[...].astype(jnp.bfloat16)

    def proj(c0, n):
        return jnp.dot(xb, w_ref[:, c0:c0 + n], preferred_element_type=jnp.float32)

    scale = HEAD_DIM ** -0.5
    qa_ref[...] = (proj(_C_QA, SWA_Q) * scale).astype(jnp.bfloat16)
    ka_ref[...] = proj(_C_KA, SWA_KV).astype(jnp.bfloat16)
    va_ref[...] = proj(_C_VA, SWA_KV).astype(jnp.bfloat16)
    ga_ref[...] = proj(_C_GA, D_MODEL)
    gb_ref[...] = proj(_C_GB, D_MODEL)

    z = proj(_C_ZF, LANES) + bf_ref[...]
    log_f = jnp.minimum(z, 0.0) - jnp.log(1.0 + jnp.exp(-jnp.abs(z)))

    @pl.when(i % tiles_per_seq == 0)
    def _():
        carry_ref[...] = jnp.zeros_like(carry_ref)

    tri = tri_ref[...]
    h3 = _split3(log_f)
    c = carry_ref[...]
    for part in h3:
        c = c + jnp.dot(tri, part.astype(jnp.bfloat16), preferred_element_type=jnp.float32)
    carry_ref[...] = c[tm - 1:tm, :]

    qf = proj(_C_QF, FOX_W) * scale
    kf = proj(_C_KF, FOX_W)
    vf = proj(_C_VF, FOX_W)
    lane = lax.broadcasted_iota(jnp.int32, (tm, LANES), 1)
    one = jnp.float32(1.0)
    zero = jnp.float32(0.0)
    for h in range(FOX_HEADS):
        pair = (h // 2) * LANES
        qp = qf[:, pair:pair + LANES]
        kp = kf[:, pair:pair + LANES]
        vp = vf[:, pair:pair + LANES]
        if h % 2 == 1:
            qp = pltpu.roll(qp, HEAD_DIM, axis=1)
            kp = pltpu.roll(kp, HEAD_DIM, axis=1)
            vp = pltpu.roll(vp, HEAD_DIM, axis=1)
        ch = jnp.broadcast_to(c[:, h:h + 1], (tm, LANES))
        hi, mid, lo = _split3(ch)
        q_aug = jnp.where(lane == AUG, hi, jnp.where(lane == AUG + 1, mid, jnp.where(
            lane == AUG + 2, lo, jnp.where(lane < AUG + 6, one, zero))))
        k_aug = jnp.where(lane < AUG + 3, one, jnp.where(lane == AUG + 3, -hi, jnp.where(
            lane == AUG + 4, -mid, jnp.where(lane == AUG + 5, -lo, zero))))
        qf_ref[h] = jnp.where(lane < AUG, qp, q_aug).astype(jnp.bfloat16)
        kf_ref[h] = jnp.where(lane < AUG, kp, k_aug).astype(jnp.bfloat16)
        vf_ref[h] = jnp.where(lane < AUG, vp, jnp.where(lane == AUG, one, zero)).astype(jnp.bfloat16)


def _in_proj(x2, w_main, bf_row, seq_len):
    t = x2.shape[0]
    tm = min(TM_PROJ, seq_len)
    tri = jnp.tril(jnp.ones((tm, tm), jnp.bfloat16))
    row = lambda n: pl.BlockSpec((tm, n), lambda i: (i, 0))
    full = lambda a: pl.BlockSpec(a.shape, lambda i: (0,) * a.ndim)
    slab = pl.BlockSpec((FOX_HEADS, tm, LANES), lambda i: (0, i, 0))
    bf16, f32 = jnp.bfloat16, jnp.float32
    slab_shape = jax.ShapeDtypeStruct((FOX_HEADS, t, LANES), bf16)
    out_shape = (
        jax.ShapeDtypeStruct((t, SWA_Q), bf16), jax.ShapeDtypeStruct((t, SWA_KV), bf16),
        jax.ShapeDtypeStruct((t, SWA_KV), bf16), slab_shape, slab_shape, slab_shape,
        jax.ShapeDtypeStruct((t, D_MODEL), f32), jax.ShapeDtypeStruct((t, D_MODEL), f32))
    return pl.pallas_call(
        functools.partial(_in_proj_kernel, seq_len // tm),
        grid=(t // tm,),
        in_specs=[row(D_MODEL), full(w_main), full(bf_row), full(tri)],
        out_specs=(row(SWA_Q), row(SWA_KV), row(SWA_KV), slab, slab, slab, row(D_MODEL), row(D_MODEL)),
        out_shape=out_shape,
        scratch_shapes=[pltpu.VMEM((1, LANES), f32)],
        compiler_params=_cparams(("arbitrary",)),
        name="in_proj",
    )(x2, w_main, bf_row, tri)


def _swa_kernel(blocks_per_seq, sink_ref, q_ref, kp_ref, kc_ref, vp_ref, vc_ref, bias_ref, o_ref):
    n = pl.program_id(0)
    first_key = jnp.where((n % blocks_per_seq) > 0, 0, Q_BLOCK)
    kk = jnp.concatenate([kp_ref[...], kc_ref[...]], axis=0)
    vv = jnp.concatenate([vp_ref[...], vc_ref[...]], axis=0)
    qi = lax.broadcasted_iota(jnp.int32, (Q_BLOCK, 2 * Q_BLOCK), 0)
    kj = lax.broadcasted_iota(jnp.int32, (Q_BLOCK, 2 * Q_BLOCK), 1)
    dist = qi + Q_BLOCK - kj
    valid = (dist >= 0) & (dist < SWA_WINDOW) & (kj >= first_key)
    lane = lax.broadcasted_iota(jnp.int32, (Q_BLOCK, LANES), 1)
    lo_half = lane < HEAD_DIM
    group = SWA_Q_HEADS // SWA_KV_HEADS
    for j in range(group):
        qs = q_ref[:, j * LANES:(j + 1) * LANES]
        outs = []
        for kv in range(SWA_KV_HEADS):
            head = kv * group + j
            qm = jnp.where(lo_half if kv == 0 else ~lo_half, qs, jnp.zeros_like(qs))
            s = lax.dot_general(qm, kk, (((1,), (1,)), ((), ())), preferred_element_type=jnp.float32)
            s = jnp.where(valid, s + bias_ref[head], -jnp.inf)
            sink = sink_ref[head]
            m = jnp.maximum(jnp.max(s, axis=-1, keepdims=True), sink)
            p = jnp.exp(s - m)
            denom = jnp.sum(p, axis=-1, keepdims=True) + jnp.exp(sink - m)
            pn = (p / denom).astype(jnp.bfloat16)
            outs.append(jnp.dot(pn, vv, preferred_element_type=jnp.float32))
        o_ref[:, j * LANES:(j + 1) * LANES] = jnp.where(lo_half, outs[0], outs[1]).astype(o_ref.dtype)


def _swa(qa, ka, va, sinks, bias_tab, seq_len):
    t = qa.shape[0]
    bps = seq_len // Q_BLOCK
    cur = lambda n, s: (n, 0)
    prev = lambda n, s: (jnp.maximum(n - 1, 0), 0)
    grid_spec = pltpu.PrefetchScalarGridSpec(
        num_scalar_prefetch=1,
        grid=(t // Q_BLOCK,),
        in_specs=[pl.BlockSpec((Q_BLOCK, SWA_Q), cur),
                  pl.BlockSpec((Q_BLOCK, SWA_KV), prev), pl.BlockSpec((Q_BLOCK, SWA_KV), cur),
                  pl.BlockSpec((Q_BLOCK, SWA_KV), prev), pl.BlockSpec((Q_BLOCK, SWA_KV), cur),
                  pl.BlockSpec(bias_tab.shape, lambda n, s: (0, 0, 0))],
        out_specs=pl.BlockSpec((Q_BLOCK, SWA_Q), cur))
    return pl.pallas_call(
        functools.partial(_swa_kernel, bps),
        grid_spec=grid_spec,
        out_shape=jax.ShapeDtypeStruct((t, SWA_Q), jnp.bfloat16),
        compiler_params=_cparams(("arbitrary",)),
        name="swa",
    )(sinks, qa, ka, ka, va, va, bias_tab)


def _t5_bias_table(rel_bias):
    i = np.arange(Q_BLOCK)[:, None]
    j = np.arange(2 * Q_BLOCK)[None, :]
    n = jnp.asarray(np.maximum(i + Q_BLOCK - j, 0))
    max_exact = N_BUCKETS // 2
    nf = jnp.maximum(n, 1).astype(jnp.float32)
    large = max_exact + (jnp.log(nf / max_exact) / math.log(MAX_DISTANCE / max_exact)
                         * (N_BUCKETS - max_exact)).astype(jnp.int32)
    large = jnp.minimum(large, N_BUCKETS - 1)
    bucket = jnp.where(n < max_exact, n, large)
    onehot = bucket[None] == jnp.arange(N_BUCKETS)[:, None, None]
    rb = rel_bias.astype(jnp.float32)
    return jnp.sum(jnp.where(onehot[:, None], rb[:, :, None, None], 0.0), axis=0)


def _fox_kernel(tk, q_ref, k_ref, v_ref, o_ref):
    qi = pl.program_id(2)
    tq = q_ref.shape[1]
    row = lax.broadcasted_iota(jnp.int32, (tq, tk), 0) + qi * tq
    col = lax.broadcasted_iota(jnp.int32, (tq, tk), 1)
    qs = (q_ref[0], q_ref[1])

    def step(j, carry, masked):
        start = pl.multiple_of(j * tk, tk)
        new = []
        for hh in range(2):
            m, acc = carry[hh]
            k = k_ref[hh, pl.ds(start, tk), :]
            v = v_ref[hh, pl.ds(start, tk), :]
            s = lax.dot_general(qs[hh], k, (((1,), (1,)), ((), ())), preferred_element_type=jnp.float32)
            if masked:
                s = jnp.where(col + j * tk <= row, s, -jnp.inf)
            m_new = jnp.maximum(m, jnp.max(s, axis=-1, keepdims=True))
            a = jnp.exp(m - m_new)
            p = jnp.exp(s - m_new)
            acc = a * acc + jnp.dot(p.astype(jnp.bfloat16), v, preferred_element_type=jnp.float32)
            new.append((m_new, acc))
        return tuple(new)

    init1 = (jnp.full((tq, 1), -jnp.inf, jnp.float32), jnp.zeros((tq, LANES), jnp.float32))
    n_full = (qi * tq) // tk
    carry = lax.fori_loop(0, n_full, functools.partial(step, masked=False), (init1, init1))
    for d in range(max(1, tq // tk)):
        carry = step(n_full + d, carry, masked=True)
    outs = [acc / acc[:, AUG:AUG + 1] for _, acc in carry]
    lane = lax.broadcasted_iota(jnp.int32, (tq, LANES), 1)
    o_ref[...] = jnp.where(lane < HEAD_DIM, outs[0], pltpu.roll(outs[1], HEAD_DIM, axis=1)).astype(o_ref.dtype)


def _fox(qf, kf, vf, batch, seq_len):
    t = qf.shape[1]
    tq = min(FOX_TQ, seq_len)
    tk = min(FOX_TK, seq_len)
    assert tq % tk == 0 or tk % tq == 0
    nq = seq_len // tq
    pairs = FOX_HEADS // 2
    seq = pl.BlockSpec((2, seq_len, LANES), lambda b, hp, i: (hp, b, 0))
    return pl.pallas_call(
        functools.partial(_fox_kernel, tk),
        grid=(batch, pairs, nq),
        in_specs=[pl.BlockSpec((2, tq, LANES), lambda b, hp, i: (hp, b * nq + i, 0)), seq, seq],
        out_specs=pl.BlockSpec((tq, LANES), lambda b, hp, i: (b * nq + i, hp)),
        out_shape=jax.ShapeDtypeStruct((t, FOX_W), jnp.bfloat16),
        compiler_params=_cparams(("arbitrary", "arbitrary", "arbitrary")),
        name="fox",
    )(qf, kf, vf)


def _layer_norm(h, g, b):
    mu = jnp.mean(h, axis=-1, keepdims=True)
    d = h - mu
    var = jnp.mean(d * d, axis=-1, keepdims=True)
    return d * lax.rsqrt(var + LN_EPS) * g + b


def _pack_rows(ref, x):
    rows = x.shape[0]
    bits = pltpu.bitcast(x.astype(jnp.bfloat16).astype(jnp.float32), jnp.uint32)
    for c in range(ROW_WORDS):
        lo = bits[:, c * LANES:(c + 1) * LANES] >> 16
        hi = bits[:, HALF_D + c * LANES:HALF_D + (c + 1) * LANES] & jnp.uint32(HI_MASK)
        ref[pl.ds(c, rows, stride=ROW_WORDS), :] = lo | hi


def _unpack_rows(ref, first, rows):
    los, his = [], []
    for c in range(ROW_WORDS):
        w = ref[pl.ds(first * ROW_WORDS + c, rows, stride=ROW_WORDS), :]
        los.append(pltpu.bitcast(w << 16, jnp.float32))
        his.append(pltpu.bitcast(w & jnp.uint32(HI_MASK), jnp.float32))
    return jnp.concatenate(los + his, axis=1)


def _packed(ref, first, rows=1):
    return ref.at[pl.ds(pl.multiple_of(first * ROW_WORDS, ROW_WORDS), rows * ROW_WORDS), :]


def _packed_spec(rows, index_map):
    return pl.BlockSpec((rows * ROW_WORDS, LANES), index_map)


def _packed_shape(rows):
    return jax.ShapeDtypeStruct((rows * ROW_WORDS, LANES), jnp.uint32)


def _merge_kernel(alpha, x_ref, ya_ref, yb_ref, ga_ref, gb_ref, wa_ref, wb_ref, wo_ref, g_ref, b_ref,
                  x1_ref, x1p_ref):
    pa = jnp.dot(ya_ref[...], wa_ref[...], preferred_element_type=jnp.float32)
    pb = jnp.dot(yb_ref[...], wb_ref[...], preferred_element_type=jnp.float32)
    mix = _sigmoid(ga_ref[...]) * pa + _sigmoid(gb_ref[...]) * pb
    out = jnp.dot(mix.astype(jnp.bfloat16), wo_ref[...], preferred_element_type=jnp.float32)
    x1 = _layer_norm(alpha * x_ref[...] + out, g_ref[...], b_ref[...])
    x1_ref[...] = x1
    _pack_rows(x1p_ref, x1)


def _merge(alpha, x2, ya, yb, ga, gb, wa, wb, wo, g, b):
    t = x2.shape[0]
    tm = TM_MERGE
    row = lambda n: pl.BlockSpec((tm, n), lambda i: (i, 0))
    full = lambda a: pl.BlockSpec(a.shape, lambda i: (0,) * a.ndim)
    return pl.pallas_call(
        functools.partial(_merge_kernel, alpha),
        grid=(t // tm,),
        in_specs=[row(D_MODEL), row(SWA_Q), row(FOX_W), row(D_MODEL), row(D_MODEL),
                  full(wa), full(wb), full(wo), full(g), full(b)],
        out_specs=(row(D_MODEL), _packed_spec(tm, lambda i: (i, 0))),
        out_shape=(jax.ShapeDtypeStruct((t, D_MODEL), jnp.float32), _packed_shape(t)),
        compiler_params=_cparams(("arbitrary",)),
        name="merge",
    )(x2, ya, yb, ga, gb, wa, wb, wo, g, b)


def _first_argmax(v, idx):
    m = jnp.max(v, axis=0, keepdims=True)
    first = jnp.min(jnp.where(v == m, idx, v.shape[0]), axis=0, keepdims=True)
    return m, first


def _router_kernel(x_ref, wr_ref, rb_ref, tri_ref, e_ref, w_ref, r_ref, cnt_ref, carry_ref):
    i = pl.program_id(0)
    tm = x_ref.shape[0]

    @pl.when(i == 0)
    def _():
        carry_ref[...] = jnp.zeros_like(carry_ref)

    xb = x_ref[...].astype(jnp.bfloat16)
    logits = lax.dot_general(wr_ref[...], xb, (((1,), (1,)), ((), ())),
                             preferred_element_type=jnp.float32)
    scores = _sigmoid(logits)
    biased = scores + rb_ref[...]

    in_group = lax.broadcasted_iota(jnp.int32, (GROUP_SIZE, tm), 0)
    grows = []
    for g in range(N_GROUPS):
        blk = biased[g * GROUP_SIZE:(g + 1) * GROUP_SIZE, :]
        m1, f1 = _first_argmax(blk, in_group)
        m2 = jnp.max(jnp.where(in_group == f1, -jnp.inf, blk), axis=0, keepdims=True)
        grows.append(m1 + m2)
    gscore = jnp.concatenate(grows, axis=0)

    gidx = lax.broadcasted_iota(jnp.int32, (N_GROUPS, tm), 0)
    gwork = gscore
    gsel = jnp.zeros((N_GROUPS, tm), jnp.float32)
    for _ in range(TOPK_GROUPS):
        _, first = _first_argmax(gwork, gidx)
        hit = gidx == first
        gsel = jnp.where(hit, 1.0, gsel)
        gwork = jnp.where(hit, -jnp.inf, gwork)

    eidx = lax.broadcasted_iota(jnp.int32, (N_EXPERTS, tm), 0)
    work = jnp.concatenate(
        [jnp.where(jnp.broadcast_to(gsel[g:g + 1, :], (GROUP_SIZE, tm)) > 0.0,
                   biased[g * GROUP_SIZE:(g + 1) * GROUP_SIZE, :], -jnp.inf) for g in range(N_GROUPS)], axis=0)
    picks = []
    sel = jnp.zeros((N_EXPERTS, tm), jnp.bool_)
    for _ in range(TOP_K):
        _, first = _first_argmax(work, eidx)
        hit = eidx == first
        picks.append((first, hit))
        sel = sel | hit
        work = jnp.where(hit, -jnp.inf, work)

    wsum = jnp.sum(jnp.where(sel, scores, 0.0), axis=0, keepdims=True)
    sel_b = jnp.where(sel, 1.0, 0.0).astype(jnp.bfloat16)
    before = carry_ref[...] + jnp.dot(sel_b, tri_ref[...], preferred_element_type=jnp.float32)
    e_rows, w_rows, r_rows = [], [], []
    for first, hit in picks:
        e_rows.append(first)
        sc = jnp.sum(jnp.where(hit, scores, 0.0), axis=0, keepdims=True)
        w_rows.append(sc / wsum * ROUTED_SCALE)
        r_rows.append(jnp.sum(jnp.where(hit, before, 0.0), axis=0, keepdims=True).astype(jnp.int32))
    e_ref[...] = jnp.concatenate(e_rows, axis=0)
    w_ref[...] = jnp.concatenate(w_rows, axis=0)
    r_ref[...] = jnp.concatenate(r_rows, axis=0)
    total = carry_ref[...] + jnp.sum(sel_b.astype(jnp.float32), axis=1, keepdims=True)
    carry_ref[...] = total
    cnt_ref[...] = jnp.broadcast_to(total, cnt_ref.shape).astype(jnp.int32)


def _router(x1, wr_t, rb_col):
    t = x1.shape[0]
    tm = TM_ROUTE
    tri = jnp.triu(jnp.ones((tm, tm), jnp.bfloat16), k=1)
    full = lambda a: pl.BlockSpec(a.shape, lambda i: (0,) * a.ndim)
    kt = pl.BlockSpec((TOP_K, tm), lambda i: (0, i))
    return pl.pallas_call(
        _router_kernel,
        grid=(t // tm,),
        in_specs=[pl.BlockSpec((tm, D_MODEL), lambda i: (i, 0)), full(wr_t), full(rb_col), full(tri)],
        out_specs=(kt, kt, kt, pl.BlockSpec((N_EXPERTS, LANES), lambda i: (0, 0))),
        out_shape=(jax.ShapeDtypeStruct((TOP_K, t), jnp.int32), jax.ShapeDtypeStruct((TOP_K, t), jnp.float32),
                   jax.ShapeDtypeStruct((TOP_K, t), jnp.int32),
                   jax.ShapeDtypeStruct((N_EXPERTS, LANES), jnp.int32)),
        scratch_shapes=[pltpu.VMEM((N_EXPERTS, 1), jnp.float32)],
        compiler_params=_cparams(("arbitrary",)),
        name="router",
    )(x1, wr_t, rb_col, tri)


def _plan_kernel(ps_ref, e_ref, r_ref, d_ref):
    tm = e_ref.shape[1]
    eidx = lax.broadcasted_iota(jnp.int32, (N_EXPERTS, tm), 0)
    rows = []
    for k in range(TOP_K):
        hit = eidx == e_ref[k:k + 1, :]
        base = jnp.sum(jnp.where(hit, ps_ref[...], 0.0), axis=0, keepdims=True)
        rows.append(base.astype(jnp.int32) + r_ref[k:k + 1, :])
    d_ref[...] = jnp.concatenate(rows, axis=0)


def _plan(pstart_col, top_e, rank):
    t = top_e.shape[1]
    tm = min(TM_PLAN, t)
    kt = pl.BlockSpec((TOP_K, tm), lambda i: (0, i))
    return pl.pallas_call(
        _plan_kernel,
        grid=(t // tm,),
        in_specs=[pl.BlockSpec(pstart_col.shape, lambda i: (0, 0)), kt, kt],
        out_specs=kt,
        out_shape=jax.ShapeDtypeStruct((TOP_K, t), jnp.int32),
        compiler_params=_cparams(("arbitrary",)),
        name="plan",
    )(pstart_col, top_e, rank)


def _dispatch_kernel(pend_ref, padded_ref, nu_ref, dest_hbm, x_ref, xs_hbm, idx_smem, zero_vmem, sem, zsem):
    i = pl.program_id(0)
    n_blocks = xs_hbm.shape[0] // (EXPERT_BLOCK * ROW_WORDS)

    def zero_copy(first_row):
        return pltpu.make_async_copy(zero_vmem, _packed(xs_hbm, first_row, EXPERT_BLOCK), zsem)

    @pl.when(i == 0)
    def _():
        zero_vmem[...] = jnp.zeros_like(zero_vmem)

        def fill(e, c):
            @pl.when(padded_ref[e] > 0)
            def _():
                zero_copy(pend_ref[e] - EXPERT_BLOCK).start()
            return c
        lax.fori_loop(0, N_EXPERTS, fill, 0)

        def fill_tail(b, c):
            zero_copy(b * EXPERT_BLOCK).start()
            return c
        lax.fori_loop(nu_ref[0], n_blocks, fill_tail, 0)

        def drain(e, c):
            @pl.when(padded_ref[e] > 0)
            def _():
                zero_copy(0).wait()
            return c
        lax.fori_loop(0, N_EXPERTS, drain, 0)

        def drain_tail(b, c):
            zero_copy(0).wait()
            return c
        lax.fori_loop(nu_ref[0], n_blocks, drain_tail, 0)

    n_idx = TM_MOVE * TOP_K
    pltpu.sync_copy(dest_hbm.at[pl.ds(i * n_idx, n_idx)], idx_smem)

    def issue(t, c):
        for k in range(TOP_K):
            pltpu.make_async_copy(_packed(x_ref, t), _packed(xs_hbm, idx_smem[t * TOP_K + k]),
                                  sem).start(priority=k % 2)
        return c
    lax.fori_loop(0, TM_MOVE, issue, 0)

    for k in range(TOP_K):
        pltpu.make_async_copy(x_ref, _packed(xs_hbm, 0, TM_MOVE), sem).wait()


def _dispatch(dest_flat, x1p, pend, padded, n_used, n_rows):
    t = x1p.shape[0] // ROW_WORDS
    grid_spec = pltpu.PrefetchScalarGridSpec(
        num_scalar_prefetch=3,
        grid=(t // TM_MOVE,),
        in_specs=[pl.BlockSpec(memory_space=pl.ANY), _packed_spec(TM_MOVE, lambda i, pe, pa, nu: (i, 0))],
        out_specs=pl.BlockSpec(memory_space=pl.ANY),
        scratch_shapes=[pltpu.SMEM((TM_MOVE * TOP_K,), jnp.int32),
                        pltpu.VMEM((EXPERT_BLOCK * ROW_WORDS, LANES), jnp.uint32),
                        pltpu.SemaphoreType.DMA, pltpu.SemaphoreType.DMA])
    return pl.pallas_call(
        _dispatch_kernel,
        grid_spec=grid_spec,
        out_shape=_packed_shape(n_rows),
        compiler_params=_cparams(("arbitrary",)),
        name="dispatch",
    )(pend, padded, n_used, dest_flat, x1p)


def _experts_kernel(be_ref, nu_ref, xs_ref, wg_ref, wu_ref, wd_ref, y_ref, wg_b, wu_b, wd_b):
    i = pl.program_id(0)
    live = i < nu_ref[0]

    @pl.when(live & ((i == 0) | (be_ref[i] != be_ref[jnp.maximum(i - 1, 0)])))
    def _():
        wg_b[...] = wg_ref[...].astype(jnp.bfloat16)
        wu_b[...] = wu_ref[...].astype(jnp.bfloat16)
        wd_b[...] = wd_ref[...].astype(jnp.bfloat16)

    @pl.when(live)
    def _():
        xb = _unpack_rows(xs_ref, 0, EXPERT_BLOCK).astype(jnp.bfloat16)
        g = jnp.dot(xb, wg_b[...], preferred_element_type=jnp.float32)
        u = jnp.dot(xb, wu_b[...], preferred_element_type=jnp.float32)
        h = g * _sigmoid(g) * u
        y = jnp.dot(h.astype(jnp.bfloat16), wd_b[...], preferred_element_type=jnp.float32)
        _pack_rows(y_ref, y)

    @pl.when(jnp.logical_not(live))
    def _():
        y_ref[...] = jnp.zeros_like(y_ref)


def _experts(layer, block_e, n_used, xs, w_gate_e, w_up_e, w_down_e):
    n_blocks = xs.shape[0] // (EXPERT_BLOCK * ROW_WORDS)
    wsel = lambda i, be, nu: (layer, be[i], 0, 0)
    grid_spec = pltpu.PrefetchScalarGridSpec(
        num_scalar_prefetch=2,
        grid=(n_blocks,),
        in_specs=[_packed_spec(EXPERT_BLOCK, lambda i, be, nu: (jnp.minimum(i, nu[0] - 1), 0)),
                  pl.BlockSpec((None, None, D_MODEL, D_EXPERT), wsel),
                  pl.BlockSpec((None, None, D_MODEL, D_EXPERT), wsel),
                  pl.BlockSpec((None, None, D_EXPERT, D_MODEL), wsel)],
        out_specs=_packed_spec(EXPERT_BLOCK, lambda i, be, nu: (i, 0)),
        scratch_shapes=[pltpu.VMEM((D_MODEL, D_EXPERT), jnp.bfloat16), pltpu.VMEM((D_MODEL, D_EXPERT), jnp.bfloat16),
                        pltpu.VMEM((D_EXPERT, D_MODEL), jnp.bfloat16)])
    return pl.pallas_call(
        _experts_kernel,
        grid_spec=grid_spec,
        out_shape=_packed_shape(n_blocks * EXPERT_BLOCK),
        compiler_params=_cparams(("arbitrary",)),
        name="experts",
    )(block_e, n_used, xs, w_gate_e, w_up_e, w_down_e)


def _combine_kernel(alpha, dest_hbm, y_hbm, w_ref, x1_ref, wg_ref, wu_ref, wd_ref, g_ref, b_ref,
                    o_ref, idx_smem, buf, sem):
    i = pl.program_id(0)
    n = pl.num_programs(0)
    n_idx = TM_MOVE * TOP_K

    def slab(slot, k):
        return (slot * TOP_K + k) * TM_MOVE

    def start_gather(step, slot):
        pltpu.sync_copy(dest_hbm.at[pl.ds(step * n_idx, n_idx)], idx_smem.at[slot])

        def issue(t, c):
            for k in range(TOP_K):
                pltpu.make_async_copy(_packed(y_hbm, idx_smem[slot, t * TOP_K + k]),
                                      _packed(buf, slab(slot, k) + t), sem.at[slot]).start(priority=k % 2)
            return c
        lax.fori_loop(0, TM_MOVE, issue, 0)

    @pl.when(i == 0)
    def _():
        start_gather(0, 0)

    @pl.when(i + 1 < n)
    def _():
        start_gather(i + 1, (i + 1) % 2)

    slot = i % 2
    for k in range(TOP_K):
        pltpu.make_async_copy(_packed(y_hbm, 0, TM_MOVE), _packed(buf, slab(slot, k), TM_MOVE),
                              sem.at[slot]).wait()

    w = w_ref[...]
    routed = jnp.zeros((TM_MOVE, D_MODEL), jnp.float32)
    for k in range(TOP_K):
        routed = routed + w[:, k:k + 1] * _unpack_rows(buf, slab(slot, k), TM_MOVE)

    x1 = x1_ref[...]
    xb = x1.astype(jnp.bfloat16)
    gs = jnp.dot(xb, wg_ref[...], preferred_element_type=jnp.float32)
    us = jnp.dot(xb, wu_ref[...], preferred_element_type=jnp.float32)
    hs = gs * _sigmoid(gs) * us
    shared = jnp.dot(hs.astype(jnp.bfloat16), wd_ref[...], preferred_element_type=jnp.float32)
    o_ref[...] = _layer_norm(alpha * x1 + (shared + routed), g_ref[...], b_ref[...])


def _combine(alpha, dest_flat, y, w_tk, x1, wg, wu, wd, g, b):
    t = x1.shape[0]
    tm = TM_MOVE
    full = lambda a: pl.BlockSpec(a.shape, lambda i: (0,) * a.ndim)
    return pl.pallas_call(
        functools.partial(_combine_kernel, alpha),
        grid=(t // tm,),
        in_specs=[pl.BlockSpec(memory_space=pl.ANY), pl.BlockSpec(memory_space=pl.ANY),
                  pl.BlockSpec((tm, TOP_K), lambda i: (i, 0)), pl.BlockSpec((tm, D_MODEL), lambda i: (i, 0)),
                  full(wg), full(wu), full(wd), full(g), full(b)],
        out_specs=pl.BlockSpec((tm, D_MODEL), lambda i: (i, 0)),
        out_shape=jax.ShapeDtypeStruct((t, D_MODEL), jnp.float32),
        scratch_shapes=[pltpu.SMEM((2, tm * TOP_K), jnp.int32),
                        pltpu.VMEM((2 * TOP_K * tm * ROW_WORDS, LANES), jnp.uint32),
                        pltpu.SemaphoreType.DMA((2,))],
        compiler_params=_cparams(("arbitrary",)),
        name="combine",
    )(dest_flat, y, w_tk, x1, wg, wu, wd, g, b)


def _prep_in_weights(w, b_forget):
    o = np.cumsum((0,) + IN_SIZES)
    group = SWA_Q_HEADS // SWA_KV_HEADS
    qa = w[:, o[0]:o[1]].reshape(D_MODEL, SWA_KV_HEADS, group, HEAD_DIM)
    qa = qa.transpose(0, 2, 1, 3).reshape(D_MODEL, SWA_Q)
    zf = jnp.pad(w[:, o[6]:o[7]], ((0, 0), (0, LANES - FOX_HEADS)))
    w_main = jnp.concatenate([qa, w[:, o[1]:o[6]], w[:, o[7]:o[9]], zf], axis=1).astype(jnp.bfloat16)
    bf_row = jnp.pad(b_forget, (0, LANES - FOX_HEADS)).reshape(1, LANES).astype(jnp.float32)
    return w_main, bf_row


def _prep_proj_a(w):
    group = SWA_Q_HEADS // SWA_KV_HEADS
    w4 = w.reshape(SWA_KV_HEADS, group, HEAD_DIM, D_MODEL).transpose(1, 0, 2, 3)
    return w4.reshape(SWA_Q, D_MODEL).astype(jnp.bfloat16)


def _route_plan(counts, n_tokens):
    n_assign = n_tokens * TOP_K
    n_rows = (n_assign + N_EXPERTS * (EXPERT_BLOCK - 1) + EXPERT_BLOCK - 1) // EXPERT_BLOCK * EXPERT_BLOCK
    n_blocks = n_rows // EXPERT_BLOCK
    padded = (counts + EXPERT_BLOCK - 1) // EXPERT_BLOCK * EXPERT_BLOCK
    pend = jnp.cumsum(padded)
    pstart = pend - padded
    n_used = pend[-1] // EXPERT_BLOCK
    blk = jnp.minimum(jnp.arange(n_blocks, dtype=jnp.int32), n_used - 1) * EXPERT_BLOCK
    block_e = jnp.minimum(jnp.sum((pend[None, :] <= blk[:, None]).astype(jnp.int32), axis=1), N_EXPERTS - 1)
    return (pstart.astype(jnp.float32).reshape(N_EXPERTS, 1), block_e.astype(jnp.int32),
            n_used.reshape(1).astype(jnp.int32), pend.astype(jnp.int32), padded.astype(jnp.int32), n_rows)


def kernel(x, w_in, b_forget, attn_sinks, rel_bias, w_proj_a, w_proj_b, w_out, ln1_g, ln1_b,
           w_router, router_bias, w_gate_e, w_up_e, w_down_e, w_gate_s, w_up_s, w_down_s, ln2_g, ln2_b):
    batch, seq_len, d = x.shape
    depth = w_in.shape[0]
    alpha = (2 * depth) ** 0.25
    t = batch * seq_len
    bf16 = jnp.bfloat16
    x2 = x.reshape(t, d)
    bias_tab = _t5_bias_table(rel_bias)
    for l in range(depth):
        w_main, bf_row = _prep_in_weights(w_in[l], b_forget[l])
        qa, ka, va, qf, kf, vf, ga, gb = _in_proj(x2, w_main, bf_row, seq_len)
        ya = _swa(qa, ka, va, attn_sinks[l].astype(jnp.float32), bias_tab, seq_len)
        yb = _fox(qf, kf, vf, batch, seq_len)
        x1, x1p = _merge(alpha, x2, ya, yb, ga, gb, _prep_proj_a(w_proj_a[l]), w_proj_b[l].astype(bf16),
                         w_out[l].astype(bf16), ln1_g[l].reshape(1, d), ln1_b[l].reshape(1, d))
        top_e, w_kt, rank, counts = _router(x1, w_router[l].T.astype(bf16),
                                            router_bias[l].reshape(N_EXPERTS, 1).astype(jnp.float32))
        pstart_col, block_e, n_used, pend, padded, n_rows = _route_plan(counts[:, 0], t)
        dest_flat = _plan(pstart_col, top_e, rank).T.reshape(-1)
        xs = _dispatch(dest_flat, x1p, pend, padded, n_used, n_rows)
        y = _experts(l, block_e, n_used, xs, w_gate_e, w_up_e, w_down_e)
        x2 = _combine(alpha, dest_flat, y, w_kt.T, x1, w_gate_s[l].astype(bf16), w_up_s[l].astype(bf16),
                      w_down_s[l].astype(bf16), ln2_g[l].reshape(1, d), ln2_b[l].reshape(1, d))
    return x2.reshape(batch, seq_len, d)
```

```python
import functools
import math

import jax
import jax.numpy as jnp
import numpy as np
from jax import lax
from jax.experimental import pallas as pl
from jax.experimental.pallas import tpu as pltpu

D_MODEL = 1024
HEAD_DIM = 64
SWA_Q_HEADS = 8
SWA_KV_HEADS = 2
SWA_WINDOW = 128
FOX_HEADS = 8
Q_BLOCK = 128
N_BUCKETS = 32
MAX_DISTANCE = 128
N_EXPERTS = 256
TOP_K = 8
N_GROUPS = 8
TOPK_GROUPS = 4
GROUP_SIZE = N_EXPERTS // N_GROUPS
D_EXPERT = 256
D_SHARED = 256
ROUTED_SCALE = 2.5
LN_EPS = 1e-5

SWA_Q = SWA_Q_HEADS * HEAD_DIM
SWA_KV = SWA_KV_HEADS * HEAD_DIM
FOX_W = FOX_HEADS * HEAD_DIM
IN_SIZES = (SWA_Q, SWA_KV, SWA_KV, FOX_W, FOX_W, FOX_W, FOX_HEADS, D_MODEL, D_MODEL)

LANES = 128
HALF_D = D_MODEL // 2
ROW_WORDS = HALF_D // LANES
VMEM_LIMIT = 56 * 1024 * 1024
HI_MASK = 0xFFFF0000

TM_PROJ = 512
TM_MERGE = 256
TM_ROUTE = 256
TM_MOVE = 256
TM_PLAN = 2048
EXPERT_BLOCK = 256
EXPERT_CHAINS = 2
FOX_TQ = 1024
FOX_TK = 1024
FOX_SUBTILES = 1

AUG = HEAD_DIM


def _cparams(sem):
    return pltpu.CompilerParams(dimension_semantics=sem, vmem_limit_bytes=VMEM_LIMIT)


def _sigmoid(x):
    return 1.0 / (1.0 + jnp.exp(-x))


def _split3(c):
    hi = c.astype(jnp.bfloat16).astype(jnp.float32)
    r = c - hi
    mid = r.astype(jnp.bfloat16).astype(jnp.float32)
    lo = (r - mid).astype(jnp.bfloat16).astype(jnp.float32)
    return hi, mid, lo


_C_QA = 0
_C_KA = _C_QA + SWA_Q
_C_VA = _C_KA + SWA_KV
_C_QF = _C_VA + SWA_KV
_C_KF = _C_QF + FOX_W
_C_VF = _C_KF + FOX_W
_C_GA = _C_VF + FOX_W
_C_GB = _C_GA + D_MODEL
_C_ZF = _C_GB + D_MODEL
_N_PROJ = _C_ZF + LANES


def _in_proj_kernel(tiles_per_seq, x_ref, w_ref, bf_ref, tri_ref,
                    qa_ref, ka_ref, va_ref, qf_ref, kf_ref, vf_ref, ga_ref, gb_ref, carry_ref):
    i = pl.program_id(0)
    tm = x_ref.shape[0]
    xb = x_ref[...].astype(jnp.bfloat16)

    def proj(c0, n):
        return jnp.dot(xb, w_ref[:, c0:c0 + n], preferred_element_type=jnp.float32)

    scale = HEAD_DIM ** -0.5
    qa_ref[...] = (proj(_C_QA, SWA_Q) * scale).astype(jnp.bfloat16)
    ka_ref[...] = proj(_C_KA, SWA_KV).astype(jnp.bfloat16)
    va_ref[...] = proj(_C_VA, SWA_KV).astype(jnp.bfloat16)
    ga_ref[...] = proj(_C_GA, D_MODEL)
    gb_ref[...] = proj(_C_GB, D_MODEL)

    z = proj(_C_ZF, LANES) + bf_ref[...]
    log_f = jnp.minimum(z, 0.0) - jnp.log(1.0 + jnp.exp(-jnp.abs(z)))

    @pl.when(i % tiles_per_seq == 0)
    def _():
        carry_ref[...] = jnp.zeros_like(carry_ref)

    tri = tri_ref[...]
    h3 = _split3(log_f)
    c = carry_ref[...]
    for part in h3:
        c = c + jnp.dot(tri, part.astype(jnp.bfloat16), preferred_element_type=jnp.float32)
    carry_ref[...] = c[tm - 1:tm, :]

    qf = proj(_C_QF, FOX_W) * scale
    kf = proj(_C_KF, FOX_W)
    vf = proj(_C_VF, FOX_W)
    lane = lax.broadcasted_iota(jnp.int32, (tm, LANES), 1)
    one = jnp.float32(1.0)
    zero = jnp.float32(0.0)
    for h in range(FOX_HEADS):
        pair = (h // 2) * LANES
        qp = qf[:, pair:pair + LANES]
        kp = kf[:, pair:pair + LANES]
        vp = vf[:, pair:pair + LANES]
        if h % 2 == 1:
            qp = pltpu.roll(qp, HEAD_DIM, axis=1)
            kp = pltpu.roll(kp, HEAD_DIM, axis=1)
            vp = pltpu.roll(vp, HEAD_DIM, axis=1)
        ch = jnp.broadcast_to(c[:, h:h + 1], (tm, LANES))
        hi, mid, lo = _split3(ch)
        q_aug = jnp.where(lane == AUG, hi, jnp.where(lane == AUG + 1, mid, jnp.where(
            lane == AUG + 2, lo, jnp.where(lane < AUG + 6, one, zero))))
        k_aug = jnp.where(lane < AUG + 3, one, jnp.where(lane == AUG + 3, -hi, jnp.where(
            lane == AUG + 4, -mid, jnp.where(lane == AUG + 5, -lo, zero))))
        qf_ref[h] = jnp.where(lane < AUG, qp, q_aug).astype(jnp.bfloat16)
        kf_ref[h] = jnp.where(lane < AUG, kp, k_aug).astype(jnp.bfloat16)
        vf_ref[h] = jnp.where(lane < AUG, vp, jnp.where(lane == AUG, one, zero)).astype(jnp.bfloat16)


def _in_proj(x2, w_main, bf_row, seq_len):
    t = x2.shape[0]
    tm = min(TM_PROJ, seq_len)
    tri = jnp.tril(jnp.ones((tm, tm), jnp.bfloat16))
    row = lambda n: pl.BlockSpec((tm, n), lambda i: (i, 0))
    full = lambda a: pl.BlockSpec(a.shape, lambda i: (0,) * a.ndim)
    slab = pl.BlockSpec((FOX_HEADS, tm, LANES), lambda i: (0, i, 0))
    bf16, f32 = jnp.bfloat16, jnp.float32
    slab_shape = jax.ShapeDtypeStruct((FOX_HEADS, t, LANES), bf16)
    out_shape = (
        jax.ShapeDtypeStruct((t, SWA_Q), bf16), jax.ShapeDtypeStruct((t, SWA_KV), bf16),
        jax.ShapeDtypeStruct((t, SWA_KV), bf16), slab_shape, slab_shape, slab_shape,
        jax.ShapeDtypeStruct((t, D_MODEL), f32), jax.ShapeDtypeStruct((t, D_MODEL), f32))
    return pl.pallas_call(
        functools.partial(_in_proj_kernel, seq_len // tm),
        grid=(t // tm,),
        in_specs=[row(D_MODEL), full(w_main), full(bf_row), full(tri)],
        out_specs=(row(SWA_Q), row(SWA_KV), row(SWA_KV), slab, slab, slab, row(D_MODEL), row(D_MODEL)),
        out_shape=out_shape,
        scratch_shapes=[pltpu.VMEM((1, LANES), f32)],
        compiler_params=_cparams(("arbitrary",)),
        name="in_proj",
    )(x2, w_main, bf_row, tri)


def _swa_kernel(blocks_per_seq, sink_ref, q_ref, kp_ref, kc_ref, vp_ref, vc_ref, bias_ref, o_ref):
    n = pl.program_id(0)
    first_key = jnp.where((n % blocks_per_seq) > 0, 0, Q_BLOCK)
    kk = jnp.concatenate([kp_ref[...], kc_ref[...]], axis=0)
    vv = jnp.concatenate([vp_ref[...], vc_ref[...]], axis=0)
    qi = lax.broadcasted_iota(jnp.int32, (Q_BLOCK, 2 * Q_BLOCK), 0)
    kj = lax.broadcasted_iota(jnp.int32, (Q_BLOCK, 2 * Q_BLOCK), 1)
    dist = qi + Q_BLOCK - kj
    valid = (dist >= 0) & (dist < SWA_WINDOW) & (kj >= first_key)
    lane = lax.broadcasted_iota(jnp.int32, (Q_BLOCK, LANES), 1)
    lo_half = lane < HEAD_DIM
    group = SWA_Q_HEADS // SWA_KV_HEADS
    for j in range(group):
        qs = q_ref[:, j * LANES:(j + 1) * LANES]
        outs = []
        for kv in range(SWA_KV_HEADS):
            head = kv * group + j
            qm = jnp.where(lo_half if kv == 0 else ~lo_half, qs, jnp.zeros_like(qs))
            s = lax.dot_general(qm, kk, (((1,), (1,)), ((), ())), preferred_element_type=jnp.float32)
            s = jnp.where(valid, s + bias_ref[head], -jnp.inf)
            sink = sink_ref[head]
            m = jnp.maximum(jnp.max(s, axis=-1, keepdims=True), sink)
            p = jnp.exp(s - m)
            denom = jnp.sum(p, axis=-1, keepdims=True) + jnp.exp(sink - m)
            pn = (p / denom).astype(jnp.bfloat16)
            outs.append(jnp.dot(pn, vv, preferred_element_type=jnp.float32))
        o_ref[:, j * LANES:(j + 1) * LANES] = jnp.where(lo_half, outs[0], outs[1]).astype(o_ref.dtype)


def _swa(qa, ka, va, sinks, bias_tab, seq_len):
    t = qa.shape[0]
    bps = seq_len // Q_BLOCK
    cur = lambda n, s: (n, 0)
    prev = lambda n, s: (jnp.maximum(n - 1, 0), 0)
    grid_spec = pltpu.PrefetchScalarGridSpec(
        num_scalar_prefetch=1,
        grid=(t // Q_BLOCK,),
        in_specs=[pl.BlockSpec((Q_BLOCK, SWA_Q), cur),
                  pl.BlockSpec((Q_BLOCK, SWA_KV), prev), pl.BlockSpec((Q_BLOCK, SWA_KV), cur),
                  pl.BlockSpec((Q_BLOCK, SWA_KV), prev), pl.BlockSpec((Q_BLOCK, SWA_KV), cur),
                  pl.BlockSpec(bias_tab.shape, lambda n, s: (0, 0, 0))],
        out_specs=pl.BlockSpec((Q_BLOCK, SWA_Q), cur))
    return pl.pallas_call(
        functools.partial(_swa_kernel, bps),
        grid_spec=grid_spec,
        out_shape=jax.ShapeDtypeStruct((t, SWA_Q), jnp.bfloat16),
        compiler_params=_cparams(("arbitrary",)),
        name="swa",
    )(sinks, qa, ka, ka, va, va, bias_tab)


def _t5_bias_table(rel_bias):
    i = np.arange(Q_BLOCK)[:, None]
    j = np.arange(2 * Q_BLOCK)[None, :]
    n = jnp.asarray(np.maximum(i + Q_BLOCK - j, 0))
    max_exact = N_BUCKETS // 2
    nf = jnp.maximum(n, 1).astype(jnp.float32)
    large = max_exact + (jnp.log(nf / max_exact) / math.log(MAX_DISTANCE / max_exact)
                         * (N_BUCKETS - max_exact)).astype(jnp.int32)
    large = jnp.minimum(large, N_BUCKETS - 1)
    bucket = jnp.where(n < max_exact, n, large)
    onehot = bucket[None] == jnp.arange(N_BUCKETS)[:, None, None]
    rb = rel_bias.astype(jnp.float32)
    return jnp.sum(jnp.where(onehot[:, None], rb[:, :, None, None], 0.0), axis=0)


def _fox_kernel(tk, q_ref, k_ref, v_ref, o_ref):
    qi = pl.program_id(2)
    tq = q_ref.shape[1]
    sub = tq // FOX_SUBTILES
    n_full = (qi * tq) // tk
    col = lax.broadcasted_iota(jnp.int32, (sub, tk), 1)
    chains = [(hh, r) for hh in range(2) for r in range(FOX_SUBTILES)]
    qs = [q_ref[hh, r * sub:(r + 1) * sub, :] for hh, r in chains]
    rows = [lax.broadcasted_iota(jnp.int32, (sub, tk), 0) + (qi * tq + r * sub) for _, r in chains]

    def step(j, carry, masked):
        start = pl.multiple_of(j * tk, tk)
        new = []
        for c, (hh, _) in enumerate(chains):
            m, acc = carry[c]
            k = k_ref[hh, pl.ds(start, tk), :]
            v = v_ref[hh, pl.ds(start, tk), :]
            s = lax.dot_general(qs[c], k, (((1,), (1,)), ((), ())), preferred_element_type=jnp.float32)
            if masked:
                s = jnp.where(col + j * tk <= rows[c], s, -jnp.inf)
            m_new = jnp.maximum(m, jnp.max(s, axis=-1, keepdims=True))
            p = jnp.exp(s - m_new).astype(jnp.bfloat16)
            acc = jnp.exp(m - m_new) * acc + jnp.dot(p, v, preferred_element_type=jnp.float32)
            new.append((m_new, acc))
        return tuple(new)

    init1 = (jnp.full((sub, 1), -jnp.inf, jnp.float32), jnp.zeros((sub, LANES), jnp.float32))
    carry = lax.fori_loop(0, n_full, functools.partial(step, masked=False), (init1,) * len(chains))
    carry = step(n_full, carry, masked=True)
    outs = []
    for hh in range(2):
        acc = jnp.concatenate([carry[c][1] for c, (h, _) in enumerate(chains) if h == hh], axis=0)
        outs.append(acc / acc[:, AUG:AUG + 1])
    lane = lax.broadcasted_iota(jnp.int32, (tq, LANES), 1)
    o_ref[...] = jnp.where(lane < HEAD_DIM, outs[0], pltpu.roll(outs[1], HEAD_DIM, axis=1)).astype(o_ref.dtype)


def _fox(qf, kf, vf, batch, seq_len):
    t = qf.shape[1]
    tq = min(FOX_TQ, seq_len)
    tk = min(FOX_TK, seq_len)
    assert tk % tq == 0 and seq_len % tk == 0
    nq = seq_len // tq
    pairs = FOX_HEADS // 2
    seq = pl.BlockSpec((2, seq_len, LANES), lambda b, hp, i: (hp, b, 0))
    return pl.pallas_call(
        functools.partial(_fox_kernel, tk),
        grid=(batch, pairs, nq),
        in_specs=[pl.BlockSpec((2, tq, LANES), lambda b, hp, i: (hp, b * nq + i, 0)), seq, seq],
        out_specs=pl.BlockSpec((tq, LANES), lambda b, hp, i: (b * nq + i, hp)),
        out_shape=jax.ShapeDtypeStruct((t, FOX_W), jnp.bfloat16),
        compiler_params=_cparams(("arbitrary", "arbitrary", "arbitrary")),
        name="fox",
    )(qf, kf, vf)


def _layer_norm(h, g, b):
    mu = jnp.mean(h, axis=-1, keepdims=True)
    d = h - mu
    var = jnp.mean(d * d, axis=-1, keepdims=True)
    return d * lax.rsqrt(var + LN_EPS) * g + b


def _pack_rows(ref, x):
    rows = x.shape[0]
    bits = pltpu.bitcast(x.astype(jnp.bfloat16).astype(jnp.float32), jnp.uint32)
    for c in range(ROW_WORDS):
        lo = bits[:, c * LANES:(c + 1) * LANES] >> 16
        hi = bits[:, HALF_D + c * LANES:HALF_D + (c + 1) * LANES] & jnp.uint32(HI_MASK)
        ref[pl.ds(c, rows, stride=ROW_WORDS), :] = lo | hi


def _unpack_rows(ref, first, rows):
    los, his = [], []
    for c in range(ROW_WORDS):
        w = ref[pl.ds(first * ROW_WORDS + c, rows, stride=ROW_WORDS), :]
        los.append(pltpu.bitcast(w << 16, jnp.float32))
        his.append(pltpu.bitcast(w & jnp.uint32(HI_MASK), jnp.float32))
    return jnp.concatenate(los + his, axis=1)


def _packed(ref, first, rows=1):
    return ref.at[pl.ds(pl.multiple_of(first * ROW_WORDS, ROW_WORDS), rows * ROW_WORDS), :]


def _packed_spec(rows, index_map):
    return pl.BlockSpec((rows * ROW_WORDS, LANES), index_map)


def _packed_shape(rows):
    return jax.ShapeDtypeStruct((rows * ROW_WORDS, LANES), jnp.uint32)


def _merge_kernel(alpha, x_ref, ya_ref, yb_ref, ga_ref, gb_ref, wa_ref, wb_ref, wo_ref, g_ref, b_ref,
                  x1_ref, x1p_ref):
    pa = jnp.dot(ya_ref[...], wa_ref[...], preferred_element_type=jnp.float32)
    pb = jnp.dot(yb_ref[...], wb_ref[...], preferred_element_type=jnp.float32)
    mix = _sigmoid(ga_ref[...]) * pa + _sigmoid(gb_ref[...]) * pb
    out = jnp.dot(mix.astype(jnp.bfloat16), wo_ref[...], preferred_element_type=jnp.float32)
    x1 = _layer_norm(alpha * x_ref[...] + out, g_ref[...], b_ref[...])
    x1_ref[...] = x1
    _pack_rows(x1p_ref, x1)


def _merge(alpha, x2, ya, yb, ga, gb, wa, wb, wo, g, b):
    t = x2.shape[0]
    tm = TM_MERGE
    row = lambda n: pl.BlockSpec((tm, n), lambda i: (i, 0))
    full = lambda a: pl.BlockSpec(a.shape, lambda i: (0,) * a.ndim)
    return pl.pallas_call(
        functools.partial(_merge_kernel, alpha),
        grid=(t // tm,),
        in_specs=[row(D_MODEL), row(SWA_Q), row(FOX_W), row(D_MODEL), row(D_MODEL),
                  full(wa), full(wb), full(wo), full(g), full(b)],
        out_specs=(row(D_MODEL), _packed_spec(tm, lambda i: (i, 0))),
        out_shape=(jax.ShapeDtypeStruct((t, D_MODEL), jnp.float32), _packed_shape(t)),
        compiler_params=_cparams(("arbitrary",)),
        name="merge",
    )(x2, ya, yb, ga, gb, wa, wb, wo, g, b)


def _first_argmax(v, idx):
    m = jnp.max(v, axis=0, keepdims=True)
    first = jnp.min(jnp.where(v == m, idx, v.shape[0]), axis=0, keepdims=True)
    return m, first


def _router_kernel(x_ref, wr_ref, rb_ref, tri_ref, e_ref, w_ref, r_ref, cnt_ref, carry_ref):
    i = pl.program_id(0)
    tm = x_ref.shape[0]

    @pl.when(i == 0)
    def _():
        carry_ref[...] = jnp.zeros_like(carry_ref)

    xb = x_ref[...].astype(jnp.bfloat16)
    logits = lax.dot_general(wr_ref[...], xb, (((1,), (1,)), ((), ())),
                             preferred_element_type=jnp.float32)
    scores = _sigmoid(logits)
    biased = scores + rb_ref[...]

    in_group = lax.broadcasted_iota(jnp.int32, (GROUP_SIZE, tm), 0)
    grows = []
    for g in range(N_GROUPS):
        blk = biased[g * GROUP_SIZE:(g + 1) * GROUP_SIZE, :]
        m1, f1 = _first_argmax(blk, in_group)
        m2 = jnp.max(jnp.where(in_group == f1, -jnp.inf, blk), axis=0, keepdims=True)
        grows.append(m1 + m2)
    gscore = jnp.concatenate(grows, axis=0)

    gidx = lax.broadcasted_iota(jnp.int32, (N_GROUPS, tm), 0)
    gwork = gscore
    gsel = jnp.zeros((N_GROUPS, tm), jnp.float32)
    for _ in range(TOPK_GROUPS):
        _, first = _first_argmax(gwork, gidx)
        hit = gidx == first
        gsel = jnp.where(hit, 1.0, gsel)
        gwork = jnp.where(hit, -jnp.inf, gwork)

    eidx = lax.broadcasted_iota(jnp.int32, (N_EXPERTS, tm), 0)
    work = jnp.concatenate(
        [jnp.where(jnp.broadcast_to(gsel[g:g + 1, :], (GROUP_SIZE, tm)) > 0.0,
                   biased[g * GROUP_SIZE:(g + 1) * GROUP_SIZE, :], -jnp.inf) for g in range(N_GROUPS)], axis=0)
    picks = []
    sel = jnp.zeros((N_EXPERTS, tm), jnp.bool_)
    for _ in range(TOP_K):
        _, first = _first_argmax(work, eidx)
        hit = eidx == first
        picks.append((first, hit))
        sel = sel | hit
        work = jnp.where(hit, -jnp.inf, work)

    wsum = jnp.sum(jnp.where(sel, scores, 0.0), axis=0, keepdims=True)
    sel_b = jnp.where(sel, 1.0, 0.0).astype(jnp.bfloat16)
    before = carry_ref[...] + jnp.dot(sel_b, tri_ref[...], preferred_element_type=jnp.float32)
    e_rows, w_rows, r_rows = [], [], []
    for first, hit in picks:
        e_rows.append(first)
        sc = jnp.sum(jnp.where(hit, scores, 0.0), axis=0, keepdims=True)
        w_rows.append(sc / wsum * ROUTED_SCALE)
        r_rows.append(jnp.sum(jnp.where(hit, before, 0.0), axis=0, keepdims=True).astype(jnp.int32))
    e_ref[...] = jnp.concatenate(e_rows, axis=0)
    w_ref[...] = jnp.concatenate(w_rows, axis=0)
    r_ref[...] = jnp.concatenate(r_rows, axis=0)
    total = carry_ref[...] + jnp.sum(sel_b.astype(jnp.float32), axis=1, keepdims=True)
    carry_ref[...] = total
    cnt_ref[...] = jnp.broadcast_to(total, cnt_ref.shape).astype(jnp.int32)


def _router(x1, wr_t, rb_col):
    t = x1.shape[0]
    tm = TM_ROUTE
    tri = jnp.triu(jnp.ones((tm, tm), jnp.bfloat16), k=1)
    full = lambda a: pl.BlockSpec(a.shape, lambda i: (0,) * a.ndim)
    kt = pl.BlockSpec((TOP_K, tm), lambda i: (0, i))
    return pl.pallas_call(
        _router_kernel,
        grid=(t // tm,),
        in_specs=[pl.BlockSpec((tm, D_MODEL), lambda i: (i, 0)), full(wr_t), full(rb_col), full(tri)],
        out_specs=(kt, kt, kt, pl.BlockSpec((N_EXPERTS, LANES), lambda i: (0, 0))),
        out_shape=(jax.ShapeDtypeStruct((TOP_K, t), jnp.int32), jax.ShapeDtypeStruct((TOP_K, t), jnp.float32),
                   jax.ShapeDtypeStruct((TOP_K, t), jnp.int32),
                   jax.ShapeDtypeStruct((N_EXPERTS, LANES), jnp.int32)),
        scratch_shapes=[pltpu.VMEM((N_EXPERTS, 1), jnp.float32)],
        compiler_params=_cparams(("arbitrary",)),
        name="router",
    )(x1, wr_t, rb_col, tri)


def _plan_kernel(ps_ref, e_ref, r_ref, d_ref):
    tm = e_ref.shape[1]
    eidx = lax.broadcasted_iota(jnp.int32, (N_EXPERTS, tm), 0)
    rows = []
    for k in range(TOP_K):
        hit = eidx == e_ref[k:k + 1, :]
        base = jnp.sum(jnp.where(hit, ps_ref[...], 0.0), axis=0, keepdims=True)
        rows.append(base.astype(jnp.int32) + r_ref[k:k + 1, :])
    d_ref[...] = jnp.concatenate(rows, axis=0)


def _plan(pstart_col, top_e, rank):
    t = top_e.shape[1]
    tm = min(TM_PLAN, t)
    kt = pl.BlockSpec((TOP_K, tm), lambda i: (0, i))
    return pl.pallas_call(
        _plan_kernel,
        grid=(t // tm,),
        in_specs=[pl.BlockSpec(pstart_col.shape, lambda i: (0, 0)), kt, kt],
        out_specs=kt,
        out_shape=jax.ShapeDtypeStruct((TOP_K, t), jnp.int32),
        compiler_params=_cparams(("arbitrary",)),
        name="plan",
    )(pstart_col, top_e, rank)


def _dispatch_kernel(pend_ref, padded_ref, nu_ref, dest_hbm, x_ref, xs_hbm, idx_smem, zero_vmem, sem, zsem):
    i = pl.program_id(0)
    n_blocks = xs_hbm.shape[0] // (EXPERT_BLOCK * ROW_WORDS)

    def zero_copy(first_row):
        return pltpu.make_async_copy(zero_vmem, _packed(xs_hbm, first_row, EXPERT_BLOCK), zsem)

    @pl.when(i == 0)
    def _():
        zero_vmem[...] = jnp.zeros_like(zero_vmem)

        def fill(e, c):
            @pl.when(padded_ref[e] > 0)
            def _():
                zero_copy(pend_ref[e] - EXPERT_BLOCK).start()
            return c
        lax.fori_loop(0, N_EXPERTS, fill, 0)

        def fill_tail(b, c):
            zero_copy(b * EXPERT_BLOCK).start()
            return c
        lax.fori_loop(nu_ref[0], n_blocks, fill_tail, 0)

        def drain(e, c):
            @pl.when(padded_ref[e] > 0)
            def _():
                zero_copy(0).wait()
            return c
        lax.fori_loop(0, N_EXPERTS, drain, 0)

        def drain_tail(b, c):
            zero_copy(0).wait()
            return c
        lax.fori_loop(nu_ref[0], n_blocks, drain_tail, 0)

    n_idx = TM_MOVE * TOP_K
    pltpu.sync_copy(dest_hbm.at[pl.ds(i * n_idx, n_idx)], idx_smem)

    def issue(t, c):
        for k in range(TOP_K):
            pltpu.make_async_copy(_packed(x_ref, t), _packed(xs_hbm, idx_smem[t * TOP_K + k]),
                                  sem).start(priority=k % 2)
        return c
    lax.fori_loop(0, TM_MOVE, issue, 0)

    for k in range(TOP_K):
        pltpu.make_async_copy(x_ref, _packed(xs_hbm, 0, TM_MOVE), sem).wait()


def _dispatch(dest_flat, x1p, pend, padded, n_used, n_rows):
    t = x1p.shape[0] // ROW_WORDS
    grid_spec = pltpu.PrefetchScalarGridSpec(
        num_scalar_prefetch=3,
        grid=(t // TM_MOVE,),
        in_specs=[pl.BlockSpec(memory_space=pl.ANY), _packed_spec(TM_MOVE, lambda i, pe, pa, nu: (i, 0))],
        out_specs=pl.BlockSpec(memory_space=pl.ANY),
        scratch_shapes=[pltpu.SMEM((TM_MOVE * TOP_K,), jnp.int32),
                        pltpu.VMEM((EXPERT_BLOCK * ROW_WORDS, LANES), jnp.uint32),
                        pltpu.SemaphoreType.DMA, pltpu.SemaphoreType.DMA])
    return pl.pallas_call(
        _dispatch_kernel,
        grid_spec=grid_spec,
        out_shape=_packed_shape(n_rows),
        compiler_params=_cparams(("arbitrary",)),
        name="dispatch",
    )(pend, padded, n_used, dest_flat, x1p)


def _experts_kernel(be_ref, nu_ref, xs_ref, *refs):
    nc = EXPERT_CHAINS
    w_refs, y_ref, w_bf = refs[:3 * nc], refs[3 * nc], refs[3 * nc + 1:]
    i = pl.program_id(0)
    n_used = nu_ref[0]

    for c in range(nc):
        blk = i * nc + c
        changed = (i == 0) | (be_ref[blk] != be_ref[jnp.maximum(blk - nc, 0)])

        @pl.when((blk < n_used) & changed)
        def _(c=c):
            for j in range(3):
                w_bf[3 * c + j][...] = w_refs[3 * c + j][...].astype(jnp.bfloat16)

    @pl.when(i * nc < n_used)
    def _():
        xbs = [_unpack_rows(xs_ref, c * EXPERT_BLOCK, EXPERT_BLOCK).astype(jnp.bfloat16) for c in range(nc)]
        gus = [(jnp.dot(xbs[c], w_bf[3 * c][...], preferred_element_type=jnp.float32),
                jnp.dot(xbs[c], w_bf[3 * c + 1][...], preferred_element_type=jnp.float32)) for c in range(nc)]
        ys = []
        for c, (g, u) in enumerate(gus):
            h = (g * _sigmoid(g) * u).astype(jnp.bfloat16)
            ys.append(jnp.dot(h, w_bf[3 * c + 2][...], preferred_element_type=jnp.float32))
        for c, y in enumerate(ys):
            y = jnp.where(i * nc + c < n_used, y, 0.0)
            _pack_rows(y_ref.at[pl.ds(c * EXPERT_BLOCK * ROW_WORDS, EXPERT_BLOCK * ROW_WORDS), :], y)

    @pl.when(i * nc >= n_used)
    def _():
        y_ref[...] = jnp.zeros_like(y_ref)


def _experts(layer, block_e, n_used, xs, w_gate_e, w_up_e, w_down_e):
    nc = EXPERT_CHAINS
    n_blocks = xs.shape[0] // (EXPERT_BLOCK * ROW_WORDS)
    assert n_blocks % nc == 0
    w_specs = []
    for c in range(nc):
        wsel = lambda i, be, nu, c=c: (layer, be[i * nc + c], 0, 0)
        w_specs += [pl.BlockSpec((None, None, D_MODEL, D_EXPERT), wsel),
                    pl.BlockSpec((None, None, D_MODEL, D_EXPERT), wsel),
                    pl.BlockSpec((None, None, D_EXPERT, D_MODEL), wsel)]
    cache = [pltpu.VMEM((D_MODEL, D_EXPERT), jnp.bfloat16), pltpu.VMEM((D_MODEL, D_EXPERT), jnp.bfloat16),
             pltpu.VMEM((D_EXPERT, D_MODEL), jnp.bfloat16)]
    grid_spec = pltpu.PrefetchScalarGridSpec(
        num_scalar_prefetch=2,
        grid=(n_blocks // nc,),
        in_specs=[_packed_spec(nc * EXPERT_BLOCK, lambda i, be, nu: (jnp.minimum(i, (nu[0] - 1) // nc), 0))] + w_specs,
        out_specs=_packed_spec(nc * EXPERT_BLOCK, lambda i, be, nu: (i, 0)),
        scratch_shapes=cache * nc)
    return pl.pallas_call(
        _experts_kernel,
        grid_spec=grid_spec,
        out_shape=_packed_shape(n_blocks * EXPERT_BLOCK),
        compiler_params=_cparams(("arbitrary",)),
        name="experts",
    )(block_e, n_used, xs, *([w_gate_e, w_up_e, w_down_e] * nc))


def _combine_kernel(alpha, dest_hbm, y_hbm, w_ref, x1_ref, wg_ref, wu_ref, wd_ref, g_ref, b_ref,
                    o_ref, idx_smem0, idx_smem1, buf, sem):
    i = pl.program_id(0)
    n = pl.num_programs(0)
    n_idx = TM_MOVE * TOP_K

    def slab(slot, k):
        return (slot * TOP_K + k) * TM_MOVE

    def start_gather(step, slot):
        idx_smem = (idx_smem0, idx_smem1)[slot]
        pltpu.sync_copy(dest_hbm.at[pl.ds(step * n_idx, n_idx)], idx_smem)

        def issue(t, c):
            for k in range(TOP_K):
                pltpu.make_async_copy(_packed(y_hbm, idx_smem[t * TOP_K + k]),
                                      _packed(buf, slab(slot, k) + t), sem.at[slot]).start(priority=k % 2)
            return c
        lax.fori_loop(0, TM_MOVE, issue, 0)

    @pl.when(i == 0)
    def _():
        start_gather(0, 0)

    def run(slot):
        @pl.when(i + 1 < n)
        def _():
            start_gather(i + 1, 1 - slot)

        for k in range(TOP_K):
            pltpu.make_async_copy(_packed(y_hbm, 0, TM_MOVE), _packed(buf, slab(slot, k), TM_MOVE),
                                  sem.at[slot]).wait()

        w = w_ref[...]
        routed = jnp.zeros((TM_MOVE, D_MODEL), jnp.float32)
        for k in range(TOP_K):
            routed = routed + w[:, k:k + 1] * _unpack_rows(buf, slab(slot, k), TM_MOVE)

        x1 = x1_ref[...]
        xb = x1.astype(jnp.bfloat16)
        gs = jnp.dot(xb, wg_ref[...], preferred_element_type=jnp.float32)
        us = jnp.dot(xb, wu_ref[...], preferred_element_type=jnp.float32)
        hs = gs * _sigmoid(gs) * us
        shared = jnp.dot(hs.astype(jnp.bfloat16), wd_ref[...], preferred_element_type=jnp.float32)
        o_ref[...] = _layer_norm(alpha * x1 + (shared + routed), g_ref[...], b_ref[...])

    for slot in range(2):
        pl.when(i % 2 == slot)(functools.partial(run, slot))


def _combine(alpha, dest_flat, y, w_tk, x1, wg, wu, wd, g, b):
    t = x1.shape[0]
    tm = TM_MOVE
    full = lambda a: pl.BlockSpec(a.shape, lambda i: (0,) * a.ndim)
    return pl.pallas_call(
        functools.partial(_combine_kernel, alpha),
        grid=(t // tm,),
        in_specs=[pl.BlockSpec(memory_space=pl.ANY), pl.BlockSpec(memory_space=pl.ANY),
                  pl.BlockSpec((tm, TOP_K), lambda i: (i, 0)), pl.BlockSpec((tm, D_MODEL), lambda i: (i, 0)),
                  full(wg), full(wu), full(wd), full(g), full(b)],
        out_specs=pl.BlockSpec((tm, D_MODEL), lambda i: (i, 0)),
        out_shape=jax.ShapeDtypeStruct((t, D_MODEL), jnp.float32),
        scratch_shapes=[pltpu.SMEM((tm * TOP_K,), jnp.int32), pltpu.SMEM((tm * TOP_K,), jnp.int32),
                        pltpu.VMEM((2 * TOP_K * tm * ROW_WORDS, LANES), jnp.uint32),
                        pltpu.SemaphoreType.DMA((2,))],
        compiler_params=_cparams(("arbitrary",)),
        name="combine",
    )(dest_flat, y, w_tk, x1, wg, wu, wd, g, b)


def _prep_in_weights(w, b_forget):
    o = np.cumsum((0,) + IN_SIZES)
    group = SWA_Q_HEADS // SWA_KV_HEADS
    qa = w[:, o[0]:o[1]].reshape(D_MODEL, SWA_KV_HEADS, group, HEAD_DIM)
    qa = qa.transpose(0, 2, 1, 3).reshape(D_MODEL, SWA_Q)
    zf = jnp.pad(w[:, o[6]:o[7]], ((0, 0), (0, LANES - FOX_HEADS)))
    w_main = jnp.concatenate([qa, w[:, o[1]:o[6]], w[:, o[7]:o[9]], zf], axis=1).astype(jnp.bfloat16)
    bf_row = jnp.pad(b_forget, (0, LANES - FOX_HEADS)).reshape(1, LANES).astype(jnp.float32)
    return w_main, bf_row


def _prep_proj_a(w):
    group = SWA_Q_HEADS // SWA_KV_HEADS
    w4 = w.reshape(SWA_KV_HEADS, group, HEAD_DIM, D_MODEL).transpose(1, 0, 2, 3)
    return w4.reshape(SWA_Q, D_MODEL).astype(jnp.bfloat16)


def _route_plan(counts, n_tokens):
    n_assign = n_tokens * TOP_K
    step_rows = EXPERT_BLOCK * EXPERT_CHAINS
    n_rows = (n_assign + N_EXPERTS * (EXPERT_BLOCK - 1) + step_rows - 1) // step_rows * step_rows
    n_blocks = n_rows // EXPERT_BLOCK
    padded = (counts + EXPERT_BLOCK - 1) // EXPERT_BLOCK * EXPERT_BLOCK
    pend = jnp.cumsum(padded)
    pstart = pend - padded
    n_used = pend[-1] // EXPERT_BLOCK
    blk = jnp.minimum(jnp.arange(n_blocks, dtype=jnp.int32), n_used - 1) * EXPERT_BLOCK
    block_e = jnp.minimum(jnp.sum((pend[None, :] <= blk[:, None]).astype(jnp.int32), axis=1), N_EXPERTS - 1)
    return (pstart.astype(jnp.float32).reshape(N_EXPERTS, 1), block_e.astype(jnp.int32),
            n_used.reshape(1).astype(jnp.int32), pend.astype(jnp.int32), padded.astype(jnp.int32), n_rows)


def kernel(x, w_in, b_forget, attn_sinks, rel_bias, w_proj_a, w_proj_b, w_out, ln1_g, ln1_b,
           w_router, router_bias, w_gate_e, w_up_e, w_down_e, w_gate_s, w_up_s, w_down_s, ln2_g, ln2_b):
    batch, seq_len, d = x.shape
    depth = w_in.shape[0]
    alpha = (2 * depth) ** 0.25
    t = batch * seq_len
    bf16 = jnp.bfloat16
    x2 = x.reshape(t, d)
    bias_tab = _t5_bias_table(rel_bias)
    for l in range(depth):
        w_main, bf_row = _prep_in_weights(w_in[l], b_forget[l])
        qa, ka, va, qf, kf, vf, ga, gb = _in_proj(x2, w_main, bf_row, seq_len)
        ya = _swa(qa, ka, va, attn_sinks[l].astype(jnp.float32), bias_tab, seq_len)
        yb = _fox(qf, kf, vf, batch, seq_len)
        x1, x1p = _merge(alpha, x2, ya, yb, ga, gb, _prep_proj_a(w_proj_a[l]), w_proj_b[l].astype(bf16),
                         w_out[l].astype(bf16), ln1_g[l].reshape(1, d), ln1_b[l].reshape(1, d))
        top_e, w_kt, rank, counts = _router(x1, w_router[l].T.astype(bf16),
                                            router_bias[l].reshape(N_EXPERTS, 1).astype(jnp.float32))
        pstart_col, block_e, n_used, pend, padded, n_rows = _route_plan(counts[:, 0], t)
        dest_flat = _plan(pstart_col, top_e, rank).T.reshape(-1)
        xs = _dispatch(dest_flat, x1p, pend, padded, n_used, n_rows)
        y = _experts(l, block_e, n_used, xs, w_gate_e, w_up_e, w_down_e)
        x2 = _combine(alpha, dest_flat, y, w_kt.T, x1, w_gate_s[l].astype(bf16), w_up_s[l].astype(bf16),
                      w_down_s[l].astype(bf16), ln2_g[l].reshape(1, d), ln2_b[l].reshape(1, d))
    return x2.reshape(batch, seq_len, d)
```

```python
import functools
import math

import jax
import jax.numpy as jnp
import numpy as np
from jax import lax
from jax.experimental import pallas as pl
from jax.experimental.pallas import tpu as pltpu

D_MODEL = 1024
HEAD_DIM = 64
SWA_Q_HEADS = 8
SWA_KV_HEADS = 2
SWA_WINDOW = 128
FOX_HEADS = 8
Q_BLOCK = 128
N_BUCKETS = 32
MAX_DISTANCE = 128
N_EXPERTS = 256
TOP_K = 8
N_GROUPS = 8
TOPK_GROUPS = 4
GROUP_SIZE = N_EXPERTS // N_GROUPS
D_EXPERT = 256
D_SHARED = 256
ROUTED_SCALE = 2.5
LN_EPS = 1e-5

SWA_Q = SWA_Q_HEADS * HEAD_DIM
SWA_KV = SWA_KV_HEADS * HEAD_DIM
FOX_W = FOX_HEADS * HEAD_DIM
IN_SIZES = (SWA_Q, SWA_KV, SWA_KV, FOX_W, FOX_W, FOX_W, FOX_HEADS, D_MODEL, D_MODEL)

LANES = 128
HALF_D = D_MODEL // 2
ROW_WORDS = HALF_D // LANES
VMEM_LIMIT = 56 * 1024 * 1024
HI_MASK = 0xFFFF0000

TM_PROJ = 512
TM_MERGE = 256
TM_ROUTE = 256
TM_DISPATCH = 1024
TM_COMBINE = 512
TM_PLAN = 2048
EXPERT_BLOCK = 512
EXPERT_SUB = 256
FOX_TILE = 1024
FOX_DIAG_BANDS = 2

AUG = HEAD_DIM


def _cparams(sem):
    return pltpu.CompilerParams(dimension_semantics=sem, vmem_limit_bytes=VMEM_LIMIT)


def _sigmoid(x):
    return 1.0 / (1.0 + jnp.exp(-x))


def _split3(c):
    hi = c.astype(jnp.bfloat16).astype(jnp.float32)
    r = c - hi
    mid = r.astype(jnp.bfloat16).astype(jnp.float32)
    lo = (r - mid).astype(jnp.bfloat16).astype(jnp.float32)
    return hi, mid, lo


_C_QA = 0
_C_KA = _C_QA + SWA_Q
_C_VA = _C_KA + SWA_KV
_C_QF = _C_VA + SWA_KV
_C_KF = _C_QF + FOX_W
_C_VF = _C_KF + FOX_W
_C_GA = _C_VF + FOX_W
_C_GB = _C_GA + D_MODEL
_C_ZF = _C_GB + D_MODEL
_N_PROJ = _C_ZF + LANES


def _in_proj_kernel(tiles_per_seq, x_ref, w_ref, bf_ref, tri_ref,
                    qa_ref, ka_ref, va_ref, qf_ref, kf_ref, vf_ref, ga_ref, gb_ref, carry_ref):
    i = pl.program_id(0)
    tm = x_ref.shape[0]
    xb = x_ref[...].astype(jnp.bfloat16)

    def proj(c0, n):
        return jnp.dot(xb, w_ref[:, c0:c0 + n], preferred_element_type=jnp.float32)

    scale = HEAD_DIM ** -0.5
    qa_ref[...] = (proj(_C_QA, SWA_Q) * scale).astype(jnp.bfloat16)
    ka_ref[...] = proj(_C_KA, SWA_KV).astype(jnp.bfloat16)
    va_ref[...] = proj(_C_VA, SWA_KV).astype(jnp.bfloat16)
    ga_ref[...] = proj(_C_GA, D_MODEL)
    gb_ref[...] = proj(_C_GB, D_MODEL)

    z = proj(_C_ZF, LANES) + bf_ref[...]
    log_f = jnp.minimum(z, 0.0) - jnp.log(1.0 + jnp.exp(-jnp.abs(z)))

    @pl.when(i % tiles_per_seq == 0)
    def _():
        carry_ref[...] = jnp.zeros_like(carry_ref)

    tri = tri_ref[...]
    h3 = _split3(log_f)
    c = carry_ref[...]
    for part in h3:
        c = c + jnp.dot(tri, part.astype(jnp.bfloat16), preferred_element_type=jnp.float32)
    carry_ref[...] = c[tm - 1:tm, :]

    qf = proj(_C_QF, FOX_W) * scale
    kf = proj(_C_KF, FOX_W)
    vf = proj(_C_VF, FOX_W)
    lane = lax.broadcasted_iota(jnp.int32, (tm, LANES), 1)
    one = jnp.float32(1.0)
    zero = jnp.float32(0.0)
    for h in range(FOX_HEADS):
        pair = (h // 2) * LANES
        qp = qf[:, pair:pair + LANES]
        kp = kf[:, pair:pair + LANES]
        vp = vf[:, pair:pair + LANES]
        if h % 2 == 1:
            qp = pltpu.roll(qp, HEAD_DIM, axis=1)
            kp = pltpu.roll(kp, HEAD_DIM, axis=1)
            vp = pltpu.roll(vp, HEAD_DIM, axis=1)
        ch = jnp.broadcast_to(c[:, h:h + 1], (tm, LANES))
        hi, mid, lo = _split3(ch)
        q_aug = jnp.where(lane == AUG, hi, jnp.where(lane == AUG + 1, mid, jnp.where(
            lane == AUG + 2, lo, jnp.where(lane < AUG + 6, one, zero))))
        k_aug = jnp.where(lane < AUG + 3, one, jnp.where(lane == AUG + 3, -hi, jnp.where(
            lane == AUG + 4, -mid, jnp.where(lane == AUG + 5, -lo, zero))))
        qf_ref[h] = jnp.where(lane < AUG, qp, q_aug).astype(jnp.bfloat16)
        kf_ref[h] = jnp.where(lane < AUG, kp, k_aug).astype(jnp.bfloat16)
        vf_ref[h] = jnp.where(lane < AUG, vp, jnp.where(lane == AUG, one, zero)).astype(jnp.bfloat16)


def _in_proj(x2, w_main, bf_row, seq_len):
    t = x2.shape[0]
    tm = min(TM_PROJ, seq_len)
    tri = jnp.tril(jnp.ones((tm, tm), jnp.bfloat16))
    row = lambda n: pl.BlockSpec((tm, n), lambda i: (i, 0))
    full = lambda a: pl.BlockSpec(a.shape, lambda i: (0,) * a.ndim)
    slab = pl.BlockSpec((FOX_HEADS, tm, LANES), lambda i: (0, i, 0))
    bf16, f32 = jnp.bfloat16, jnp.float32
    slab_shape = jax.ShapeDtypeStruct((FOX_HEADS, t, LANES), bf16)
    out_shape = (
        jax.ShapeDtypeStruct((t, SWA_Q), bf16), jax.ShapeDtypeStruct((t, SWA_KV), bf16),
        jax.ShapeDtypeStruct((t, SWA_KV), bf16), slab_shape, slab_shape, slab_shape,
        jax.ShapeDtypeStruct((t, D_MODEL), f32), jax.ShapeDtypeStruct((t, D_MODEL), f32))
    return pl.pallas_call(
        functools.partial(_in_proj_kernel, seq_len // tm),
        grid=(t // tm,),
        in_specs=[row(D_MODEL), full(w_main), full(bf_row), full(tri)],
        out_specs=(row(SWA_Q), row(SWA_KV), row(SWA_KV), slab, slab, slab, row(D_MODEL), row(D_MODEL)),
        out_shape=out_shape,
        scratch_shapes=[pltpu.VMEM((1, LANES), f32)],
        compiler_params=_cparams(("arbitrary",)),
        name="in_proj",
    )(x2, w_main, bf_row, tri)


def _swa_kernel(blocks_per_seq, sink_ref, q_ref, kp_ref, kc_ref, vp_ref, vc_ref, bias_ref, o_ref):
    n = pl.program_id(0)
    first_key = jnp.where((n % blocks_per_seq) > 0, 0, Q_BLOCK)
    kk = jnp.concatenate([kp_ref[...], kc_ref[...]], axis=0)
    vv = jnp.concatenate([vp_ref[...], vc_ref[...]], axis=0)
    qi = lax.broadcasted_iota(jnp.int32, (Q_BLOCK, 2 * Q_BLOCK), 0)
    kj = lax.broadcasted_iota(jnp.int32, (Q_BLOCK, 2 * Q_BLOCK), 1)
    dist = qi + Q_BLOCK - kj
    valid = (dist >= 0) & (dist < SWA_WINDOW) & (kj >= first_key)
    lane = lax.broadcasted_iota(jnp.int32, (Q_BLOCK, LANES), 1)
    lo_half = lane < HEAD_DIM
    group = SWA_Q_HEADS // SWA_KV_HEADS
    for j in range(group):
        qs = q_ref[:, j * LANES:(j + 1) * LANES]
        outs = []
        for kv in range(SWA_KV_HEADS):
            head = kv * group + j
            qm = jnp.where(lo_half if kv == 0 else ~lo_half, qs, jnp.zeros_like(qs))
            s = lax.dot_general(qm, kk, (((1,), (1,)), ((), ())), preferred_element_type=jnp.float32)
            s = jnp.where(valid, s + bias_ref[head], -jnp.inf)
            sink = sink_ref[head]
            m = jnp.maximum(jnp.max(s, axis=-1, keepdims=True), sink)
            p = jnp.exp(s - m)
            denom = jnp.sum(p, axis=-1, keepdims=True) + jnp.exp(sink - m)
            pn = (p / denom).astype(jnp.bfloat16)
            outs.append(jnp.dot(pn, vv, preferred_element_type=jnp.float32))
        o_ref[:, j * LANES:(j + 1) * LANES] = jnp.where(lo_half, outs[0], outs[1]).astype(o_ref.dtype)


def _swa(qa, ka, va, sinks, bias_tab, seq_len):
    t = qa.shape[0]
    bps = seq_len // Q_BLOCK
    cur = lambda n, s: (n, 0)
    prev = lambda n, s: (jnp.maximum(n - 1, 0), 0)
    grid_spec = pltpu.PrefetchScalarGridSpec(
        num_scalar_prefetch=1,
        grid=(t // Q_BLOCK,),
        in_specs=[pl.BlockSpec((Q_BLOCK, SWA_Q), cur),
                  pl.BlockSpec((Q_BLOCK, SWA_KV), prev), pl.BlockSpec((Q_BLOCK, SWA_KV), cur),
                  pl.BlockSpec((Q_BLOCK, SWA_KV), prev), pl.BlockSpec((Q_BLOCK, SWA_KV), cur),
                  pl.BlockSpec(bias_tab.shape, lambda n, s: (0, 0, 0))],
        out_specs=pl.BlockSpec((Q_BLOCK, SWA_Q), cur))
    return pl.pallas_call(
        functools.partial(_swa_kernel, bps),
        grid_spec=grid_spec,
        out_shape=jax.ShapeDtypeStruct((t, SWA_Q), jnp.bfloat16),
        compiler_params=_cparams(("arbitrary",)),
        name="swa",
    )(sinks, qa, ka, ka, va, va, bias_tab)


def _t5_bias_table(rel_bias):
    i = np.arange(Q_BLOCK)[:, None]
    j = np.arange(2 * Q_BLOCK)[None, :]
    n = jnp.asarray(np.maximum(i + Q_BLOCK - j, 0))
    max_exact = N_BUCKETS // 2
    nf = jnp.maximum(n, 1).astype(jnp.float32)
    large = max_exact + (jnp.log(nf / max_exact) / math.log(MAX_DISTANCE / max_exact)
                         * (N_BUCKETS - max_exact)).astype(jnp.int32)
    large = jnp.minimum(large, N_BUCKETS - 1)
    bucket = jnp.where(n < max_exact, n, large)
    onehot = bucket[None] == jnp.arange(N_BUCKETS)[:, None, None]
    rb = rel_bias.astype(jnp.float32)
    return jnp.sum(jnp.where(onehot[:, None], rb[:, :, None, None], 0.0), axis=0)


def _fox_kernel(q_ref, k_ref, v_ref, o_ref):
    qi = pl.program_id(2)
    tq = q_ref.shape[1]
    qs = (q_ref[0], q_ref[1])

    def update(s, m, acc, v):
        m_new = jnp.maximum(m, jnp.max(s, axis=-1, keepdims=True))
        p = jnp.exp(s - m_new).astype(jnp.bfloat16)
        return m_new, jnp.exp(m - m_new) * acc + jnp.dot(p, v, preferred_element_type=jnp.float32)

    def scores(q, k):
        return lax.dot_general(q, k, (((1,), (1,)), ((), ())), preferred_element_type=jnp.float32)

    def step(j, carry):
        start = pl.multiple_of(j * tq, tq)
        return tuple(update(scores(qs[hh], k_ref[hh, pl.ds(start, tq), :]), *carry[hh],
                            v_ref[hh, pl.ds(start, tq), :]) for hh in range(2))

    def diagonal(carry):
        start = pl.multiple_of(qi * tq, tq)
        band = tq // FOX_DIAG_BANDS
        new = []
        for hh in range(2):
            m, acc = carry[hh]
            parts = []
            for r in range(FOX_DIAG_BANDS):
                lo, hi = r * band, (r + 1) * band
                s = scores(qs[hh][lo:hi], k_ref[hh, pl.ds(start, hi), :])
                row = lax.broadcasted_iota(jnp.int32, (band, hi), 0) + lo
                col = lax.broadcasted_iota(jnp.int32, (band, hi), 1)
                s = jnp.where(col <= row, s, -jnp.inf)
                parts.append(update(s, m[lo:hi], acc[lo:hi], v_ref[hh, pl.ds(start, hi), :]))
            new.append((jnp.concatenate([p[0] for p in parts], axis=0),
                        jnp.concatenate([p[1] for p in parts], axis=0)))
        return tuple(new)

    init1 = (jnp.full((tq, 1), -jnp.inf, jnp.float32), jnp.zeros((tq, LANES), jnp.float32))
    carry = diagonal(lax.fori_loop(0, qi, step, (init1, init1)))
    outs = [acc / acc[:, AUG:AUG + 1] for _, acc in carry]
    lane = lax.broadcasted_iota(jnp.int32, (tq, LANES), 1)
    o_ref[...] = jnp.where(lane < HEAD_DIM, outs[0], pltpu.roll(outs[1], HEAD_DIM, axis=1)).astype(o_ref.dtype)


def _fox(qf, kf, vf, batch, seq_len):
    t = qf.shape[1]
    tq = min(FOX_TILE, seq_len)
    assert seq_len % tq == 0 and tq % FOX_DIAG_BANDS == 0
    nq = seq_len // tq
    pairs = FOX_HEADS // 2
    seq = pl.BlockSpec((2, seq_len, LANES), lambda b, hp, i: (hp, b, 0))
    return pl.pallas_call(
        _fox_kernel,
        grid=(batch, pairs, nq),
        in_specs=[pl.BlockSpec((2, tq, LANES), lambda b, hp, i: (hp, b * nq + i, 0)), seq, seq],
        out_specs=pl.BlockSpec((tq, LANES), lambda b, hp, i: (b * nq + i, hp)),
        out_shape=jax.ShapeDtypeStruct((t, FOX_W), jnp.bfloat16),
        compiler_params=_cparams(("arbitrary", "arbitrary", "arbitrary")),
        name="fox",
    )(qf, kf, vf)


def _layer_norm(h, g, b):
    mu = jnp.mean(h, axis=-1, keepdims=True)
    d = h - mu
    var = jnp.mean(d * d, axis=-1, keepdims=True)
    return d * lax.rsqrt(var + LN_EPS) * g + b


def _pack_rows(ref, x):
    rows = x.shape[0]
    bits = pltpu.bitcast(x.astype(jnp.bfloat16).astype(jnp.float32), jnp.uint32)
    for c in range(ROW_WORDS):
        lo = bits[:, c * LANES:(c + 1) * LANES] >> 16
        hi = bits[:, HALF_D + c * LANES:HALF_D + (c + 1) * LANES] & jnp.uint32(HI_MASK)
        ref[pl.ds(c, rows, stride=ROW_WORDS), :] = lo | hi


def _unpack_rows(ref, first, rows):
    los, his = [], []
    for c in range(ROW_WORDS):
        w = ref[pl.ds(first * ROW_WORDS + c, rows, stride=ROW_WORDS), :]
        los.append(pltpu.bitcast(w << 16, jnp.float32))
        his.append(pltpu.bitcast(w & jnp.uint32(HI_MASK), jnp.float32))
    return jnp.concatenate(los + his, axis=1)


def _packed(ref, first, rows=1):
    return ref.at[pl.ds(pl.multiple_of(first * ROW_WORDS, ROW_WORDS), rows * ROW_WORDS), :]


def _packed_spec(rows, index_map):
    return pl.BlockSpec((rows * ROW_WORDS, LANES), index_map)


def _packed_shape(rows):
    return jax.ShapeDtypeStruct((rows * ROW_WORDS, LANES), jnp.uint32)


def _merge_kernel(alpha, x_ref, ya_ref, yb_ref, ga_ref, gb_ref, wa_ref, wb_ref, wo_ref, g_ref, b_ref,
                  x1_ref, x1p_ref):
    pa = jnp.dot(ya_ref[...], wa_ref[...], preferred_element_type=jnp.float32)
    pb = jnp.dot(yb_ref[...], wb_ref[...], preferred_element_type=jnp.float32)
    mix = _sigmoid(ga_ref[...]) * pa + _sigmoid(gb_ref[...]) * pb
    out = jnp.dot(mix.astype(jnp.bfloat16), wo_ref[...], preferred_element_type=jnp.float32)
    x1 = _layer_norm(alpha * x_ref[...] + out, g_ref[...], b_ref[...])
    x1_ref[...] = x1
    _pack_rows(x1p_ref, x1)


def _merge(alpha, x2, ya, yb, ga, gb, wa, wb, wo, g, b):
    t = x2.shape[0]
    tm = TM_MERGE
    row = lambda n: pl.BlockSpec((tm, n), lambda i: (i, 0))
    full = lambda a: pl.BlockSpec(a.shape, lambda i: (0,) * a.ndim)
    return pl.pallas_call(
        functools.partial(_merge_kernel, alpha),
        grid=(t // tm,),
        in_specs=[row(D_MODEL), row(SWA_Q), row(FOX_W), row(D_MODEL), row(D_MODEL),
                  full(wa), full(wb), full(wo), full(g), full(b)],
        out_specs=(row(D_MODEL), _packed_spec(tm, lambda i: (i, 0))),
        out_shape=(jax.ShapeDtypeStruct((t, D_MODEL), jnp.float32), _packed_shape(t)),
        compiler_params=_cparams(("arbitrary",)),
        name="merge",
    )(x2, ya, yb, ga, gb, wa, wb, wo, g, b)


def _first_argmax(v, idx):
    m = jnp.max(v, axis=0, keepdims=True)
    first = jnp.min(jnp.where(v == m, idx, v.shape[0]), axis=0, keepdims=True)
    return m, first


def _router_kernel(x_ref, wr_ref, rb_ref, tri_ref, e_ref, w_ref, r_ref, cnt_ref, carry_ref):
    i = pl.program_id(0)
    tm = x_ref.shape[0]

    @pl.when(i == 0)
    def _():
        carry_ref[...] = jnp.zeros_like(carry_ref)

    xb = x_ref[...].astype(jnp.bfloat16)
    logits = lax.dot_general(wr_ref[...], xb, (((1,), (1,)), ((), ())),
                             preferred_element_type=jnp.float32)
    scores = _sigmoid(logits)
    biased = scores + rb_ref[...]

    in_group = lax.broadcasted_iota(jnp.int32, (GROUP_SIZE, tm), 0)
    grows = []
    for g in range(N_GROUPS):
        blk = biased[g * GROUP_SIZE:(g + 1) * GROUP_SIZE, :]
        m1, f1 = _first_argmax(blk, in_group)
        m2 = jnp.max(jnp.where(in_group == f1, -jnp.inf, blk), axis=0, keepdims=True)
        grows.append(m1 + m2)
    gscore = jnp.concatenate(grows, axis=0)

    gidx = lax.broadcasted_iota(jnp.int32, (N_GROUPS, tm), 0)
    gwork = gscore
    gsel = jnp.zeros((N_GROUPS, tm), jnp.float32)
    for _ in range(TOPK_GROUPS):
        _, first = _first_argmax(gwork, gidx)
        hit = gidx == first
        gsel = jnp.where(hit, 1.0, gsel)
        gwork = jnp.where(hit, -jnp.inf, gwork)

    eidx = lax.broadcasted_iota(jnp.int32, (N_EXPERTS, tm), 0)
    work = jnp.concatenate(
        [jnp.where(jnp.broadcast_to(gsel[g:g + 1, :], (GROUP_SIZE, tm)) > 0.0,
                   biased[g * GROUP_SIZE:(g + 1) * GROUP_SIZE, :], -jnp.inf) for g in range(N_GROUPS)], axis=0)
    picks = []
    sel = jnp.zeros((N_EXPERTS, tm), jnp.bool_)
    for _ in range(TOP_K):
        _, first = _first_argmax(work, eidx)
        hit = eidx == first
        picks.append((first, hit))
        sel = sel | hit
        work = jnp.where(hit, -jnp.inf, work)

    wsum = jnp.sum(jnp.where(sel, scores, 0.0), axis=0, keepdims=True)
    sel_b = jnp.where(sel, 1.0, 0.0).astype(jnp.bfloat16)
    before = carry_ref[...] + jnp.dot(sel_b, tri_ref[...], preferred_element_type=jnp.float32)
    e_rows, w_rows, r_rows = [], [], []
    for first, hit in picks:
        e_rows.append(first)
        sc = jnp.sum(jnp.where(hit, scores, 0.0), axis=0, keepdims=True)
        w_rows.append(sc / wsum * ROUTED_SCALE)
        r_rows.append(jnp.sum(jnp.where(hit, before, 0.0), axis=0, keepdims=True).astype(jnp.int32))
    e_ref[...] = jnp.concatenate(e_rows, axis=0)
    w_ref[...] = jnp.concatenate(w_rows, axis=0)
    r_ref[...] = jnp.concatenate(r_rows, axis=0)
    total = carry_ref[...] + jnp.sum(sel_b.astype(jnp.float32), axis=1, keepdims=True)
    carry_ref[...] = total
    cnt_ref[...] = jnp.broadcast_to(total, cnt_ref.shape).astype(jnp.int32)


def _router(x1, wr_t, rb_col):
    t = x1.shape[0]
    tm = TM_ROUTE
    tri = jnp.triu(jnp.ones((tm, tm), jnp.bfloat16), k=1)
    full = lambda a: pl.BlockSpec(a.shape, lambda i: (0,) * a.ndim)
    kt = pl.BlockSpec((TOP_K, tm), lambda i: (0, i))
    return pl.pallas_call(
        _router_kernel,
        grid=(t // tm,),
        in_specs=[pl.BlockSpec((tm, D_MODEL), lambda i: (i, 0)), full(wr_t), full(rb_col), full(tri)],
        out_specs=(kt, kt, kt, pl.BlockSpec((N_EXPERTS, LANES), lambda i: (0, 0))),
        out_shape=(jax.ShapeDtypeStruct((TOP_K, t), jnp.int32), jax.ShapeDtypeStruct((TOP_K, t), jnp.float32),
                   jax.ShapeDtypeStruct((TOP_K, t), jnp.int32),
                   jax.ShapeDtypeStruct((N_EXPERTS, LANES), jnp.int32)),
        scratch_shapes=[pltpu.VMEM((N_EXPERTS, 1), jnp.float32)],
        compiler_params=_cparams(("arbitrary",)),
        name="router",
    )(x1, wr_t, rb_col, tri)


def _plan_kernel(ps_ref, e_ref, r_ref, d_ref):
    tm = e_ref.shape[1]
    eidx = lax.broadcasted_iota(jnp.int32, (N_EXPERTS, tm), 0)
    rows = []
    for k in range(TOP_K):
        hit = eidx == e_ref[k:k + 1, :]
        base = jnp.sum(jnp.where(hit, ps_ref[...], 0.0), axis=0, keepdims=True)
        rows.append(base.astype(jnp.int32) + r_ref[k:k + 1, :])
    d_ref[...] = jnp.concatenate(rows, axis=0)


def _plan(pstart_col, top_e, rank):
    t = top_e.shape[1]
    tm = min(TM_PLAN, t)
    kt = pl.BlockSpec((TOP_K, tm), lambda i: (0, i))
    return pl.pallas_call(
        _plan_kernel,
        grid=(t // tm,),
        in_specs=[pl.BlockSpec(pstart_col.shape, lambda i: (0, 0)), kt, kt],
        out_specs=kt,
        out_shape=jax.ShapeDtypeStruct((TOP_K, t), jnp.int32),
        compiler_params=_cparams(("arbitrary",)),
        name="plan",
    )(pstart_col, top_e, rank)


def _dispatch_kernel(pend_ref, padded_ref, nu_ref, dest_hbm, x_ref, xs_hbm, idx_smem, zero_vmem, sem, zsem):
    i = pl.program_id(0)
    n_blocks = xs_hbm.shape[0] // (EXPERT_BLOCK * ROW_WORDS)

    def zero_copy(first_row):
        return pltpu.make_async_copy(zero_vmem, _packed(xs_hbm, first_row, EXPERT_BLOCK), zsem)

    @pl.when(i == 0)
    def _():
        zero_vmem[...] = jnp.zeros_like(zero_vmem)

        def fill(e, c):
            @pl.when(padded_ref[e] > 0)
            def _():
                zero_copy(pend_ref[e] - EXPERT_BLOCK).start()
            return c
        lax.fori_loop(0, N_EXPERTS, fill, 0)

        def fill_tail(b, c):
            zero_copy(b * EXPERT_BLOCK).start()
            return c
        lax.fori_loop(nu_ref[0], n_blocks, fill_tail, 0)

        def drain(e, c):
            @pl.when(padded_ref[e] > 0)
            def _():
                zero_copy(0).wait()
            return c
        lax.fori_loop(0, N_EXPERTS, drain, 0)

        def drain_tail(b, c):
            zero_copy(0).wait()
            return c
        lax.fori_loop(nu_ref[0], n_blocks, drain_tail, 0)

    n_idx = TM_DISPATCH * TOP_K
    pltpu.sync_copy(dest_hbm.at[pl.ds(i * n_idx, n_idx)], idx_smem)

    def issue(t, c):
        for k in range(TOP_K):
            pltpu.make_async_copy(_packed(x_ref, t), _packed(xs_hbm, idx_smem[t * TOP_K + k]),
                                  sem).start(priority=k % 2)
        return c
    lax.fori_loop(0, TM_DISPATCH, issue, 0)

    for k in range(TOP_K):
        pltpu.make_async_copy(x_ref, _packed(xs_hbm, 0, TM_DISPATCH), sem).wait()


def _dispatch(dest_flat, x1p, pend, padded, n_used, n_rows):
    t = x1p.shape[0] // ROW_WORDS
    grid_spec = pltpu.PrefetchScalarGridSpec(
        num_scalar_prefetch=3,
        grid=(t // TM_DISPATCH,),
        in_specs=[pl.BlockSpec(memory_space=pl.ANY), _packed_spec(TM_DISPATCH, lambda i, pe, pa, nu: (i, 0))],
        out_specs=pl.BlockSpec(memory_space=pl.ANY),
        scratch_shapes=[pltpu.SMEM((TM_DISPATCH * TOP_K,), jnp.int32),
                        pltpu.VMEM((EXPERT_BLOCK * ROW_WORDS, LANES), jnp.uint32),
                        pltpu.SemaphoreType.DMA, pltpu.SemaphoreType.DMA])
    return pl.pallas_call(
        _dispatch_kernel,
        grid_spec=grid_spec,
        out_shape=_packed_shape(n_rows),
        compiler_params=_cparams(("arbitrary",)),
        name="dispatch",
    )(pend, padded, n_used, dest_flat, x1p)


def _experts_kernel(be_ref, nu_ref, xs_ref, wg_ref, wu_ref, wd_ref, y_ref, wg_b, wu_b, wd_b):
    nc = EXPERT_BLOCK // EXPERT_SUB
    i = pl.program_id(0)
    live = i < nu_ref[0]

    @pl.when(live & ((i == 0) | (be_ref[i] != be_ref[jnp.maximum(i - 1, 0)])))
    def _():
        wg_b[...] = wg_ref[...].astype(jnp.bfloat16)
        wu_b[...] = wu_ref[...].astype(jnp.bfloat16)
        wd_b[...] = wd_ref[...].astype(jnp.bfloat16)

    @pl.when(live)
    def _():
        xbs = [_unpack_rows(xs_ref, c * EXPERT_SUB, EXPERT_SUB).astype(jnp.bfloat16) for c in range(nc)]
        gus = [(jnp.dot(xb, wg_b[...], preferred_element_type=jnp.float32),
                jnp.dot(xb, wu_b[...], preferred_element_type=jnp.float32)) for xb in xbs]
        ys = []
        for g, u in gus:
            h = (g * _sigmoid(g) * u).astype(jnp.bfloat16)
            ys.append(jnp.dot(h, wd_b[...], preferred_element_type=jnp.float32))
        for c, y in enumerate(ys):
            _pack_rows(y_ref.at[pl.ds(c * EXPERT_SUB * ROW_WORDS, EXPERT_SUB * ROW_WORDS), :], y)

    @pl.when(jnp.logical_not(live))
    def _():
        y_ref[...] = jnp.zeros_like(y_ref)


def _experts(layer, block_e, n_used, xs, w_gate_e, w_up_e, w_down_e):
    n_blocks = xs.shape[0] // (EXPERT_BLOCK * ROW_WORDS)
    wsel = lambda i, be, nu: (layer, be[i], 0, 0)
    grid_spec = pltpu.PrefetchScalarGridSpec(
        num_scalar_prefetch=2,
        grid=(n_blocks,),
        in_specs=[_packed_spec(EXPERT_BLOCK, lambda i, be, nu: (jnp.minimum(i, nu[0] - 1), 0)),
                  pl.BlockSpec((None, None, D_MODEL, D_EXPERT), wsel),
                  pl.BlockSpec((None, None, D_MODEL, D_EXPERT), wsel),
                  pl.BlockSpec((None, None, D_EXPERT, D_MODEL), wsel)],
        out_specs=_packed_spec(EXPERT_BLOCK, lambda i, be, nu: (i, 0)),
        scratch_shapes=[pltpu.VMEM((D_MODEL, D_EXPERT), jnp.bfloat16), pltpu.VMEM((D_MODEL, D_EXPERT), jnp.bfloat16),
                        pltpu.VMEM((D_EXPERT, D_MODEL), jnp.bfloat16)])
    return pl.pallas_call(
        _experts_kernel,
        grid_spec=grid_spec,
        out_shape=_packed_shape(n_blocks * EXPERT_BLOCK),
        compiler_params=_cparams(("arbitrary",)),
        name="experts",
    )(block_e, n_used, xs, w_gate_e, w_up_e, w_down_e)


def _combine_kernel(alpha, dest_hbm, y_hbm, w_ref, x1_ref, wg_ref, wu_ref, wd_ref, g_ref, b_ref,
                    o_ref, idx_smem0, idx_smem1, buf, sem):
    i = pl.program_id(0)
    n = pl.num_programs(0)
    n_idx = TM_COMBINE * TOP_K

    def slab(slot, k):
        return (slot * TOP_K + k) * TM_COMBINE

    def start_gather(step, slot):
        idx_smem = (idx_smem0, idx_smem1)[slot]
        pltpu.sync_copy(dest_hbm.at[pl.ds(step * n_idx, n_idx)], idx_smem)

        def issue(t, c):
            for k in range(TOP_K):
                pltpu.make_async_copy(_packed(y_hbm, idx_smem[t * TOP_K + k]),
                                      _packed(buf, slab(slot, k) + t), sem.at[slot]).start(priority=k % 2)
            return c
        lax.fori_loop(0, TM_COMBINE, issue, 0)

    @pl.when(i == 0)
    def _():
        start_gather(0, 0)

    def run(slot):
        @pl.when(i + 1 < n)
        def _():
            start_gather(i + 1, 1 - slot)

        for k in range(TOP_K):
            pltpu.make_async_copy(_packed(y_hbm, 0, TM_COMBINE), _packed(buf, slab(slot, k), TM_COMBINE),
                                  sem.at[slot]).wait()

        w = w_ref[...]
        routed = jnp.zeros((TM_COMBINE, D_MODEL), jnp.float32)
        for k in range(TOP_K):
            routed = routed + w[:, k:k + 1] * _unpack_rows(buf, slab(slot, k), TM_COMBINE)

        x1 = x1_ref[...]
        xb = x1.astype(jnp.bfloat16)
        gs = jnp.dot(xb, wg_ref[...], preferred_element_type=jnp.float32)
        us = jnp.dot(xb, wu_ref[...], preferred_element_type=jnp.float32)
        hs = gs * _sigmoid(gs) * us
        shared = jnp.dot(hs.astype(jnp.bfloat16), wd_ref[...], preferred_element_type=jnp.float32)
        o_ref[...] = _layer_norm(alpha * x1 + (shared + routed), g_ref[...], b_ref[...])

    for slot in range(2):
        pl.when(i % 2 == slot)(functools.partial(run, slot))


def _combine(alpha, dest_flat, y, w_tk, x1, wg, wu, wd, g, b):
    t = x1.shape[0]
    tm = TM_COMBINE
    full = lambda a: pl.BlockSpec(a.shape, lambda i: (0,) * a.ndim)
    return pl.pallas_call(
        functools.partial(_combine_kernel, alpha),
        grid=(t // tm,),
        in_specs=[pl.BlockSpec(memory_space=pl.ANY), pl.BlockSpec(memory_space=pl.ANY),
                  pl.BlockSpec((tm, TOP_K), lambda i: (i, 0)), pl.BlockSpec((tm, D_MODEL), lambda i: (i, 0)),
                  full(wg), full(wu), full(wd), full(g), full(b)],
        out_specs=pl.BlockSpec((tm, D_MODEL), lambda i: (i, 0)),
        out_shape=jax.ShapeDtypeStruct((t, D_MODEL), jnp.float32),
        scratch_shapes=[pltpu.SMEM((tm * TOP_K,), jnp.int32), pltpu.SMEM((tm * TOP_K,), jnp.int32),
                        pltpu.VMEM((2 * TOP_K * tm * ROW_WORDS, LANES), jnp.uint32),
                        pltpu.SemaphoreType.DMA((2,))],
        compiler_params=_cparams(("arbitrary",)),
        name="combine",
    )(dest_flat, y, w_tk, x1, wg, wu, wd, g, b)


def _prep_in_weights(w, b_forget):
    o = np.cumsum((0,) + IN_SIZES)
    group = SWA_Q_HEADS // SWA_KV_HEADS
    qa = w[:, o[0]:o[1]].reshape(D_MODEL, SWA_KV_HEADS, group, HEAD_DIM)
    qa = qa.transpose(0, 2, 1, 3).reshape(D_MODEL, SWA_Q)
    zf = jnp.pad(w[:, o[6]:o[7]], ((0, 0), (0, LANES - FOX_HEADS)))
    w_main = jnp.concatenate([qa, w[:, o[1]:o[6]], w[:, o[7]:o[9]], zf], axis=1).astype(jnp.bfloat16)
    bf_row = jnp.pad(b_forget, (0, LANES - FOX_HEADS)).reshape(1, LANES).astype(jnp.float32)
    return w_main, bf_row


def _prep_proj_a(w):
    group = SWA_Q_HEADS // SWA_KV_HEADS
    w4 = w.reshape(SWA_KV_HEADS, group, HEAD_DIM, D_MODEL).transpose(1, 0, 2, 3)
    return w4.reshape(SWA_Q, D_MODEL).astype(jnp.bfloat16)


def _route_plan(counts, n_tokens):
    n_assign = n_tokens * TOP_K
    n_rows = (n_assign + N_EXPERTS * (EXPERT_BLOCK - 1) + EXPERT_BLOCK - 1) // EXPERT_BLOCK * EXPERT_BLOCK
    n_blocks = n_rows // EXPERT_BLOCK
    padded = (counts + EXPERT_BLOCK - 1) // EXPERT_BLOCK * EXPERT_BLOCK
    pend = jnp.cumsum(padded)
    pstart = pend - padded
    n_used = pend[-1] // EXPERT_BLOCK
    blk = jnp.minimum(jnp.arange(n_blocks, dtype=jnp.int32), n_used - 1) * EXPERT_BLOCK
    block_e = jnp.minimum(jnp.sum((pend[None, :] <= blk[:, None]).astype(jnp.int32), axis=1), N_EXPERTS - 1)
    return (pstart.astype(jnp.float32).reshape(N_EXPERTS, 1), block_e.astype(jnp.int32),
            n_used.reshape(1).astype(jnp.int32), pend.astype(jnp.int32), padded.astype(jnp.int32), n_rows)


def kernel(x, w_in, b_forget, attn_sinks, rel_bias, w_proj_a, w_proj_b, w_out, ln1_g, ln1_b,
           w_router, router_bias, w_gate_e, w_up_e, w_down_e, w_gate_s, w_up_s, w_down_s, ln2_g, ln2_b):
    batch, seq_len, d = x.shape
    depth = w_in.shape[0]
    alpha = (2 * depth) ** 0.25
    t = batch * seq_len
    bf16 = jnp.bfloat16
    x2 = x.reshape(t, d)
    bias_tab = _t5_bias_table(rel_bias)
    for l in range(depth):
        w_main, bf_row = _prep_in_weights(w_in[l], b_forget[l])
        qa, ka, va, qf, kf, vf, ga, gb = _in_proj(x2, w_main, bf_row, seq_len)
        ya = _swa(qa, ka, va, attn_sinks[l].astype(jnp.float32), bias_tab, seq_len)
        yb = _fox(qf, kf, vf, batch, seq_len)
        x1, x1p = _merge(alpha, x2, ya, yb, ga, gb, _prep_proj_a(w_proj_a[l]), w_proj_b[l].astype(bf16),
                         w_out[l].astype(bf16), ln1_g[l].reshape(1, d), ln1_b[l].reshape(1, d))
        top_e, w_kt, rank, counts = _router(x1, w_router[l].T.astype(bf16),
                                            router_bias[l].reshape(N_EXPERTS, 1).astype(jnp.float32))
        pstart_col, block_e, n_used, pend, padded, n_rows = _route_plan(counts[:, 0], t)
        dest_flat = _plan(pstart_col, top_e, rank).T.reshape(-1)
        xs = _dispatch(dest_flat, x1p, pend, padded, n_used, n_rows)
        y = _experts(l, block_e, n_used, xs, w_gate_e, w_up_e, w_down_e)
        x2 = _combine(alpha, dest_flat, y, w_kt.T, x1, w_gate_s[l].astype(bf16), w_up_s[l].astype(bf16),
                      w_down_s[l].astype(bf16), ln2_g[l].reshape(1, d), ln2_b[l].reshape(1, d))
    return x2.reshape(batch, seq_len, d)
```

```python
import functools
import math

import jax
import jax.numpy as jnp
import numpy as np
from jax import lax
from jax.experimental import pallas as pl
from jax.experimental.pallas import tpu as pltpu

D_MODEL = 1024
HEAD_DIM = 64
SWA_Q_HEADS = 8
SWA_KV_HEADS = 2
SWA_WINDOW = 128
FOX_HEADS = 8
Q_BLOCK = 128
N_BUCKETS = 32
MAX_DISTANCE = 128
N_EXPERTS = 256
TOP_K = 8
N_GROUPS = 8
TOPK_GROUPS = 4
GROUP_SIZE = N_EXPERTS // N_GROUPS
D_EXPERT = 256
D_SHARED = 256
ROUTED_SCALE = 2.5
LN_EPS = 1e-5

SWA_Q = SWA_Q_HEADS * HEAD_DIM
SWA_KV = SWA_KV_HEADS * HEAD_DIM
FOX_W = FOX_HEADS * HEAD_DIM
IN_SIZES = (SWA_Q, SWA_KV, SWA_KV, FOX_W, FOX_W, FOX_W, FOX_HEADS, D_MODEL, D_MODEL)

LANES = 128
HALF_D = D_MODEL // 2
ROW_WORDS = HALF_D // LANES
VMEM_LIMIT = 56 * 1024 * 1024
HI_MASK = 0xFFFF0000

TM_PROJ = 512
TM_MERGE = 256
TM_ROUTE = 256
TM_DISPATCH = 1024
TM_COMBINE = 512
TM_PLAN = 2048
EXPERT_BLOCK = 512
EXPERT_SUB = 256
SWA_STEP = 4
FOX_TILE = 1024
FOX_DIAG_BANDS = 2

AUG = HEAD_DIM


def _cparams(sem):
    return pltpu.CompilerParams(dimension_semantics=sem, vmem_limit_bytes=VMEM_LIMIT)


def _sigmoid(x):
    return 1.0 / (1.0 + jnp.exp(-x))


def _split3(c):
    hi = c.astype(jnp.bfloat16).astype(jnp.float32)
    r = c - hi
    mid = r.astype(jnp.bfloat16).astype(jnp.float32)
    lo = (r - mid).astype(jnp.bfloat16).astype(jnp.float32)
    return hi, mid, lo


_C_QA = 0
_C_KA = _C_QA + SWA_Q
_C_VA = _C_KA + SWA_KV
_C_QF = _C_VA + SWA_KV
_C_KF = _C_QF + FOX_W
_C_VF = _C_KF + FOX_W
_C_GA = _C_VF + FOX_W
_C_GB = _C_GA + D_MODEL
_C_ZF = _C_GB + D_MODEL
_N_PROJ = _C_ZF + LANES


def _in_proj_kernel(tiles_per_seq, x_ref, w_ref, bf_ref, tri_ref,
                    qa_ref, ka_ref, va_ref, qf_ref, kf_ref, vf_ref, ga_ref, gb_ref, carry_ref):
    i = pl.program_id(0)
    tm = x_ref.shape[0]
    xb = x_ref[...].astype(jnp.bfloat16)

    def proj(c0, n):
        return jnp.dot(xb, w_ref[:, c0:c0 + n], preferred_element_type=jnp.float32)

    scale = HEAD_DIM ** -0.5
    qa_ref[...] = (proj(_C_QA, SWA_Q) * scale).astype(jnp.bfloat16)
    ka_ref[...] = proj(_C_KA, SWA_KV).astype(jnp.bfloat16)
    va_ref[...] = proj(_C_VA, SWA_KV).astype(jnp.bfloat16)
    ga_ref[...] = proj(_C_GA, D_MODEL)
    gb_ref[...] = proj(_C_GB, D_MODEL)

    z = proj(_C_ZF, LANES) + bf_ref[...]
    log_f = jnp.minimum(z, 0.0) - jnp.log(1.0 + jnp.exp(-jnp.abs(z)))

    @pl.when(i % tiles_per_seq == 0)
    def _():
        carry_ref[...] = jnp.zeros_like(carry_ref)

    tri = tri_ref[...]
    h3 = _split3(log_f)
    c = carry_ref[...]
    for part in h3:
        c = c + jnp.dot(tri, part.astype(jnp.bfloat16), preferred_element_type=jnp.float32)
    carry_ref[...] = c[tm - 1:tm, :]

    qf = proj(_C_QF, FOX_W) * scale
    kf = proj(_C_KF, FOX_W)
    vf = proj(_C_VF, FOX_W)
    lane = lax.broadcasted_iota(jnp.int32, (tm, LANES), 1)
    one = jnp.float32(1.0)
    zero = jnp.float32(0.0)
    for h in range(FOX_HEADS):
        pair = (h // 2) * LANES
        qp = qf[:, pair:pair + LANES]
        kp = kf[:, pair:pair + LANES]
        vp = vf[:, pair:pair + LANES]
        if h % 2 == 1:
            qp = pltpu.roll(qp, HEAD_DIM, axis=1)
            kp = pltpu.roll(kp, HEAD_DIM, axis=1)
            vp = pltpu.roll(vp, HEAD_DIM, axis=1)
        ch = jnp.broadcast_to(c[:, h:h + 1], (tm, LANES))
        hi, mid, lo = _split3(ch)
        q_aug = jnp.where(lane == AUG, hi, jnp.where(lane == AUG + 1, mid, jnp.where(
            lane == AUG + 2, lo, jnp.where(lane < AUG + 6, one, zero))))
        k_aug = jnp.where(lane < AUG + 3, one, jnp.where(lane == AUG + 3, -hi, jnp.where(
            lane == AUG + 4, -mid, jnp.where(lane == AUG + 5, -lo, zero))))
        qf_ref[h] = jnp.where(lane < AUG, qp, q_aug).astype(jnp.bfloat16)
        kf_ref[h] = jnp.where(lane < AUG, kp, k_aug).astype(jnp.bfloat16)
        vf_ref[h] = jnp.where(lane < AUG, vp, jnp.where(lane == AUG, one, zero)).astype(jnp.bfloat16)


def _in_proj(x2, w_main, bf_row, seq_len):
    t = x2.shape[0]
    tm = min(TM_PROJ, seq_len)
    tri = jnp.tril(jnp.ones((tm, tm), jnp.bfloat16))
    row = lambda n: pl.BlockSpec((tm, n), lambda i: (i, 0))
    full = lambda a: pl.BlockSpec(a.shape, lambda i: (0,) * a.ndim)
    slab = pl.BlockSpec((FOX_HEADS, tm, LANES), lambda i: (0, i, 0))
    bf16, f32 = jnp.bfloat16, jnp.float32
    slab_shape = jax.ShapeDtypeStruct((FOX_HEADS, t, LANES), bf16)
    out_shape = (
        jax.ShapeDtypeStruct((t, SWA_Q), bf16), jax.ShapeDtypeStruct((t, SWA_KV), bf16),
        jax.ShapeDtypeStruct((t, SWA_KV), bf16), slab_shape, slab_shape, slab_shape,
        jax.ShapeDtypeStruct((t, D_MODEL), f32), jax.ShapeDtypeStruct((t, D_MODEL), f32))
    return pl.pallas_call(
        functools.partial(_in_proj_kernel, seq_len // tm),
        grid=(t // tm,),
        in_specs=[row(D_MODEL), full(w_main), full(bf_row), full(tri)],
        out_specs=(row(SWA_Q), row(SWA_KV), row(SWA_KV), slab, slab, slab, row(D_MODEL), row(D_MODEL)),
        out_shape=out_shape,
        scratch_shapes=[pltpu.VMEM((1, LANES), f32)],
        compiler_params=_cparams(("arbitrary",)),
        name="in_proj",
    )(x2, w_main, bf_row, tri)


def _swa_kernel(steps_per_seq, sink_ref, q_ref, kp_ref, kc_ref, vp_ref, vc_ref, bias_ref, o_ref):
    n = pl.program_id(0)
    kall = jnp.concatenate([kp_ref[...], kc_ref[...]], axis=0)
    vall = jnp.concatenate([vp_ref[...], vc_ref[...]], axis=0)
    qi = lax.broadcasted_iota(jnp.int32, (Q_BLOCK, 2 * Q_BLOCK), 0)
    kj = lax.broadcasted_iota(jnp.int32, (Q_BLOCK, 2 * Q_BLOCK), 1)
    dist = qi + Q_BLOCK - kj
    in_window = (dist >= 0) & (dist < SWA_WINDOW)
    first_key = jnp.where((n % steps_per_seq) > 0, 0, Q_BLOCK)
    lane = lax.broadcasted_iota(jnp.int32, (Q_BLOCK, LANES), 1)
    lo_half = lane < HEAD_DIM
    group = SWA_Q_HEADS // SWA_KV_HEADS
    for b in range(SWA_STEP):
        rows = slice(b * Q_BLOCK, (b + 1) * Q_BLOCK)
        kk = kall[b * Q_BLOCK:(b + 2) * Q_BLOCK]
        vv = vall[b * Q_BLOCK:(b + 2) * Q_BLOCK]
        valid = (in_window & (kj >= first_key)) if b == 0 else in_window
        for j in range(group):
            qs = q_ref[rows, j * LANES:(j + 1) * LANES]
            outs = []
            for kv in range(SWA_KV_HEADS):
                head = kv * group + j
                qm = jnp.where(lo_half if kv == 0 else ~lo_half, qs, jnp.zeros_like(qs))
                s = lax.dot_general(qm, kk, (((1,), (1,)), ((), ())), preferred_element_type=jnp.float32)
                s = jnp.where(valid, s + bias_ref[head], -jnp.inf)
                sink = sink_ref[head]
                m = jnp.maximum(jnp.max(s, axis=-1, keepdims=True), sink)
                p = jnp.exp(s - m)
                denom = jnp.sum(p, axis=-1, keepdims=True) + jnp.exp(sink - m)
                pn = (p / denom).astype(jnp.bfloat16)
                outs.append(jnp.dot(pn, vv, preferred_element_type=jnp.float32))
            o_ref[rows, j * LANES:(j + 1) * LANES] = jnp.where(lo_half, outs[0], outs[1]).astype(o_ref.dtype)


def _swa(qa, ka, va, sinks, bias_tab, seq_len):
    t = qa.shape[0]
    tm = SWA_STEP * Q_BLOCK
    assert seq_len % tm == 0
    cur = lambda n, s: (n, 0)
    prev = lambda n, s: (jnp.maximum(n * SWA_STEP - 1, 0), 0)
    grid_spec = pltpu.PrefetchScalarGridSpec(
        num_scalar_prefetch=1,
        grid=(t // tm,),
        in_specs=[pl.BlockSpec((tm, SWA_Q), cur),
                  pl.BlockSpec((Q_BLOCK, SWA_KV), prev), pl.BlockSpec((tm, SWA_KV), cur),
                  pl.BlockSpec((Q_BLOCK, SWA_KV), prev), pl.BlockSpec((tm, SWA_KV), cur),
                  pl.BlockSpec(bias_tab.shape, lambda n, s: (0, 0, 0))],
        out_specs=pl.BlockSpec((tm, SWA_Q), cur))
    return pl.pallas_call(
        functools.partial(_swa_kernel, seq_len // tm),
        grid_spec=grid_spec,
        out_shape=jax.ShapeDtypeStruct((t, SWA_Q), jnp.bfloat16),
        compiler_params=_cparams(("arbitrary",)),
        name="swa",
    )(sinks, qa, ka, ka, va, va, bias_tab)


def _t5_bias_table(rel_bias):
    i = np.arange(Q_BLOCK)[:, None]
    j = np.arange(2 * Q_BLOCK)[None, :]
    n = jnp.asarray(np.maximum(i + Q_BLOCK - j, 0))
    max_exact = N_BUCKETS // 2
    nf = jnp.maximum(n, 1).astype(jnp.float32)
    large = max_exact + (jnp.log(nf / max_exact) / math.log(MAX_DISTANCE / max_exact)
                         * (N_BUCKETS - max_exact)).astype(jnp.int32)
    large = jnp.minimum(large, N_BUCKETS - 1)
    bucket = jnp.where(n < max_exact, n, large)
    onehot = bucket[None] == jnp.arange(N_BUCKETS)[:, None, None]
    rb = rel_bias.astype(jnp.float32)
    return jnp.sum(jnp.where(onehot[:, None], rb[:, :, None, None], 0.0), axis=0)


def _fox_kernel(q_ref, k_ref, v_ref, o_ref):
    qi = pl.program_id(2)
    tq = q_ref.shape[1]
    qs = (q_ref[0], q_ref[1])

    def update(s, m, acc, v):
        m_new = jnp.maximum(m, jnp.max(s, axis=-1, keepdims=True))
        p = jnp.exp(s - m_new).astype(jnp.bfloat16)
        return m_new, jnp.exp(m - m_new) * acc + jnp.dot(p, v, preferred_element_type=jnp.float32)

    def scores(q, k):
        return lax.dot_general(q, k, (((1,), (1,)), ((), ())), preferred_element_type=jnp.float32)

    def step(j, carry):
        start = pl.multiple_of(j * tq, tq)
        return tuple(update(scores(qs[hh], k_ref[hh, pl.ds(start, tq), :]), *carry[hh],
                            v_ref[hh, pl.ds(start, tq), :]) for hh in range(2))

    def diagonal(carry):
        start = pl.multiple_of(qi * tq, tq)
        band = tq // FOX_DIAG_BANDS
        new = []
        for hh in range(2):
            m, acc = carry[hh]
            parts = []
            for r in range(FOX_DIAG_BANDS):
                lo, hi = r * band, (r + 1) * band
                s = scores(qs[hh][lo:hi], k_ref[hh, pl.ds(start, hi), :])
                row = lax.broadcasted_iota(jnp.int32, (band, hi), 0) + lo
                col = lax.broadcasted_iota(jnp.int32, (band, hi), 1)
                s = jnp.where(col <= row, s, -jnp.inf)
                parts.append(update(s, m[lo:hi], acc[lo:hi], v_ref[hh, pl.ds(start, hi), :]))
            new.append((jnp.concatenate([p[0] for p in parts], axis=0),
                        jnp.concatenate([p[1] for p in parts], axis=0)))
        return tuple(new)

    init1 = (jnp.full((tq, 1), -jnp.inf, jnp.float32), jnp.zeros((tq, LANES), jnp.float32))
    carry = diagonal(lax.fori_loop(0, qi, step, (init1, init1)))
    outs = [acc / acc[:, AUG:AUG + 1] for _, acc in carry]
    lane = lax.broadcasted_iota(jnp.int32, (tq, LANES), 1)
    o_ref[...] = jnp.where(lane < HEAD_DIM, outs[0], pltpu.roll(outs[1], HEAD_DIM, axis=1)).astype(o_ref.dtype)


def _fox(qf, kf, vf, batch, seq_len):
    t = qf.shape[1]
    tq = min(FOX_TILE, seq_len)
    assert seq_len % tq == 0 and tq % FOX_DIAG_BANDS == 0
    nq = seq_len // tq
    pairs = FOX_HEADS // 2
    seq = pl.BlockSpec((2, seq_len, LANES), lambda b, hp, i: (hp, b, 0))
    return pl.pallas_call(
        _fox_kernel,
        grid=(batch, pairs, nq),
        in_specs=[pl.BlockSpec((2, tq, LANES), lambda b, hp, i: (hp, b * nq + i, 0)), seq, seq],
        out_specs=pl.BlockSpec((tq, LANES), lambda b, hp, i: (b * nq + i, hp)),
        out_shape=jax.ShapeDtypeStruct((t, FOX_W), jnp.bfloat16),
        compiler_params=_cparams(("arbitrary", "arbitrary", "arbitrary")),
        name="fox",
    )(qf, kf, vf)


def _layer_norm(h, g, b):
    mu = jnp.mean(h, axis=-1, keepdims=True)
    d = h - mu
    var = jnp.mean(d * d, axis=-1, keepdims=True)
    return d * lax.rsqrt(var + LN_EPS) * g + b


def _pack_rows(ref, x):
    rows = x.shape[0]
    bits = pltpu.bitcast(x.astype(jnp.bfloat16).astype(jnp.float32), jnp.uint32)
    for c in range(ROW_WORDS):
        lo = bits[:, c * LANES:(c + 1) * LANES] >> 16
        hi = bits[:, HALF_D + c * LANES:HALF_D + (c + 1) * LANES] & jnp.uint32(HI_MASK)
        ref[pl.ds(c, rows, stride=ROW_WORDS), :] = lo | hi


def _unpack_rows(ref, first, rows):
    los, his = [], []
    for c in range(ROW_WORDS):
        w = ref[pl.ds(first * ROW_WORDS + c, rows, stride=ROW_WORDS), :]
        los.append(pltpu.bitcast(w << 16, jnp.float32))
        his.append(pltpu.bitcast(w & jnp.uint32(HI_MASK), jnp.float32))
    return jnp.concatenate(los + his, axis=1)


def _packed(ref, first, rows=1):
    return ref.at[pl.ds(pl.multiple_of(first * ROW_WORDS, ROW_WORDS), rows * ROW_WORDS), :]


def _packed_spec(rows, index_map):
    return pl.BlockSpec((rows * ROW_WORDS, LANES), index_map)


def _packed_shape(rows):
    return jax.ShapeDtypeStruct((rows * ROW_WORDS, LANES), jnp.uint32)


def _merge_kernel(alpha, x_ref, ya_ref, yb_ref, ga_ref, gb_ref, wa_ref, wb_ref, wo_ref, g_ref, b_ref,
                  x1_ref, x1p_ref):
    pa = jnp.dot(ya_ref[...], wa_ref[...], preferred_element_type=jnp.float32)
    pb = jnp.dot(yb_ref[...], wb_ref[...], preferred_element_type=jnp.float32)
    mix = _sigmoid(ga_ref[...]) * pa + _sigmoid(gb_ref[...]) * pb
    out = jnp.dot(mix.astype(jnp.bfloat16), wo_ref[...], preferred_element_type=jnp.float32)
    x1 = _layer_norm(alpha * x_ref[...] + out, g_ref[...], b_ref[...])
    x1_ref[...] = x1
    _pack_rows(x1p_ref, x1)


def _merge(alpha, x2, ya, yb, ga, gb, wa, wb, wo, g, b):
    t = x2.shape[0]
    tm = TM_MERGE
    row = lambda n: pl.BlockSpec((tm, n), lambda i: (i, 0))
    full = lambda a: pl.BlockSpec(a.shape, lambda i: (0,) * a.ndim)
    return pl.pallas_call(
        functools.partial(_merge_kernel, alpha),
        grid=(t // tm,),
        in_specs=[row(D_MODEL), row(SWA_Q), row(FOX_W), row(D_MODEL), row(D_MODEL),
                  full(wa), full(wb), full(wo), full(g), full(b)],
        out_specs=(row(D_MODEL), _packed_spec(tm, lambda i: (i, 0))),
        out_shape=(jax.ShapeDtypeStruct((t, D_MODEL), jnp.float32), _packed_shape(t)),
        compiler_params=_cparams(("arbitrary",)),
        name="merge",
    )(x2, ya, yb, ga, gb, wa, wb, wo, g, b)


def _first_argmax(v, idx):
    m = jnp.max(v, axis=0, keepdims=True)
    first = jnp.min(jnp.where(v == m, idx, v.shape[0]), axis=0, keepdims=True)
    return m, first


def _router_kernel(x_ref, wr_ref, rb_ref, tri_ref, e_ref, w_ref, r_ref, cnt_ref, carry_ref):
    i = pl.program_id(0)
    tm = x_ref.shape[0]

    @pl.when(i == 0)
    def _():
        carry_ref[...] = jnp.zeros_like(carry_ref)

    xb = x_ref[...].astype(jnp.bfloat16)
    logits = lax.dot_general(wr_ref[...], xb, (((1,), (1,)), ((), ())),
                             preferred_element_type=jnp.float32)
    scores = _sigmoid(logits)
    biased = scores + rb_ref[...]

    in_group = lax.broadcasted_iota(jnp.int32, (GROUP_SIZE, tm), 0)
    grows = []
    for g in range(N_GROUPS):
        blk = biased[g * GROUP_SIZE:(g + 1) * GROUP_SIZE, :]
        m1, f1 = _first_argmax(blk, in_group)
        m2 = jnp.max(jnp.where(in_group == f1, -jnp.inf, blk), axis=0, keepdims=True)
        grows.append(m1 + m2)
    gscore = jnp.concatenate(grows, axis=0)

    gidx = lax.broadcasted_iota(jnp.int32, (N_GROUPS, tm), 0)
    gwork = gscore
    gsel = jnp.zeros((N_GROUPS, tm), jnp.float32)
    for _ in range(TOPK_GROUPS):
        _, first = _first_argmax(gwork, gidx)
        hit = gidx == first
        gsel = jnp.where(hit, 1.0, gsel)
        gwork = jnp.where(hit, -jnp.inf, gwork)

    eidx = lax.broadcasted_iota(jnp.int32, (N_EXPERTS, tm), 0)
    work = jnp.concatenate(
        [jnp.where(jnp.broadcast_to(gsel[g:g + 1, :], (GROUP_SIZE, tm)) > 0.0,
                   biased[g * GROUP_SIZE:(g + 1) * GROUP_SIZE, :], -jnp.inf) for g in range(N_GROUPS)], axis=0)
    picks = []
    sel = jnp.zeros((N_EXPERTS, tm), jnp.bool_)
    for _ in range(TOP_K):
        _, first = _first_argmax(work, eidx)
        hit = eidx == first
        picks.append((first, hit))
        sel = sel | hit
        work = jnp.where(hit, -jnp.inf, work)

    wsum = jnp.sum(jnp.where(sel, scores, 0.0), axis=0, keepdims=True)
    sel_b = jnp.where(sel, 1.0, 0.0).astype(jnp.bfloat16)
    before = carry_ref[...] + jnp.dot(sel_b, tri_ref[...], preferred_element_type=jnp.float32)
    e_rows, w_rows, r_rows = [], [], []
    for first, hit in picks:
        e_rows.append(first)
        sc = jnp.sum(jnp.where(hit, scores, 0.0), axis=0, keepdims=True)
        w_rows.append(sc / wsum * ROUTED_SCALE)
        r_rows.append(jnp.sum(jnp.where(hit, before, 0.0), axis=0, keepdims=True).astype(jnp.int32))
    e_ref[...] = jnp.concatenate(e_rows, axis=0)
    w_ref[...] = jnp.concatenate(w_rows, axis=0)
    r_ref[...] = jnp.concatenate(r_rows, axis=0)
    total = carry_ref[...] + jnp.sum(sel_b.astype(jnp.float32), axis=1, keepdims=True)
    carry_ref[...] = total
    cnt_ref[...] = jnp.broadcast_to(total, cnt_ref.shape).astype(jnp.int32)


def _router(x1, wr_t, rb_col):
    t = x1.shape[0]
    tm = TM_ROUTE
    tri = jnp.triu(jnp.ones((tm, tm), jnp.bfloat16), k=1)
    full = lambda a: pl.BlockSpec(a.shape, lambda i: (0,) * a.ndim)
    kt = pl.BlockSpec((TOP_K, tm), lambda i: (0, i))
    return pl.pallas_call(
        _router_kernel,
        grid=(t // tm,),
        in_specs=[pl.BlockSpec((tm, D_MODEL), lambda i: (i, 0)), full(wr_t), full(rb_col), full(tri)],
        out_specs=(kt, kt, kt, pl.BlockSpec((N_EXPERTS, LANES), lambda i: (0, 0))),
        out_shape=(jax.ShapeDtypeStruct((TOP_K, t), jnp.int32), jax.ShapeDtypeStruct((TOP_K, t), jnp.float32),
                   jax.ShapeDtypeStruct((TOP_K, t), jnp.int32),
                   jax.ShapeDtypeStruct((N_EXPERTS, LANES), jnp.int32)),
        scratch_shapes=[pltpu.VMEM((N_EXPERTS, 1), jnp.float32)],
        compiler_params=_cparams(("arbitrary",)),
        name="router",
    )(x1, wr_t, rb_col, tri)


def _plan_kernel(ps_ref, e_ref, r_ref, d_ref):
    tm = e_ref.shape[1]
    eidx = lax.broadcasted_iota(jnp.int32, (N_EXPERTS, tm), 0)
    rows = []
    for k in range(TOP_K):
        hit = eidx == e_ref[k:k + 1, :]
        base = jnp.sum(jnp.where(hit, ps_ref[...], 0.0), axis=0, keepdims=True)
        rows.append(base.astype(jnp.int32) + r_ref[k:k + 1, :])
    d_ref[...] = jnp.concatenate(rows, axis=0)


def _plan(pstart_col, top_e, rank):
    t = top_e.shape[1]
    tm = min(TM_PLAN, t)
    kt = pl.BlockSpec((TOP_K, tm), lambda i: (0, i))
    return pl.pallas_call(
        _plan_kernel,
        grid=(t // tm,),
        in_specs=[pl.BlockSpec(pstart_col.shape, lambda i: (0, 0)), kt, kt],
        out_specs=kt,
        out_shape=jax.ShapeDtypeStruct((TOP_K, t), jnp.int32),
        compiler_params=_cparams(("arbitrary",)),
        name="plan",
    )(pstart_col, top_e, rank)


def _dispatch_kernel(pend_ref, padded_ref, nu_ref, dest_hbm, x_ref, xs_hbm, idx_smem, zero_vmem, sem, zsem):
    i = pl.program_id(0)
    n_blocks = xs_hbm.shape[0] // (EXPERT_BLOCK * ROW_WORDS)

    def zero_copy(first_row):
        return pltpu.make_async_copy(zero_vmem, _packed(xs_hbm, first_row, EXPERT_BLOCK), zsem)

    @pl.when(i == 0)
    def _():
        zero_vmem[...] = jnp.zeros_like(zero_vmem)

        def fill(e, c):
            @pl.when(padded_ref[e] > 0)
            def _():
                zero_copy(pend_ref[e] - EXPERT_BLOCK).start()
            return c
        lax.fori_loop(0, N_EXPERTS, fill, 0)

        def fill_tail(b, c):
            zero_copy(b * EXPERT_BLOCK).start()
            return c
        lax.fori_loop(nu_ref[0], n_blocks, fill_tail, 0)

        def drain(e, c):
            @pl.when(padded_ref[e] > 0)
            def _():
                zero_copy(0).wait()
            return c
        lax.fori_loop(0, N_EXPERTS, drain, 0)

        def drain_tail(b, c):
            zero_copy(0).wait()
            return c
        lax.fori_loop(nu_ref[0], n_blocks, drain_tail, 0)

    n_idx = TM_DISPATCH * TOP_K
    pltpu.sync_copy(dest_hbm.at[pl.ds(i * n_idx, n_idx)], idx_smem)

    def issue(t, c):
        for k in range(TOP_K):
            pltpu.make_async_copy(_packed(x_ref, t), _packed(xs_hbm, idx_smem[t * TOP_K + k]),
                                  sem).start(priority=k % 2)
        return c
    lax.fori_loop(0, TM_DISPATCH, issue, 0)

    for k in range(TOP_K):
        pltpu.make_async_copy(x_ref, _packed(xs_hbm, 0, TM_DISPATCH), sem).wait()


def _dispatch(dest_flat, x1p, pend, padded, n_used, n_rows):
    t = x1p.shape[0] // ROW_WORDS
    grid_spec = pltpu.PrefetchScalarGridSpec(
        num_scalar_prefetch=3,
        grid=(t // TM_DISPATCH,),
        in_specs=[pl.BlockSpec(memory_space=pl.ANY), _packed_spec(TM_DISPATCH, lambda i, pe, pa, nu: (i, 0))],
        out_specs=pl.BlockSpec(memory_space=pl.ANY),
        scratch_shapes=[pltpu.SMEM((TM_DISPATCH * TOP_K,), jnp.int32),
                        pltpu.VMEM((EXPERT_BLOCK * ROW_WORDS, LANES), jnp.uint32),
                        pltpu.SemaphoreType.DMA, pltpu.SemaphoreType.DMA])
    return pl.pallas_call(
        _dispatch_kernel,
        grid_spec=grid_spec,
        out_shape=_packed_shape(n_rows),
        compiler_params=_cparams(("arbitrary",)),
        name="dispatch",
    )(pend, padded, n_used, dest_flat, x1p)


def _experts_kernel(first_ref, nblk_ref, nu_ref, xs_hbm, wg_ref, wu_ref, wd_ref, y_hbm,
                    xbuf, ybuf, wg_b, wu_b, wd_b, in_sem, out_sem):
    nc = EXPERT_BLOCK // EXPERT_SUB
    e = pl.program_id(0)
    n_used = nu_ref[0]
    n_blocks = y_hbm.shape[0] // (EXPERT_BLOCK * ROW_WORDS)
    first = first_ref[e]

    def slot_rows(ref, slot):
        return _packed(ref, slot * EXPERT_BLOCK, EXPERT_BLOCK)

    def fetch(b):
        return pltpu.make_async_copy(_packed(xs_hbm, b * EXPERT_BLOCK, EXPERT_BLOCK), slot_rows(xbuf, b % 2),
                                     in_sem.at[b % 2])

    def write_back(b):
        return pltpu.make_async_copy(slot_rows(ybuf, b % 2), _packed(y_hbm, b * EXPERT_BLOCK, EXPERT_BLOCK),
                                     out_sem.at[b % 2])

    @pl.when(e == 0)
    def _():
        fetch(0).start()

    @pl.when(nblk_ref[e] > 0)
    def _():
        wg_b[...] = wg_ref[...].astype(jnp.bfloat16)
        wu_b[...] = wu_ref[...].astype(jnp.bfloat16)
        wd_b[...] = wd_ref[...].astype(jnp.bfloat16)

    def block(j, carry):
        b = first + j
        slot = b % 2

        @pl.when(b + 1 < n_used)
        def _():
            fetch(b + 1).start()

        fetch(b).wait()

        @pl.when(b >= 2)
        def _():
            write_back(b - 2).wait()

        base = slot * EXPERT_BLOCK
        xbs = [_unpack_rows(xbuf, base + c * EXPERT_SUB, EXPERT_SUB).astype(jnp.bfloat16) for c in range(nc)]
        gus = [(jnp.dot(xb, wg_b[...], preferred_element_type=jnp.float32),
                jnp.dot(xb, wu_b[...], preferred_element_type=jnp.float32)) for xb in xbs]
        ys = []
        for g, u in gus:
            h = (g * _sigmoid(g) * u).astype(jnp.bfloat16)
            ys.append(jnp.dot(h, wd_b[...], preferred_element_type=jnp.float32))
        for c, y in enumerate(ys):
            _pack_rows(_packed(ybuf, base + c * EXPERT_SUB, EXPERT_SUB), y)
        write_back(b).start()
        return carry

    lax.fori_loop(0, nblk_ref[e], block, 0)

    @pl.when(e == pl.num_programs(0) - 1)
    def _():
        @pl.when(n_used >= 2)
        def _():
            write_back(n_used - 2).wait()
        write_back(n_used - 1).wait()

        xbuf[...] = jnp.zeros_like(xbuf)

        def zero_copy(b):
            return pltpu.make_async_copy(slot_rows(xbuf, 0), _packed(y_hbm, b * EXPERT_BLOCK, EXPERT_BLOCK),
                                         out_sem.at[0])

        def fill(b, c):
            zero_copy(b).start()
            return c
        lax.fori_loop(n_used, n_blocks, fill, 0)

        def drain(b, c):
            zero_copy(b).wait()
            return c
        lax.fori_loop(n_used, n_blocks, drain, 0)


def _experts(layer, first_blk, n_blk, n_used, xs, w_gate_e, w_up_e, w_down_e):
    wsel = lambda e, fb, nb, nu: (layer, e, 0, 0)
    slots = pltpu.VMEM((2 * EXPERT_BLOCK * ROW_WORDS, LANES), jnp.uint32)
    grid_spec = pltpu.PrefetchScalarGridSpec(
        num_scalar_prefetch=3,
        grid=(N_EXPERTS,),
        in_specs=[pl.BlockSpec(memory_space=pl.ANY),
                  pl.BlockSpec((None, None, D_MODEL, D_EXPERT), wsel),
                  pl.BlockSpec((None, None, D_MODEL, D_EXPERT), wsel),
                  pl.BlockSpec((None, None, D_EXPERT, D_MODEL), wsel)],
        out_specs=pl.BlockSpec(memory_space=pl.ANY),
        scratch_shapes=[slots, slots,
                        pltpu.VMEM((D_MODEL, D_EXPERT), jnp.bfloat16), pltpu.VMEM((D_MODEL, D_EXPERT), jnp.bfloat16),
                        pltpu.VMEM((D_EXPERT, D_MODEL), jnp.bfloat16),
                        pltpu.SemaphoreType.DMA((2,)), pltpu.SemaphoreType.DMA((2,))])
    return pl.pallas_call(
        _experts_kernel,
        grid_spec=grid_spec,
        out_shape=jax.ShapeDtypeStruct(xs.shape, jnp.uint32),
        compiler_params=_cparams(("arbitrary",)),
        name="experts",
    )(first_blk, n_blk, n_used, xs, w_gate_e, w_up_e, w_down_e)


def _combine_kernel(alpha, dest_hbm, y_hbm, w_ref, x1_ref, wg_ref, wu_ref, wd_ref, g_ref, b_ref,
                    o_ref, idx_smem0, idx_smem1, buf, sem):
    i = pl.program_id(0)
    n = pl.num_programs(0)
    n_idx = TM_COMBINE * TOP_K

    def slab(slot, k):
        return (slot * TOP_K + k) * TM_COMBINE

    def start_gather(step, slot):
        idx_smem = (idx_smem0, idx_smem1)[slot]
        pltpu.sync_copy(dest_hbm.at[pl.ds(step * n_idx, n_idx)], idx_smem)

        def issue(t, c):
            for k in range(TOP_K):
                pltpu.make_async_copy(_packed(y_hbm, idx_smem[t * TOP_K + k]),
                                      _packed(buf, slab(slot, k) + t), sem.at[slot]).start(priority=k % 2)
            return c
        lax.fori_loop(0, TM_COMBINE, issue, 0)

    @pl.when(i == 0)
    def _():
        start_gather(0, 0)

    def run(slot):
        @pl.when(i + 1 < n)
        def _():
            start_gather(i + 1, 1 - slot)

        for k in range(TOP_K):
            pltpu.make_async_copy(_packed(y_hbm, 0, TM_COMBINE), _packed(buf, slab(slot, k), TM_COMBINE),
                                  sem.at[slot]).wait()

        w = w_ref[...]
        routed = jnp.zeros((TM_COMBINE, D_MODEL), jnp.float32)
        for k in range(TOP_K):
            routed = routed + w[:, k:k + 1] * _unpack_rows(buf, slab(slot, k), TM_COMBINE)

        x1 = x1_ref[...]
        xb = x1.astype(jnp.bfloat16)
        gs = jnp.dot(xb, wg_ref[...], preferred_element_type=jnp.float32)
        us = jnp.dot(xb, wu_ref[...], preferred_element_type=jnp.float32)
        hs = gs * _sigmoid(gs) * us
        shared = jnp.dot(hs.astype(jnp.bfloat16), wd_ref[...], preferred_element_type=jnp.float32)
        o_ref[...] = _layer_norm(alpha * x1 + (shared + routed), g_ref[...], b_ref[...])

    for slot in range(2):
        pl.when(i % 2 == slot)(functools.partial(run, slot))


def _combine(alpha, dest_flat, y, w_tk, x1, wg, wu, wd, g, b):
    t = x1.shape[0]
    tm = TM_COMBINE
    full = lambda a: pl.BlockSpec(a.shape, lambda i: (0,) * a.ndim)
    return pl.pallas_call(
        functools.partial(_combine_kernel, alpha),
        grid=(t // tm,),
        in_specs=[pl.BlockSpec(memory_space=pl.ANY), pl.BlockSpec(memory_space=pl.ANY),
                  pl.BlockSpec((tm, TOP_K), lambda i: (i, 0)), pl.BlockSpec((tm, D_MODEL), lambda i: (i, 0)),
                  full(wg), full(wu), full(wd), full(g), full(b)],
        out_specs=pl.BlockSpec((tm, D_MODEL), lambda i: (i, 0)),
        out_shape=jax.ShapeDtypeStruct((t, D_MODEL), jnp.float32),
        scratch_shapes=[pltpu.SMEM((tm * TOP_K,), jnp.int32), pltpu.SMEM((tm * TOP_K,), jnp.int32),
                        pltpu.VMEM((2 * TOP_K * tm * ROW_WORDS, LANES), jnp.uint32),
                        pltpu.SemaphoreType.DMA((2,))],
        compiler_params=_cparams(("arbitrary",)),
        name="combine",
    )(dest_flat, y, w_tk, x1, wg, wu, wd, g, b)


def _prep_in_weights(w, b_forget):
    o = np.cumsum((0,) + IN_SIZES)
    group = SWA_Q_HEADS // SWA_KV_HEADS
    qa = w[:, o[0]:o[1]].reshape(D_MODEL, SWA_KV_HEADS, group, HEAD_DIM)
    qa = qa.transpose(0, 2, 1, 3).reshape(D_MODEL, SWA_Q)
    zf = jnp.pad(w[:, o[6]:o[7]], ((0, 0), (0, LANES - FOX_HEADS)))
    w_main = jnp.concatenate([qa, w[:, o[1]:o[6]], w[:, o[7]:o[9]], zf], axis=1).astype(jnp.bfloat16)
    bf_row = jnp.pad(b_forget, (0, LANES - FOX_HEADS)).reshape(1, LANES).astype(jnp.float32)
    return w_main, bf_row


def _prep_proj_a(w):
    group = SWA_Q_HEADS // SWA_KV_HEADS
    w4 = w.reshape(SWA_KV_HEADS, group, HEAD_DIM, D_MODEL).transpose(1, 0, 2, 3)
    return w4.reshape(SWA_Q, D_MODEL).astype(jnp.bfloat16)


def _route_plan(counts, n_tokens):
    n_assign = n_tokens * TOP_K
    n_rows = (n_assign + N_EXPERTS * (EXPERT_BLOCK - 1) + EXPERT_BLOCK - 1) // EXPERT_BLOCK * EXPERT_BLOCK
    padded = (counts + EXPERT_BLOCK - 1) // EXPERT_BLOCK * EXPERT_BLOCK
    pend = jnp.cumsum(padded)
    pstart = pend - padded
    n_used = pend[-1] // EXPERT_BLOCK
    i32 = jnp.int32
    return (pstart.astype(jnp.float32).reshape(N_EXPERTS, 1), (pstart // EXPERT_BLOCK).astype(i32),
            (padded // EXPERT_BLOCK).astype(i32), n_used.reshape(1).astype(i32), pend.astype(i32),
            padded.astype(i32), n_rows)


def kernel(x, w_in, b_forget, attn_sinks, rel_bias, w_proj_a, w_proj_b, w_out, ln1_g, ln1_b,
           w_router, router_bias, w_gate_e, w_up_e, w_down_e, w_gate_s, w_up_s, w_down_s, ln2_g, ln2_b):
    batch, seq_len, d = x.shape
    depth = w_in.shape[0]
    alpha = (2 * depth) ** 0.25
    t = batch * seq_len
    bf16 = jnp.bfloat16
    x2 = x.reshape(t, d)
    bias_tab = _t5_bias_table(rel_bias)
    for l in range(depth):
        w_main, bf_row = _prep_in_weights(w_in[l], b_forget[l])
        qa, ka, va, qf, kf, vf, ga, gb = _in_proj(x2, w_main, bf_row, seq_len)
        ya = _swa(qa, ka, va, attn_sinks[l].astype(jnp.float32), bias_tab, seq_len)
        yb = _fox(qf, kf, vf, batch, seq_len)
        x1, x1p = _merge(alpha, x2, ya, yb, ga, gb, _prep_proj_a(w_proj_a[l]), w_proj_b[l].astype(bf16),
                         w_out[l].astype(bf16), ln1_g[l].reshape(1, d), ln1_b[l].reshape(1, d))
        top_e, w_kt, rank, counts = _router(x1, w_router[l].T.astype(bf16),
                                            router_bias[l].reshape(N_EXPERTS, 1).astype(jnp.float32))
        pstart_col, first_blk, n_blk, n_used, pend, padded, n_rows = _route_plan(counts[:, 0], t)
        dest_flat = _plan(pstart_col, top_e, rank).T.reshape(-1)
        xs = _dispatch(dest_flat, x1p, pend, padded, n_used, n_rows)
        y = _experts(l, first_blk, n_blk, n_used, xs, w_gate_e, w_up_e, w_down_e)
        x2 = _combine(alpha, dest_flat, y, w_kt.T, x1, w_gate_s[l].astype(bf16), w_up_s[l].astype(bf16),
                      w_down_s[l].astype(bf16), ln2_g[l].reshape(1, d), ln2_b[l].reshape(1, d))
    return x2.reshape(batch, seq_len, d)
```

```python
import functools
import math

import jax
import jax.numpy as jnp
import numpy as np
from jax import lax
from jax.experimental import pallas as pl
from jax.experimental.pallas import tpu as pltpu

D_MODEL = 1024
HEAD_DIM = 64
SWA_Q_HEADS = 8
SWA_KV_HEADS = 2
SWA_WINDOW = 128
FOX_HEADS = 8
Q_BLOCK = 128
N_BUCKETS = 32
MAX_DISTANCE = 128
N_EXPERTS = 256
TOP_K = 8
N_GROUPS = 8
TOPK_GROUPS = 4
GROUP_SIZE = N_EXPERTS // N_GROUPS
D_EXPERT = 256
D_SHARED = 256
ROUTED_SCALE = 2.5
LN_EPS = 1e-5

SWA_Q = SWA_Q_HEADS * HEAD_DIM
SWA_KV = SWA_KV_HEADS * HEAD_DIM
FOX_W = FOX_HEADS * HEAD_DIM
IN_SIZES = (SWA_Q, SWA_KV, SWA_KV, FOX_W, FOX_W, FOX_W, FOX_HEADS, D_MODEL, D_MODEL)

LANES = 128
HALF_D = D_MODEL // 2
ROW_WORDS = HALF_D // LANES
VMEM_LIMIT = 56 * 1024 * 1024
HI_MASK = 0xFFFF0000

TM_PROJ = 512
TM_MERGE = 512
TM_ROUTE = 256
TM_DISPATCH = 1024
TM_COMBINE = 512
TM_PLAN = 2048
EXPERT_BLOCK = 512
EXPERT_SUB = 512
SWA_STEP = 4
FOX_TILE = 1024
FOX_DIAG_BANDS = 2

AUG = HEAD_DIM


def _cparams(sem):
    return pltpu.CompilerParams(dimension_semantics=sem, vmem_limit_bytes=VMEM_LIMIT)


def _sigmoid(x):
    return 1.0 / (1.0 + jnp.exp(-x))


def _split3(c):
    hi = c.astype(jnp.bfloat16).astype(jnp.float32)
    r = c - hi
    mid = r.astype(jnp.bfloat16).astype(jnp.float32)
    lo = (r - mid).astype(jnp.bfloat16).astype(jnp.float32)
    return hi, mid, lo


_C_QA = 0
_C_KA = _C_QA + SWA_Q
_C_VA = _C_KA + SWA_KV
_C_QF = _C_VA + SWA_KV
_C_KF = _C_QF + FOX_W
_C_VF = _C_KF + FOX_W
_C_GA = _C_VF + FOX_W
_C_GB = _C_GA + D_MODEL
_C_ZF = _C_GB + D_MODEL
_N_PROJ = _C_ZF + LANES


def _in_proj_kernel(tiles_per_seq, x_ref, w_ref, bf_ref, tri_ref,
                    qa_ref, ka_ref, va_ref, qf_ref, kf_ref, vf_ref, ga_ref, gb_ref, carry_ref):
    i = pl.program_id(0)
    tm = x_ref.shape[0]
    xb = x_ref[...].astype(jnp.bfloat16)

    def proj(c0, n):
        return jnp.dot(xb, w_ref[:, c0:c0 + n], preferred_element_type=jnp.float32)

    scale = HEAD_DIM ** -0.5
    qa_ref[...] = (proj(_C_QA, SWA_Q) * scale).astype(jnp.bfloat16)
    ka_ref[...] = proj(_C_KA, SWA_KV).astype(jnp.bfloat16)
    va_ref[...] = proj(_C_VA, SWA_KV).astype(jnp.bfloat16)
    ga_ref[...] = proj(_C_GA, D_MODEL)
    gb_ref[...] = proj(_C_GB, D_MODEL)

    z = proj(_C_ZF, LANES) + bf_ref[...]
    log_f = jnp.minimum(z, 0.0) - jnp.log(1.0 + jnp.exp(-jnp.abs(z)))

    @pl.when(i % tiles_per_seq == 0)
    def _():
        carry_ref[...] = jnp.zeros_like(carry_ref)

    tri = tri_ref[...]
    h3 = _split3(log_f)
    c = carry_ref[...]
    for part in h3:
        c = c + jnp.dot(tri, part.astype(jnp.bfloat16), preferred_element_type=jnp.float32)
    carry_ref[...] = c[tm - 1:tm, :]

    qf = proj(_C_QF, FOX_W) * scale
    kf = proj(_C_KF, FOX_W)
    vf = proj(_C_VF, FOX_W)
    lane = lax.broadcasted_iota(jnp.int32, (tm, LANES), 1)
    one = jnp.float32(1.0)
    zero = jnp.float32(0.0)
    for h in range(FOX_HEADS):
        pair = (h // 2) * LANES
        qp = qf[:, pair:pair + LANES]
        kp = kf[:, pair:pair + LANES]
        vp = vf[:, pair:pair + LANES]
        if h % 2 == 1:
            qp = pltpu.roll(qp, HEAD_DIM, axis=1)
            kp = pltpu.roll(kp, HEAD_DIM, axis=1)
            vp = pltpu.roll(vp, HEAD_DIM, axis=1)
        ch = jnp.broadcast_to(c[:, h:h + 1], (tm, LANES))
        hi, mid, lo = _split3(ch)
        q_aug = jnp.where(lane == AUG, hi, jnp.where(lane == AUG + 1, mid, jnp.where(
            lane == AUG + 2, lo, jnp.where(lane < AUG + 6, one, zero))))
        k_aug = jnp.where(lane < AUG + 3, one, jnp.where(lane == AUG + 3, -hi, jnp.where(
            lane == AUG + 4, -mid, jnp.where(lane == AUG + 5, -lo, zero))))
        qf_ref[h] = jnp.where(lane < AUG, qp, q_aug).astype(jnp.bfloat16)
        kf_ref[h] = jnp.where(lane < AUG, kp, k_aug).astype(jnp.bfloat16)
        vf_ref[h] = jnp.where(lane < AUG, vp, jnp.where(lane == AUG, one, zero)).astype(jnp.bfloat16)


def _in_proj(x2, w_main, bf_row, seq_len):
    t = x2.shape[0]
    tm = min(TM_PROJ, seq_len)
    tri = jnp.tril(jnp.ones((tm, tm), jnp.bfloat16))
    row = lambda n: pl.BlockSpec((tm, n), lambda i: (i, 0))
    full = lambda a: pl.BlockSpec(a.shape, lambda i: (0,) * a.ndim)
    slab = pl.BlockSpec((FOX_HEADS, tm, LANES), lambda i: (0, i, 0))
    bf16, f32 = jnp.bfloat16, jnp.float32
    slab_shape = jax.ShapeDtypeStruct((FOX_HEADS, t, LANES), bf16)
    out_shape = (
        jax.ShapeDtypeStruct((t, SWA_Q), bf16), jax.ShapeDtypeStruct((t, SWA_KV), bf16),
        jax.ShapeDtypeStruct((t, SWA_KV), bf16), slab_shape, slab_shape, slab_shape,
        jax.ShapeDtypeStruct((t, D_MODEL), f32), jax.ShapeDtypeStruct((t, D_MODEL), f32))
    return pl.pallas_call(
        functools.partial(_in_proj_kernel, seq_len // tm),
        grid=(t // tm,),
        in_specs=[row(D_MODEL), full(w_main), full(bf_row), full(tri)],
        out_specs=(row(SWA_Q), row(SWA_KV), row(SWA_KV), slab, slab, slab, row(D_MODEL), row(D_MODEL)),
        out_shape=out_shape,
        scratch_shapes=[pltpu.VMEM((1, LANES), f32)],
        compiler_params=_cparams(("arbitrary",)),
        name="in_proj",
    )(x2, w_main, bf_row, tri)


def _swa_kernel(steps_per_seq, sink_ref, q_ref, kp_ref, kc_ref, vp_ref, vc_ref, bias_ref, o_ref):
    n = pl.program_id(0)
    kall = jnp.concatenate([kp_ref[...], kc_ref[...]], axis=0)
    vall = jnp.concatenate([vp_ref[...], vc_ref[...]], axis=0)
    qi = lax.broadcasted_iota(jnp.int32, (Q_BLOCK, 2 * Q_BLOCK), 0)
    kj = lax.broadcasted_iota(jnp.int32, (Q_BLOCK, 2 * Q_BLOCK), 1)
    dist = qi + Q_BLOCK - kj
    in_window = (dist >= 0) & (dist < SWA_WINDOW)
    first_key = jnp.where((n % steps_per_seq) > 0, 0, Q_BLOCK)
    lane = lax.broadcasted_iota(jnp.int32, (Q_BLOCK, LANES), 1)
    lo_half = lane < HEAD_DIM
    group = SWA_Q_HEADS // SWA_KV_HEADS
    for b in range(SWA_STEP):
        rows = slice(b * Q_BLOCK, (b + 1) * Q_BLOCK)
        kk = kall[b * Q_BLOCK:(b + 2) * Q_BLOCK]
        vv = vall[b * Q_BLOCK:(b + 2) * Q_BLOCK]
        valid = (in_window & (kj >= first_key)) if b == 0 else in_window
        for j in range(group):
            qs = q_ref[rows, j * LANES:(j + 1) * LANES]
            outs = []
            for kv in range(SWA_KV_HEADS):
                head = kv * group + j
                qm = jnp.where(lo_half if kv == 0 else ~lo_half, qs, jnp.zeros_like(qs))
                s = lax.dot_general(qm, kk, (((1,), (1,)), ((), ())), preferred_element_type=jnp.float32)
                s = jnp.where(valid, s + bias_ref[head], -jnp.inf)
                sink = sink_ref[head]
                m = jnp.maximum(jnp.max(s, axis=-1, keepdims=True), sink)
                p = jnp.exp(s - m)
                denom = jnp.sum(p, axis=-1, keepdims=True) + jnp.exp(sink - m)
                pn = (p / denom).astype(jnp.bfloat16)
                outs.append(jnp.dot(pn, vv, preferred_element_type=jnp.float32))
            o_ref[rows, j * LANES:(j + 1) * LANES] = jnp.where(lo_half, outs[0], outs[1]).astype(o_ref.dtype)


def _swa(qa, ka, va, sinks, bias_tab, seq_len):
    t = qa.shape[0]
    tm = SWA_STEP * Q_BLOCK
    assert seq_len % tm == 0
    cur = lambda n, s: (n, 0)
    prev = lambda n, s: (jnp.maximum(n * SWA_STEP - 1, 0), 0)
    grid_spec = pltpu.PrefetchScalarGridSpec(
        num_scalar_prefetch=1,
        grid=(t // tm,),
        in_specs=[pl.BlockSpec((tm, SWA_Q), cur),
                  pl.BlockSpec((Q_BLOCK, SWA_KV), prev), pl.BlockSpec((tm, SWA_KV), cur),
                  pl.BlockSpec((Q_BLOCK, SWA_KV), prev), pl.BlockSpec((tm, SWA_KV), cur),
                  pl.BlockSpec(bias_tab.shape, lambda n, s: (0, 0, 0))],
        out_specs=pl.BlockSpec((tm, SWA_Q), cur))
    return pl.pallas_call(
        functools.partial(_swa_kernel, seq_len // tm),
        grid_spec=grid_spec,
        out_shape=jax.ShapeDtypeStruct((t, SWA_Q), jnp.bfloat16),
        compiler_params=_cparams(("arbitrary",)),
        name="swa",
    )(sinks, qa, ka, ka, va, va, bias_tab)


def _t5_bias_table(rel_bias):
    i = np.arange(Q_BLOCK)[:, None]
    j = np.arange(2 * Q_BLOCK)[None, :]
    n = jnp.asarray(np.maximum(i + Q_BLOCK - j, 0))
    max_exact = N_BUCKETS // 2
    nf = jnp.maximum(n, 1).astype(jnp.float32)
    large = max_exact + (jnp.log(nf / max_exact) / math.log(MAX_DISTANCE / max_exact)
                         * (N_BUCKETS - max_exact)).astype(jnp.int32)
    large = jnp.minimum(large, N_BUCKETS - 1)
    bucket = jnp.where(n < max_exact, n, large)
    onehot = bucket[None] == jnp.arange(N_BUCKETS)[:, None, None]
    rb = rel_bias.astype(jnp.float32)
    return jnp.sum(jnp.where(onehot[:, None], rb[:, :, None, None], 0.0), axis=0)


def _fox_kernel(q_ref, k_ref, v_ref, o_ref):
    qi = pl.program_id(2)
    tq = q_ref.shape[1]
    qs = (q_ref[0], q_ref[1])

    def update(s, m, acc, v):
        m_new = jnp.maximum(m, jnp.max(s, axis=-1, keepdims=True))
        p = jnp.exp(s - m_new).astype(jnp.bfloat16)
        return m_new, jnp.exp(m - m_new) * acc + jnp.dot(p, v, preferred_element_type=jnp.float32)

    def scores(q, k):
        return lax.dot_general(q, k, (((1,), (1,)), ((), ())), preferred_element_type=jnp.float32)

    def step(j, carry):
        start = pl.multiple_of(j * tq, tq)
        return tuple(update(scores(qs[hh], k_ref[hh, pl.ds(start, tq), :]), *carry[hh],
                            v_ref[hh, pl.ds(start, tq), :]) for hh in range(2))

    def diagonal(carry):
        start = pl.multiple_of(qi * tq, tq)
        band = tq // FOX_DIAG_BANDS
        new = []
        for hh in range(2):
            m, acc = carry[hh]
            parts = []
            for r in range(FOX_DIAG_BANDS):
                lo, hi = r * band, (r + 1) * band
                s = scores(qs[hh][lo:hi], k_ref[hh, pl.ds(start, hi), :])
                row = lax.broadcasted_iota(jnp.int32, (band, hi), 0) + lo
                col = lax.broadcasted_iota(jnp.int32, (band, hi), 1)
                s = jnp.where(col <= row, s, -jnp.inf)
                parts.append(update(s, m[lo:hi], acc[lo:hi], v_ref[hh, pl.ds(start, hi), :]))
            new.append((jnp.concatenate([p[0] for p in parts], axis=0),
                        jnp.concatenate([p[1] for p in parts], axis=0)))
        return tuple(new)

    init1 = (jnp.full((tq, 1), -jnp.inf, jnp.float32), jnp.zeros((tq, LANES), jnp.float32))
    carry = diagonal(lax.fori_loop(0, qi, step, (init1, init1)))
    outs = [acc / acc[:, AUG:AUG + 1] for _, acc in carry]
    lane = lax.broadcasted_iota(jnp.int32, (tq, LANES), 1)
    o_ref[...] = jnp.where(lane < HEAD_DIM, outs[0], pltpu.roll(outs[1], HEAD_DIM, axis=1)).astype(o_ref.dtype)


def _fox(qf, kf, vf, batch, seq_len):
    t = qf.shape[1]
    tq = min(FOX_TILE, seq_len)
    assert seq_len % tq == 0 and tq % FOX_DIAG_BANDS == 0
    nq = seq_len // tq
    pairs = FOX_HEADS // 2
    seq = pl.BlockSpec((2, seq_len, LANES), lambda b, hp, i: (hp, b, 0))
    return pl.pallas_call(
        _fox_kernel,
        grid=(batch, pairs, nq),
        in_specs=[pl.BlockSpec((2, tq, LANES), lambda b, hp, i: (hp, b * nq + i, 0)), seq, seq],
        out_specs=pl.BlockSpec((tq, LANES), lambda b, hp, i: (b * nq + i, hp)),
        out_shape=jax.ShapeDtypeStruct((t, FOX_W), jnp.bfloat16),
        compiler_params=_cparams(("arbitrary", "arbitrary", "arbitrary")),
        name="fox",
    )(qf, kf, vf)


def _layer_norm(h, g, b):
    mu = jnp.mean(h, axis=-1, keepdims=True)
    d = h - mu
    var = jnp.mean(d * d, axis=-1, keepdims=True)
    return d * lax.rsqrt(var + LN_EPS) * g + b


def _pack_rows(ref, x):
    rows = x.shape[0]
    bits = pltpu.bitcast(x.astype(jnp.bfloat16).astype(jnp.float32), jnp.uint32)
    for c in range(ROW_WORDS):
        lo = bits[:, c * LANES:(c + 1) * LANES] >> 16
        hi = bits[:, HALF_D + c * LANES:HALF_D + (c + 1) * LANES] & jnp.uint32(HI_MASK)
        ref[pl.ds(c, rows, stride=ROW_WORDS), :] = lo | hi


def _unpack_rows(ref, first, rows):
    los, his = [], []
    for c in range(ROW_WORDS):
        w = ref[pl.ds(first * ROW_WORDS + c, rows, stride=ROW_WORDS), :]
        los.append(pltpu.bitcast(w << 16, jnp.float32))
        his.append(pltpu.bitcast(w & jnp.uint32(HI_MASK), jnp.float32))
    return jnp.concatenate(los + his, axis=1)


def _packed(ref, first, rows=1):
    return ref.at[pl.ds(pl.multiple_of(first * ROW_WORDS, ROW_WORDS), rows * ROW_WORDS), :]


def _packed_spec(rows, index_map):
    return pl.BlockSpec((rows * ROW_WORDS, LANES), index_map)


def _packed_shape(rows):
    return jax.ShapeDtypeStruct((rows * ROW_WORDS, LANES), jnp.uint32)


def _merge_kernel(alpha, x_ref, ya_ref, yb_ref, ga_ref, gb_ref, wa_ref, wb_ref, wo_ref, g_ref, b_ref,
                  x1_ref, x1p_ref):
    pa = jnp.dot(ya_ref[...], wa_ref[...], preferred_element_type=jnp.float32)
    pb = jnp.dot(yb_ref[...], wb_ref[...], preferred_element_type=jnp.float32)
    mix = _sigmoid(ga_ref[...]) * pa + _sigmoid(gb_ref[...]) * pb
    out = jnp.dot(mix.astype(jnp.bfloat16), wo_ref[...], preferred_element_type=jnp.float32)
    x1 = _layer_norm(alpha * x_ref[...] + out, g_ref[...], b_ref[...])
    x1_ref[...] = x1
    _pack_rows(x1p_ref, x1)


def _merge(alpha, x2, ya, yb, ga, gb, wa, wb, wo, g, b):
    t = x2.shape[0]
    tm = TM_MERGE
    row = lambda n: pl.BlockSpec((tm, n), lambda i: (i, 0))
    full = lambda a: pl.BlockSpec(a.shape, lambda i: (0,) * a.ndim)
    return pl.pallas_call(
        functools.partial(_merge_kernel, alpha),
        grid=(t // tm,),
        in_specs=[row(D_MODEL), row(SWA_Q), row(FOX_W), row(D_MODEL), row(D_MODEL),
                  full(wa), full(wb), full(wo), full(g), full(b)],
        out_specs=(row(D_MODEL), _packed_spec(tm, lambda i: (i, 0))),
        out_shape=(jax.ShapeDtypeStruct((t, D_MODEL), jnp.float32), _packed_shape(t)),
        compiler_params=_cparams(("arbitrary",)),
        name="merge",
    )(x2, ya, yb, ga, gb, wa, wb, wo, g, b)


def _first_argmax(v, idx):
    m = jnp.max(v, axis=0, keepdims=True)
    first = jnp.min(jnp.where(v == m, idx, v.shape[0]), axis=0, keepdims=True)
    return m, first


def _router_kernel(x_ref, wr_ref, rb_ref, tri_ref, e_ref, w_ref, r_ref, cnt_ref, carry_ref):
    i = pl.program_id(0)
    tm = x_ref.shape[0]

    @pl.when(i == 0)
    def _():
        carry_ref[...] = jnp.zeros_like(carry_ref)

    xb = x_ref[...].astype(jnp.bfloat16)
    logits = lax.dot_general(wr_ref[...], xb, (((1,), (1,)), ((), ())),
                             preferred_element_type=jnp.float32)
    scores = _sigmoid(logits)
    biased = scores + rb_ref[...]

    in_group = lax.broadcasted_iota(jnp.int32, (GROUP_SIZE, tm), 0)
    grows = []
    for g in range(N_GROUPS):
        blk = biased[g * GROUP_SIZE:(g + 1) * GROUP_SIZE, :]
        m1, f1 = _first_argmax(blk, in_group)
        m2 = jnp.max(jnp.where(in_group == f1, -jnp.inf, blk), axis=0, keepdims=True)
        grows.append(m1 + m2)
    gscore = jnp.concatenate(grows, axis=0)

    gidx = lax.broadcasted_iota(jnp.int32, (N_GROUPS, tm), 0)
    gwork = gscore
    gsel = jnp.zeros((N_GROUPS, tm), jnp.float32)
    for _ in range(TOPK_GROUPS):
        _, first = _first_argmax(gwork, gidx)
        hit = gidx == first
        gsel = jnp.where(hit, 1.0, gsel)
        gwork = jnp.where(hit, -jnp.inf, gwork)

    eidx = lax.broadcasted_iota(jnp.int32, (N_EXPERTS, tm), 0)
    work = jnp.concatenate(
        [jnp.where(jnp.broadcast_to(gsel[g:g + 1, :], (GROUP_SIZE, tm)) > 0.0,
                   biased[g * GROUP_SIZE:(g + 1) * GROUP_SIZE, :], -jnp.inf) for g in range(N_GROUPS)], axis=0)
    picks = []
    sel = jnp.zeros((N_EXPERTS, tm), jnp.bool_)
    for _ in range(TOP_K):
        _, first = _first_argmax(work, eidx)
        hit = eidx == first
        picks.append((first, hit))
        sel = sel | hit
        work = jnp.where(hit, -jnp.inf, work)

    wsum = jnp.sum(jnp.where(sel, scores, 0.0), axis=0, keepdims=True)
    sel_b = jnp.where(sel, 1.0, 0.0).astype(jnp.bfloat16)
    before = carry_ref[...] + jnp.dot(sel_b, tri_ref[...], preferred_element_type=jnp.float32)
    e_rows, w_rows, r_rows = [], [], []
    for first, hit in picks:
        e_rows.append(first)
        sc = jnp.sum(jnp.where(hit, scores, 0.0), axis=0, keepdims=True)
        w_rows.append(sc / wsum * ROUTED_SCALE)
        r_rows.append(jnp.sum(jnp.where(hit, before, 0.0), axis=0, keepdims=True).astype(jnp.int32))
    e_ref[...] = jnp.concatenate(e_rows, axis=0)
    w_ref[...] = jnp.concatenate(w_rows, axis=0)
    r_ref[...] = jnp.concatenate(r_rows, axis=0)
    total = carry_ref[...] + jnp.sum(sel_b.astype(jnp.float32), axis=1, keepdims=True)
    carry_ref[...] = total
    cnt_ref[...] = jnp.broadcast_to(total, cnt_ref.shape).astype(jnp.int32)


def _router(x1, wr_t, rb_col):
    t = x1.shape[0]
    tm = TM_ROUTE
    tri = jnp.triu(jnp.ones((tm, tm), jnp.bfloat16), k=1)
    full = lambda a: pl.BlockSpec(a.shape, lambda i: (0,) * a.ndim)
    kt = pl.BlockSpec((TOP_K, tm), lambda i: (0, i))
    return pl.pallas_call(
        _router_kernel,
        grid=(t // tm,),
        in_specs=[pl.BlockSpec((tm, D_MODEL), lambda i: (i, 0)), full(wr_t), full(rb_col), full(tri)],
        out_specs=(kt, kt, kt, pl.BlockSpec((N_EXPERTS, LANES), lambda i: (0, 0))),
        out_shape=(jax.ShapeDtypeStruct((TOP_K, t), jnp.int32), jax.ShapeDtypeStruct((TOP_K, t), jnp.float32),
                   jax.ShapeDtypeStruct((TOP_K, t), jnp.int32),
                   jax.ShapeDtypeStruct((N_EXPERTS, LANES), jnp.int32)),
        scratch_shapes=[pltpu.VMEM((N_EXPERTS, 1), jnp.float32)],
        compiler_params=_cparams(("arbitrary",)),
        name="router",
    )(x1, wr_t, rb_col, tri)


def _plan_kernel(ps_ref, e_ref, r_ref, d_ref):
    tm = e_ref.shape[1]
    eidx = lax.broadcasted_iota(jnp.int32, (N_EXPERTS, tm), 0)
    rows = []
    for k in range(TOP_K):
        hit = eidx == e_ref[k:k + 1, :]
        base = jnp.sum(jnp.where(hit, ps_ref[...], 0.0), axis=0, keepdims=True)
        rows.append(base.astype(jnp.int32) + r_ref[k:k + 1, :])
    d_ref[...] = jnp.concatenate(rows, axis=0)


def _plan(pstart_col, top_e, rank):
    t = top_e.shape[1]
    tm = min(TM_PLAN, t)
    kt = pl.BlockSpec((TOP_K, tm), lambda i: (0, i))
    return pl.pallas_call(
        _plan_kernel,
        grid=(t // tm,),
        in_specs=[pl.BlockSpec(pstart_col.shape, lambda i: (0, 0)), kt, kt],
        out_specs=kt,
        out_shape=jax.ShapeDtypeStruct((TOP_K, t), jnp.int32),
        compiler_params=_cparams(("arbitrary",)),
        name="plan",
    )(pstart_col, top_e, rank)


def _dispatch_kernel(pend_ref, padded_ref, nu_ref, dest_hbm, x_ref, xs_hbm, idx_smem, zero_vmem, sem, zsem):
    i = pl.program_id(0)
    n_blocks = xs_hbm.shape[0] // (EXPERT_BLOCK * ROW_WORDS)

    def zero_copy(first_row):
        return pltpu.make_async_copy(zero_vmem, _packed(xs_hbm, first_row, EXPERT_BLOCK), zsem)

    @pl.when(i == 0)
    def _():
        zero_vmem[...] = jnp.zeros_like(zero_vmem)

        def fill(e, c):
            @pl.when(padded_ref[e] > 0)
            def _():
                zero_copy(pend_ref[e] - EXPERT_BLOCK).start()
            return c
        lax.fori_loop(0, N_EXPERTS, fill, 0)

        def fill_tail(b, c):
            zero_copy(b * EXPERT_BLOCK).start()
            return c
        lax.fori_loop(nu_ref[0], n_blocks, fill_tail, 0)

        def drain(e, c):
            @pl.when(padded_ref[e] > 0)
            def _():
                zero_copy(0).wait()
            return c
        lax.fori_loop(0, N_EXPERTS, drain, 0)

        def drain_tail(b, c):
            zero_copy(0).wait()
            return c
        lax.fori_loop(nu_ref[0], n_blocks, drain_tail, 0)

    n_idx = TM_DISPATCH * TOP_K
    pltpu.sync_copy(dest_hbm.at[pl.ds(i * n_idx, n_idx)], idx_smem)

    def issue(t, c):
        for k in range(TOP_K):
            pltpu.make_async_copy(_packed(x_ref, t), _packed(xs_hbm, idx_smem[t * TOP_K + k]),
                                  sem).start(priority=k % 2)
        return c
    lax.fori_loop(0, TM_DISPATCH, issue, 0)

    for k in range(TOP_K):
        pltpu.make_async_copy(x_ref, _packed(xs_hbm, 0, TM_DISPATCH), sem).wait()


def _dispatch(dest_flat, x1p, pend, padded, n_used, n_rows):
    t = x1p.shape[0] // ROW_WORDS
    grid_spec = pltpu.PrefetchScalarGridSpec(
        num_scalar_prefetch=3,
        grid=(t // TM_DISPATCH,),
        in_specs=[pl.BlockSpec(memory_space=pl.ANY), _packed_spec(TM_DISPATCH, lambda i, pe, pa, nu: (i, 0))],
        out_specs=pl.BlockSpec(memory_space=pl.ANY),
        scratch_shapes=[pltpu.SMEM((TM_DISPATCH * TOP_K,), jnp.int32),
                        pltpu.VMEM((EXPERT_BLOCK * ROW_WORDS, LANES), jnp.uint32),
                        pltpu.SemaphoreType.DMA, pltpu.SemaphoreType.DMA])
    return pl.pallas_call(
        _dispatch_kernel,
        grid_spec=grid_spec,
        out_shape=_packed_shape(n_rows),
        compiler_params=_cparams(("arbitrary",)),
        name="dispatch",
    )(pend, padded, n_used, dest_flat, x1p)


def _experts_kernel(first_ref, nblk_ref, nu_ref, xs_hbm, wg_ref, wu_ref, wd_ref, y_hbm,
                    xbuf, ybuf, wg_b, wu_b, wd_b, in_sem, out_sem):
    nc = EXPERT_BLOCK // EXPERT_SUB
    e = pl.program_id(0)
    n_used = nu_ref[0]
    n_blocks = y_hbm.shape[0] // (EXPERT_BLOCK * ROW_WORDS)
    first = first_ref[e]

    def slot_rows(ref, slot):
        return _packed(ref, slot * EXPERT_BLOCK, EXPERT_BLOCK)

    def fetch(b):
        return pltpu.make_async_copy(_packed(xs_hbm, b * EXPERT_BLOCK, EXPERT_BLOCK), slot_rows(xbuf, b % 2),
                                     in_sem.at[b % 2])

    def write_back(b):
        return pltpu.make_async_copy(slot_rows(ybuf, b % 2), _packed(y_hbm, b * EXPERT_BLOCK, EXPERT_BLOCK),
                                     out_sem.at[b % 2])

    @pl.when(e == 0)
    def _():
        fetch(0).start()

    @pl.when(nblk_ref[e] > 0)
    def _():
        wg_b[...] = wg_ref[...].astype(jnp.bfloat16)
        wu_b[...] = wu_ref[...].astype(jnp.bfloat16)
        wd_b[...] = wd_ref[...].astype(jnp.bfloat16)

    def block(j, carry):
        b = first + j
        slot = b % 2

        @pl.when(b + 1 < n_used)
        def _():
            fetch(b + 1).start()

        fetch(b).wait()

        @pl.when(b >= 2)
        def _():
            write_back(b - 2).wait()

        base = slot * EXPERT_BLOCK
        xbs = [_unpack_rows(xbuf, base + c * EXPERT_SUB, EXPERT_SUB).astype(jnp.bfloat16) for c in range(nc)]
        gus = [(jnp.dot(xb, wg_b[...], preferred_element_type=jnp.float32),
                jnp.dot(xb, wu_b[...], preferred_element_type=jnp.float32)) for xb in xbs]
        ys = []
        for g, u in gus:
            h = (g * _sigmoid(g) * u).astype(jnp.bfloat16)
            ys.append(jnp.dot(h, wd_b[...], preferred_element_type=jnp.float32))
        for c, y in enumerate(ys):
            _pack_rows(_packed(ybuf, base + c * EXPERT_SUB, EXPERT_SUB), y)
        write_back(b).start()
        return carry

    lax.fori_loop(0, nblk_ref[e], block, 0)

    @pl.when(e == pl.num_programs(0) - 1)
    def _():
        @pl.when(n_used >= 2)
        def _():
            write_back(n_used - 2).wait()
        write_back(n_used - 1).wait()

        xbuf[...] = jnp.zeros_like(xbuf)

        def zero_copy(b):
            return pltpu.make_async_copy(slot_rows(xbuf, 0), _packed(y_hbm, b * EXPERT_BLOCK, EXPERT_BLOCK),
                                         out_sem.at[0])

        def fill(b, c):
            zero_copy(b).start()
            return c
        lax.fori_loop(n_used, n_blocks, fill, 0)

        def drain(b, c):
            zero_copy(b).wait()
            return c
        lax.fori_loop(n_used, n_blocks, drain, 0)


def _experts(layer, first_blk, n_blk, n_used, xs, w_gate_e, w_up_e, w_down_e):
    wsel = lambda e, fb, nb, nu: (layer, e, 0, 0)
    slots = pltpu.VMEM((2 * EXPERT_BLOCK * ROW_WORDS, LANES), jnp.uint32)
    grid_spec = pltpu.PrefetchScalarGridSpec(
        num_scalar_prefetch=3,
        grid=(N_EXPERTS,),
        in_specs=[pl.BlockSpec(memory_space=pl.ANY),
                  pl.BlockSpec((None, None, D_MODEL, D_EXPERT), wsel),
                  pl.BlockSpec((None, None, D_MODEL, D_EXPERT), wsel),
                  pl.BlockSpec((None, None, D_EXPERT, D_MODEL), wsel)],
        out_specs=pl.BlockSpec(memory_space=pl.ANY),
        scratch_shapes=[slots, slots,
                        pltpu.VMEM((D_MODEL, D_EXPERT), jnp.bfloat16), pltpu.VMEM((D_MODEL, D_EXPERT), jnp.bfloat16),
                        pltpu.VMEM((D_EXPERT, D_MODEL), jnp.bfloat16),
                        pltpu.SemaphoreType.DMA((2,)), pltpu.SemaphoreType.DMA((2,))])
    return pl.pallas_call(
        _experts_kernel,
        grid_spec=grid_spec,
        out_shape=jax.ShapeDtypeStruct(xs.shape, jnp.uint32),
        compiler_params=_cparams(("arbitrary",)),
        name="experts",
    )(first_blk, n_blk, n_used, xs, w_gate_e, w_up_e, w_down_e)


def _combine_kernel(alpha, dest_hbm, y_hbm, w_ref, x1_ref, wg_ref, wu_ref, wd_ref, g_ref, b_ref,
                    o_ref, idx_smem0, idx_smem1, buf, sem):
    i = pl.program_id(0)
    n = pl.num_programs(0)
    n_idx = TM_COMBINE * TOP_K

    def slab(slot, k):
        return (slot * TOP_K + k) * TM_COMBINE

    def start_gather(step, slot):
        idx_smem = (idx_smem0, idx_smem1)[slot]
        pltpu.sync_copy(dest_hbm.at[pl.ds(step * n_idx, n_idx)], idx_smem)

        def issue(t, c):
            for k in range(TOP_K):
                pltpu.make_async_copy(_packed(y_hbm, idx_smem[t * TOP_K + k]),
                                      _packed(buf, slab(slot, k) + t), sem.at[slot]).start(priority=k % 2)
            return c
        lax.fori_loop(0, TM_COMBINE, issue, 0)

    @pl.when(i == 0)
    def _():
        start_gather(0, 0)

    def run(slot):
        @pl.when(i + 1 < n)
        def _():
            start_gather(i + 1, 1 - slot)

        for k in range(TOP_K):
            pltpu.make_async_copy(_packed(y_hbm, 0, TM_COMBINE), _packed(buf, slab(slot, k), TM_COMBINE),
                                  sem.at[slot]).wait()

        w = w_ref[...]
        routed = jnp.zeros((TM_COMBINE, D_MODEL), jnp.float32)
        for k in range(TOP_K):
            routed = routed + w[:, k:k + 1] * _unpack_rows(buf, slab(slot, k), TM_COMBINE)

        x1 = x1_ref[...]
        xb = x1.astype(jnp.bfloat16)
        gs = jnp.dot(xb, wg_ref[...], preferred_element_type=jnp.float32)
        us = jnp.dot(xb, wu_ref[...], preferred_element_type=jnp.float32)
        hs = gs * _sigmoid(gs) * us
        shared = jnp.dot(hs.astype(jnp.bfloat16), wd_ref[...], preferred_element_type=jnp.float32)
        o_ref[...] = _layer_norm(alpha * x1 + (shared + routed), g_ref[...], b_ref[...])

    for slot in range(2):
        pl.when(i % 2 == slot)(functools.partial(run, slot))


def _combine(alpha, dest_flat, y, w_tk, x1, wg, wu, wd, g, b):
    t = x1.shape[0]
    tm = TM_COMBINE
    full = lambda a: pl.BlockSpec(a.shape, lambda i: (0,) * a.ndim)
    return pl.pallas_call(
        functools.partial(_combine_kernel, alpha),
        grid=(t // tm,),
        in_specs=[pl.BlockSpec(memory_space=pl.ANY), pl.BlockSpec(memory_space=pl.ANY),
                  pl.BlockSpec((tm, TOP_K), lambda i: (i, 0)), pl.BlockSpec((tm, D_MODEL), lambda i: (i, 0)),
                  full(wg), full(wu), full(wd), full(g), full(b)],
        out_specs=pl.BlockSpec((tm, D_MODEL), lambda i: (i, 0)),
        out_shape=jax.ShapeDtypeStruct((t, D_MODEL), jnp.float32),
        scratch_shapes=[pltpu.SMEM((tm * TOP_K,), jnp.int32), pltpu.SMEM((tm * TOP_K,), jnp.int32),
                        pltpu.VMEM((2 * TOP_K * tm * ROW_WORDS, LANES), jnp.uint32),
                        pltpu.SemaphoreType.DMA((2,))],
        compiler_params=_cparams(("arbitrary",)),
        name="combine",
    )(dest_flat, y, w_tk, x1, wg, wu, wd, g, b)


def _prep_in_weights(w, b_forget):
    o = np.cumsum((0,) + IN_SIZES)
    group = SWA_Q_HEADS // SWA_KV_HEADS
    qa = w[:, o[0]:o[1]].reshape(D_MODEL, SWA_KV_HEADS, group, HEAD_DIM)
    qa = qa.transpose(0, 2, 1, 3).reshape(D_MODEL, SWA_Q)
    zf = jnp.pad(w[:, o[6]:o[7]], ((0, 0), (0, LANES - FOX_HEADS)))
    w_main = jnp.concatenate([qa, w[:, o[1]:o[6]], w[:, o[7]:o[9]], zf], axis=1).astype(jnp.bfloat16)
    bf_row = jnp.pad(b_forget, (0, LANES - FOX_HEADS)).reshape(1, LANES).astype(jnp.float32)
    return w_main, bf_row


def _prep_proj_a(w):
    group = SWA_Q_HEADS // SWA_KV_HEADS
    w4 = w.reshape(SWA_KV_HEADS, group, HEAD_DIM, D_MODEL).transpose(1, 0, 2, 3)
    return w4.reshape(SWA_Q, D_MODEL).astype(jnp.bfloat16)


def _route_plan(counts, n_tokens):
    n_assign = n_tokens * TOP_K
    n_rows = (n_assign + N_EXPERTS * (EXPERT_BLOCK - 1) + EXPERT_BLOCK - 1) // EXPERT_BLOCK * EXPERT_BLOCK
    padded = (counts + EXPERT_BLOCK - 1) // EXPERT_BLOCK * EXPERT_BLOCK
    pend = jnp.cumsum(padded)
    pstart = pend - padded
    n_used = pend[-1] // EXPERT_BLOCK
    i32 = jnp.int32
    return (pstart.astype(jnp.float32).reshape(N_EXPERTS, 1), (pstart // EXPERT_BLOCK).astype(i32),
            (padded // EXPERT_BLOCK).astype(i32), n_used.reshape(1).astype(i32), pend.astype(i32),
            padded.astype(i32), n_rows)


def kernel(x, w_in, b_forget, attn_sinks, rel_bias, w_proj_a, w_proj_b, w_out, ln1_g, ln1_b,
           w_router, router_bias, w_gate_e, w_up_e, w_down_e, w_gate_s, w_up_s, w_down_s, ln2_g, ln2_b):
    batch, seq_len, d = x.shape
    depth = w_in.shape[0]
    alpha = (2 * depth) ** 0.25
    t = batch * seq_len
    bf16 = jnp.bfloat16
    x2 = x.reshape(t, d)
    bias_tab = _t5_bias_table(rel_bias)
    for l in range(depth):
        w_main, bf_row = _prep_in_weights(w_in[l], b_forget[l])
        qa, ka, va, qf, kf, vf, ga, gb = _in_proj(x2, w_main, bf_row, seq_len)
        ya = _swa(qa, ka, va, attn_sinks[l].astype(jnp.float32), bias_tab, seq_len)
        yb = _fox(qf, kf, vf, batch, seq_len)
        x1, x1p = _merge(alpha, x2, ya, yb, ga, gb, _prep_proj_a(w_proj_a[l]), w_proj_b[l].astype(bf16),
                         w_out[l].astype(bf16), ln1_g[l].reshape(1, d), ln1_b[l].reshape(1, d))
        top_e, w_kt, rank, counts = _router(x1, w_router[l].T.astype(bf16),
                                            router_bias[l].reshape(N_EXPERTS, 1).astype(jnp.float32))
        pstart_col, first_blk, n_blk, n_used, pend, padded, n_rows = _route_plan(counts[:, 0], t)
        dest_flat = _plan(pstart_col, top_e, rank).T.reshape(-1)
        xs = _dispatch(dest_flat, x1p, pend, padded, n_used, n_rows)
        y = _experts(l, first_blk, n_blk, n_used, xs, w_gate_e, w_up_e, w_down_e)
        x2 = _combine(alpha, dest_flat, y, w_kt.T, x1, w_gate_s[l].astype(bf16), w_up_s[l].astype(bf16),
                      w_down_s[l].astype(bf16), ln2_g[l].reshape(1, d), ln2_b[l].reshape(1, d))
    return x2.reshape(batch, seq_len, d)
```

```python
import functools
import math

import jax
import jax.numpy as jnp
import numpy as np
from jax import lax
from jax.experimental import pallas as pl
from jax.experimental.pallas import tpu as pltpu

D_MODEL = 1024
HEAD_DIM = 64
SWA_Q_HEADS = 8
SWA_KV_HEADS = 2
SWA_WINDOW = 128
FOX_HEADS = 8
Q_BLOCK = 128
N_BUCKETS = 32
MAX_DISTANCE = 128
N_EXPERTS = 256
TOP_K = 8
N_GROUPS = 8
TOPK_GROUPS = 4
GROUP_SIZE = N_EXPERTS // N_GROUPS
D_EXPERT = 256
D_SHARED = 256
ROUTED_SCALE = 2.5
LN_EPS = 1e-5

SWA_Q = SWA_Q_HEADS * HEAD_DIM
SWA_KV = SWA_KV_HEADS * HEAD_DIM
FOX_W = FOX_HEADS * HEAD_DIM
IN_SIZES = (SWA_Q, SWA_KV, SWA_KV, FOX_W, FOX_W, FOX_W, FOX_HEADS, D_MODEL, D_MODEL)

LANES = 128
HALF_D = D_MODEL // 2
VMEM_LIMIT = 56 * 1024 * 1024
HI_MASK = 0xFFFF0000

TM_PROJ = 512
TM_MERGE = 512
TM_ROUTE = 256
TM_DISPATCH = 1024
TM_COMBINE = 512
TM_PLAN = 2048
EXPERT_BLOCK = 512
EXPERT_SUB = 512
SWA_STEP = 4
FOX_TILE = 1024
FOX_DIAG_BANDS = 2

AUG = HEAD_DIM


def _cparams(sem):
    return pltpu.CompilerParams(dimension_semantics=sem, vmem_limit_bytes=VMEM_LIMIT)


def _sigmoid(x):
    return 1.0 / (1.0 + jnp.exp(-x))


def _split3(c):
    hi = c.astype(jnp.bfloat16).astype(jnp.float32)
    r = c - hi
    mid = r.astype(jnp.bfloat16).astype(jnp.float32)
    lo = (r - mid).astype(jnp.bfloat16).astype(jnp.float32)
    return hi, mid, lo


_C_QA = 0
_C_KA = _C_QA + SWA_Q
_C_VA = _C_KA + SWA_KV
_C_QF = _C_VA + SWA_KV
_C_KF = _C_QF + FOX_W
_C_VF = _C_KF + FOX_W
_C_GA = _C_VF + FOX_W
_C_GB = _C_GA + D_MODEL
_C_ZF = _C_GB + D_MODEL
_N_PROJ = _C_ZF + LANES


def _in_proj_kernel(tiles_per_seq, x_ref, w_ref, bf_ref, tri_ref,
                    qa_ref, ka_ref, va_ref, qf_ref, kf_ref, vf_ref, ga_ref, gb_ref, carry_ref):
    i = pl.program_id(0)
    tm = x_ref.shape[0]
    xb = x_ref[...].astype(jnp.bfloat16)

    def proj(c0, n):
        return jnp.dot(xb, w_ref[:, c0:c0 + n], preferred_element_type=jnp.float32)

    scale = HEAD_DIM ** -0.5
    qa_ref[...] = (proj(_C_QA, SWA_Q) * scale).astype(jnp.bfloat16)
    ka_ref[...] = proj(_C_KA, SWA_KV).astype(jnp.bfloat16)
    va_ref[...] = proj(_C_VA, SWA_KV).astype(jnp.bfloat16)
    ga_ref[...] = proj(_C_GA, D_MODEL)
    gb_ref[...] = proj(_C_GB, D_MODEL)

    z = proj(_C_ZF, LANES) + bf_ref[...]
    log_f = jnp.minimum(z, 0.0) - jnp.log(1.0 + jnp.exp(-jnp.abs(z)))

    @pl.when(i % tiles_per_seq == 0)
    def _():
        carry_ref[...] = jnp.zeros_like(carry_ref)

    tri = tri_ref[...]
    h3 = _split3(log_f)
    c = carry_ref[...]
    for part in h3:
        c = c + jnp.dot(tri, part.astype(jnp.bfloat16), preferred_element_type=jnp.float32)
    carry_ref[...] = c[tm - 1:tm, :]

    qf = proj(_C_QF, FOX_W) * scale
    kf = proj(_C_KF, FOX_W)
    vf = proj(_C_VF, FOX_W)
    lane = lax.broadcasted_iota(jnp.int32, (tm, LANES), 1)
    one = jnp.float32(1.0)
    zero = jnp.float32(0.0)
    for h in range(FOX_HEADS):
        pair = (h // 2) * LANES
        qp = qf[:, pair:pair + LANES]
        kp = kf[:, pair:pair + LANES]
        vp = vf[:, pair:pair + LANES]
        if h % 2 == 1:
            qp = pltpu.roll(qp, HEAD_DIM, axis=1)
            kp = pltpu.roll(kp, HEAD_DIM, axis=1)
            vp = pltpu.roll(vp, HEAD_DIM, axis=1)
        ch = jnp.broadcast_to(c[:, h:h + 1], (tm, LANES))
        hi, mid, lo = _split3(ch)
        q_aug = jnp.where(lane == AUG, hi, jnp.where(lane == AUG + 1, mid, jnp.where(
            lane == AUG + 2, lo, jnp.where(lane < AUG + 6, one, zero))))
        k_aug = jnp.where(lane < AUG + 3, one, jnp.where(lane == AUG + 3, -hi, jnp.where(
            lane == AUG + 4, -mid, jnp.where(lane == AUG + 5, -lo, zero))))
        qf_ref[h] = jnp.where(lane < AUG, qp, q_aug).astype(jnp.bfloat16)
        kf_ref[h] = jnp.where(lane < AUG, kp, k_aug).astype(jnp.bfloat16)
        vf_ref[h] = jnp.where(lane < AUG, vp, jnp.where(lane == AUG, one, zero)).astype(jnp.bfloat16)


def _in_proj(x2, w_main, bf_row, seq_len):
    t = x2.shape[0]
    tm = min(TM_PROJ, seq_len)
    tri = jnp.tril(jnp.ones((tm, tm), jnp.bfloat16))
    row = lambda n: pl.BlockSpec((tm, n), lambda i: (i, 0))
    full = lambda a: pl.BlockSpec(a.shape, lambda i: (0,) * a.ndim)
    slab = pl.BlockSpec((FOX_HEADS, tm, LANES), lambda i: (0, i, 0))
    bf16, f32 = jnp.bfloat16, jnp.float32
    slab_shape = jax.ShapeDtypeStruct((FOX_HEADS, t, LANES), bf16)
    out_shape = (
        jax.ShapeDtypeStruct((t, SWA_Q), bf16), jax.ShapeDtypeStruct((t, SWA_KV), bf16),
        jax.ShapeDtypeStruct((t, SWA_KV), bf16), slab_shape, slab_shape, slab_shape,
        jax.ShapeDtypeStruct((t, D_MODEL), f32), jax.ShapeDtypeStruct((t, D_MODEL), f32))
    return pl.pallas_call(
        functools.partial(_in_proj_kernel, seq_len // tm),
        grid=(t // tm,),
        in_specs=[row(D_MODEL), full(w_main), full(bf_row), full(tri)],
        out_specs=(row(SWA_Q), row(SWA_KV), row(SWA_KV), slab, slab, slab, row(D_MODEL), row(D_MODEL)),
        out_shape=out_shape,
        scratch_shapes=[pltpu.VMEM((1, LANES), f32)],
        compiler_params=_cparams(("arbitrary",)),
        name="in_proj",
    )(x2, w_main, bf_row, tri)


def _swa_kernel(steps_per_seq, sink_ref, q_ref, kp_ref, kc_ref, vp_ref, vc_ref, bias_ref, o_ref):
    n = pl.program_id(0)
    kall = jnp.concatenate([kp_ref[...], kc_ref[...]], axis=0)
    vall = jnp.concatenate([vp_ref[...], vc_ref[...]], axis=0)
    qi = lax.broadcasted_iota(jnp.int32, (Q_BLOCK, 2 * Q_BLOCK), 0)
    kj = lax.broadcasted_iota(jnp.int32, (Q_BLOCK, 2 * Q_BLOCK), 1)
    dist = qi + Q_BLOCK - kj
    in_window = (dist >= 0) & (dist < SWA_WINDOW)
    first_key = jnp.where((n % steps_per_seq) > 0, 0, Q_BLOCK)
    lane = lax.broadcasted_iota(jnp.int32, (Q_BLOCK, LANES), 1)
    lo_half = lane < HEAD_DIM
    group = SWA_Q_HEADS // SWA_KV_HEADS
    for b in range(SWA_STEP):
        rows = slice(b * Q_BLOCK, (b + 1) * Q_BLOCK)
        kk = kall[b * Q_BLOCK:(b + 2) * Q_BLOCK]
        vv = vall[b * Q_BLOCK:(b + 2) * Q_BLOCK]
        valid = (in_window & (kj >= first_key)) if b == 0 else in_window
        for j in range(group):
            qs = q_ref[rows, j * LANES:(j + 1) * LANES]
            outs = []
            for kv in range(SWA_KV_HEADS):
                head = kv * group + j
                qm = jnp.where(lo_half if kv == 0 else ~lo_half, qs, jnp.zeros_like(qs))
                s = lax.dot_general(qm, kk, (((1,), (1,)), ((), ())), preferred_element_type=jnp.float32)
                s = jnp.where(valid, s + bias_ref[head], -jnp.inf)
                sink = sink_ref[head]
                m = jnp.maximum(jnp.max(s, axis=-1, keepdims=True), sink)
                p = jnp.exp(s - m)
                denom = jnp.sum(p, axis=-1, keepdims=True) + jnp.exp(sink - m)
                pn = (p / denom).astype(jnp.bfloat16)
                outs.append(jnp.dot(pn, vv, preferred_element_type=jnp.float32))
            o_ref[rows, j * LANES:(j + 1) * LANES] = jnp.where(lo_half, outs[0], outs[1]).astype(o_ref.dtype)


def _swa(qa, ka, va, sinks, bias_tab, seq_len):
    t = qa.shape[0]
    tm = SWA_STEP * Q_BLOCK
    assert seq_len % tm == 0
    cur = lambda n, s: (n, 0)
    prev = lambda n, s: (jnp.maximum(n * SWA_STEP - 1, 0), 0)
    grid_spec = pltpu.PrefetchScalarGridSpec(
        num_scalar_prefetch=1,
        grid=(t // tm,),
        in_specs=[pl.BlockSpec((tm, SWA_Q), cur),
                  pl.BlockSpec((Q_BLOCK, SWA_KV), prev), pl.BlockSpec((tm, SWA_KV), cur),
                  pl.BlockSpec((Q_BLOCK, SWA_KV), prev), pl.BlockSpec((tm, SWA_KV), cur),
                  pl.BlockSpec(bias_tab.shape, lambda n, s: (0, 0, 0))],
        out_specs=pl.BlockSpec((tm, SWA_Q), cur))
    return pl.pallas_call(
        functools.partial(_swa_kernel, seq_len // tm),
        grid_spec=grid_spec,
        out_shape=jax.ShapeDtypeStruct((t, SWA_Q), jnp.bfloat16),
        compiler_params=_cparams(("arbitrary",)),
        name="swa",
    )(sinks, qa, ka, ka, va, va, bias_tab)


def _t5_bias_table(rel_bias):
    i = np.arange(Q_BLOCK)[:, None]
    j = np.arange(2 * Q_BLOCK)[None, :]
    n = jnp.asarray(np.maximum(i + Q_BLOCK - j, 0))
    max_exact = N_BUCKETS // 2
    nf = jnp.maximum(n, 1).astype(jnp.float32)
    large = max_exact + (jnp.log(nf / max_exact) / math.log(MAX_DISTANCE / max_exact)
                         * (N_BUCKETS - max_exact)).astype(jnp.int32)
    large = jnp.minimum(large, N_BUCKETS - 1)
    bucket = jnp.where(n < max_exact, n, large)
    onehot = bucket[None] == jnp.arange(N_BUCKETS)[:, None, None]
    rb = rel_bias.astype(jnp.float32)
    return jnp.sum(jnp.where(onehot[:, None], rb[:, :, None, None], 0.0), axis=0)


def _fox_kernel(q_ref, k_ref, v_ref, o_ref):
    qi = pl.program_id(2)
    tq = q_ref.shape[1]
    qs = (q_ref[0], q_ref[1])

    def update(s, m, acc, v):
        m_new = jnp.maximum(m, jnp.max(s, axis=-1, keepdims=True))
        p = jnp.exp(s - m_new).astype(jnp.bfloat16)
        return m_new, jnp.exp(m - m_new) * acc + jnp.dot(p, v, preferred_element_type=jnp.float32)

    def scores(q, k):
        return lax.dot_general(q, k, (((1,), (1,)), ((), ())), preferred_element_type=jnp.float32)

    def step(j, carry):
        start = pl.multiple_of(j * tq, tq)
        return tuple(update(scores(qs[hh], k_ref[hh, pl.ds(start, tq), :]), *carry[hh],
                            v_ref[hh, pl.ds(start, tq), :]) for hh in range(2))

    def diagonal(carry):
        start = pl.multiple_of(qi * tq, tq)
        band = tq // FOX_DIAG_BANDS
        new = []
        for hh in range(2):
            m, acc = carry[hh]
            parts = []
            for r in range(FOX_DIAG_BANDS):
                lo, hi = r * band, (r + 1) * band
                s = scores(qs[hh][lo:hi], k_ref[hh, pl.ds(start, hi), :])
                row = lax.broadcasted_iota(jnp.int32, (band, hi), 0) + lo
                col = lax.broadcasted_iota(jnp.int32, (band, hi), 1)
                s = jnp.where(col <= row, s, -jnp.inf)
                parts.append(update(s, m[lo:hi], acc[lo:hi], v_ref[hh, pl.ds(start, hi), :]))
            new.append((jnp.concatenate([p[0] for p in parts], axis=0),
                        jnp.concatenate([p[1] for p in parts], axis=0)))
        return tuple(new)

    init1 = (jnp.full((tq, 1), -jnp.inf, jnp.float32), jnp.zeros((tq, LANES), jnp.float32))
    carry = diagonal(lax.fori_loop(0, qi, step, (init1, init1)))
    outs = [acc / acc[:, AUG:AUG + 1] for _, acc in carry]
    lane = lax.broadcasted_iota(jnp.int32, (tq, LANES), 1)
    o_ref[...] = jnp.where(lane < HEAD_DIM, outs[0], pltpu.roll(outs[1], HEAD_DIM, axis=1)).astype(o_ref.dtype)


def _fox(qf, kf, vf, batch, seq_len):
    t = qf.shape[1]
    tq = min(FOX_TILE, seq_len)
    assert seq_len % tq == 0 and tq % FOX_DIAG_BANDS == 0
    nq = seq_len // tq
    pairs = FOX_HEADS // 2
    seq = pl.BlockSpec((2, seq_len, LANES), lambda b, hp, i: (hp, b, 0))
    return pl.pallas_call(
        _fox_kernel,
        grid=(batch, pairs, nq),
        in_specs=[pl.BlockSpec((2, tq, LANES), lambda b, hp, i: (hp, b * nq + i, 0)), seq, seq],
        out_specs=pl.BlockSpec((tq, LANES), lambda b, hp, i: (b * nq + i, hp)),
        out_shape=jax.ShapeDtypeStruct((t, FOX_W), jnp.bfloat16),
        compiler_params=_cparams(("arbitrary", "arbitrary", "arbitrary")),
        name="fox",
    )(qf, kf, vf)


def _layer_norm(h, g, b):
    mu = jnp.mean(h, axis=-1, keepdims=True)
    d = h - mu
    var = jnp.mean(d * d, axis=-1, keepdims=True)
    return d * lax.rsqrt(var + LN_EPS) * g + b


def _pack_rows(ref, x):
    bits = pltpu.bitcast(x.astype(jnp.bfloat16).astype(jnp.float32), jnp.uint32)
    ref[...] = (bits[:, :HALF_D] >> 16) | (bits[:, HALF_D:] & jnp.uint32(HI_MASK))


def _unpack_rows(ref, first, rows):
    w = ref[pl.ds(pl.multiple_of(first, 8), rows), :]
    return jnp.concatenate([pltpu.bitcast(w << 16, jnp.float32),
                            pltpu.bitcast(w & jnp.uint32(HI_MASK), jnp.float32)], axis=1)


def _packed(ref, first, rows=1):
    if rows % 8 == 0:
        first = pl.multiple_of(first, 8)
    return ref.at[pl.ds(first, rows), :]


def _packed_spec(rows, index_map):
    return pl.BlockSpec((rows, HALF_D), index_map)


def _packed_shape(rows):
    return jax.ShapeDtypeStruct((rows, HALF_D), jnp.uint32)


def _merge_kernel(alpha, x_ref, ya_ref, yb_ref, ga_ref, gb_ref, wa_ref, wb_ref, wo_ref, g_ref, b_ref,
                  x1_ref, x1p_ref):
    pa = jnp.dot(ya_ref[...], wa_ref[...], preferred_element_type=jnp.float32)
    pb = jnp.dot(yb_ref[...], wb_ref[...], preferred_element_type=jnp.float32)
    mix = _sigmoid(ga_ref[...]) * pa + _sigmoid(gb_ref[...]) * pb
    out = jnp.dot(mix.astype(jnp.bfloat16), wo_ref[...], preferred_element_type=jnp.float32)
    x1 = _layer_norm(alpha * x_ref[...] + out, g_ref[...], b_ref[...])
    x1_ref[...] = x1
    _pack_rows(x1p_ref, x1)


def _merge(alpha, x2, ya, yb, ga, gb, wa, wb, wo, g, b):
    t = x2.shape[0]
    tm = TM_MERGE
    row = lambda n: pl.BlockSpec((tm, n), lambda i: (i, 0))
    full = lambda a: pl.BlockSpec(a.shape, lambda i: (0,) * a.ndim)
    return pl.pallas_call(
        functools.partial(_merge_kernel, alpha),
        grid=(t // tm,),
        in_specs=[row(D_MODEL), row(SWA_Q), row(FOX_W), row(D_MODEL), row(D_MODEL),
                  full(wa), full(wb), full(wo), full(g), full(b)],
        out_specs=(row(D_MODEL), _packed_spec(tm, lambda i: (i, 0))),
        out_shape=(jax.ShapeDtypeStruct((t, D_MODEL), jnp.float32), _packed_shape(t)),
        compiler_params=_cparams(("arbitrary",)),
        name="merge",
    )(x2, ya, yb, ga, gb, wa, wb, wo, g, b)


def _first_argmax(v, idx):
    m = jnp.max(v, axis=0, keepdims=True)
    first = jnp.min(jnp.where(v == m, idx, v.shape[0]), axis=0, keepdims=True)
    return m, first


def _router_kernel(x_ref, wr_ref, rb_ref, tri_ref, e_ref, w_ref, r_ref, cnt_ref, carry_ref):
    i = pl.program_id(0)
    tm = x_ref.shape[0]

    @pl.when(i == 0)
    def _():
        carry_ref[...] = jnp.zeros_like(carry_ref)

    xb = x_ref[...].astype(jnp.bfloat16)
    logits = lax.dot_general(wr_ref[...], xb, (((1,), (1,)), ((), ())),
                             preferred_element_type=jnp.float32)
    scores = _sigmoid(logits)
    biased = scores + rb_ref[...]

    in_group = lax.broadcasted_iota(jnp.int32, (GROUP_SIZE, tm), 0)
    grows = []
    for g in range(N_GROUPS):
        blk = biased[g * GROUP_SIZE:(g + 1) * GROUP_SIZE, :]
        m1, f1 = _first_argmax(blk, in_group)
        m2 = jnp.max(jnp.where(in_group == f1, -jnp.inf, blk), axis=0, keepdims=True)
        grows.append(m1 + m2)
    gscore = jnp.concatenate(grows, axis=0)

    gidx = lax.broadcasted_iota(jnp.int32, (N_GROUPS, tm), 0)
    gwork = gscore
    gsel = jnp.zeros((N_GROUPS, tm), jnp.float32)
    for _ in range(TOPK_GROUPS):
        _, first = _first_argmax(gwork, gidx)
        hit = gidx == first
        gsel = jnp.where(hit, 1.0, gsel)
        gwork = jnp.where(hit, -jnp.inf, gwork)

    eidx = lax.broadcasted_iota(jnp.int32, (N_EXPERTS, tm), 0)
    work = jnp.concatenate(
        [jnp.where(jnp.broadcast_to(gsel[g:g + 1, :], (GROUP_SIZE, tm)) > 0.0,
                   biased[g * GROUP_SIZE:(g + 1) * GROUP_SIZE, :], -jnp.inf) for g in range(N_GROUPS)], axis=0)
    picks = []
    sel = jnp.zeros((N_EXPERTS, tm), jnp.bool_)
    for _ in range(TOP_K):
        _, first = _first_argmax(work, eidx)
        hit = eidx == first
        picks.append((first, hit))
        sel = sel | hit
        work = jnp.where(hit, -jnp.inf, work)

    wsum = jnp.sum(jnp.where(sel, scores, 0.0), axis=0, keepdims=True)
    sel_b = jnp.where(sel, 1.0, 0.0).astype(jnp.bfloat16)
    before = carry_ref[...] + jnp.dot(sel_b, tri_ref[...], preferred_element_type=jnp.float32)
    e_rows, w_rows, r_rows = [], [], []
    for first, hit in picks:
        e_rows.append(first)
        sc = jnp.sum(jnp.where(hit, scores, 0.0), axis=0, keepdims=True)
        w_rows.append(sc / wsum * ROUTED_SCALE)
        r_rows.append(jnp.sum(jnp.where(hit, before, 0.0), axis=0, keepdims=True).astype(jnp.int32))
    e_ref[...] = jnp.concatenate(e_rows, axis=0)
    w_ref[...] = jnp.concatenate(w_rows, axis=0)
    r_ref[...] = jnp.concatenate(r_rows, axis=0)
    total = carry_ref[...] + jnp.sum(sel_b.astype(jnp.float32), axis=1, keepdims=True)
    carry_ref[...] = total
    cnt_ref[...] = jnp.broadcast_to(total, cnt_ref.shape).astype(jnp.int32)


def _router(x1, wr_t, rb_col):
    t = x1.shape[0]
    tm = TM_ROUTE
    tri = jnp.triu(jnp.ones((tm, tm), jnp.bfloat16), k=1)
    full = lambda a: pl.BlockSpec(a.shape, lambda i: (0,) * a.ndim)
    kt = pl.BlockSpec((TOP_K, tm), lambda i: (0, i))
    return pl.pallas_call(
        _router_kernel,
        grid=(t // tm,),
        in_specs=[pl.BlockSpec((tm, D_MODEL), lambda i: (i, 0)), full(wr_t), full(rb_col), full(tri)],
        out_specs=(kt, kt, kt, pl.BlockSpec((N_EXPERTS, LANES), lambda i: (0, 0))),
        out_shape=(jax.ShapeDtypeStruct((TOP_K, t), jnp.int32), jax.ShapeDtypeStruct((TOP_K, t), jnp.float32),
                   jax.ShapeDtypeStruct((TOP_K, t), jnp.int32),
                   jax.ShapeDtypeStruct((N_EXPERTS, LANES), jnp.int32)),
        scratch_shapes=[pltpu.VMEM((N_EXPERTS, 1), jnp.float32)],
        compiler_params=_cparams(("arbitrary",)),
        name="router",
    )(x1, wr_t, rb_col, tri)


def _plan_kernel(ps_ref, e_ref, r_ref, d_ref):
    tm = e_ref.shape[1]
    eidx = lax.broadcasted_iota(jnp.int32, (N_EXPERTS, tm), 0)
    rows = []
    for k in range(TOP_K):
        hit = eidx == e_ref[k:k + 1, :]
        base = jnp.sum(jnp.where(hit, ps_ref[...], 0.0), axis=0, keepdims=True)
        rows.append(base.astype(jnp.int32) + r_ref[k:k + 1, :])
    d_ref[...] = jnp.concatenate(rows, axis=0)


def _plan(pstart_col, top_e, rank):
    t = top_e.shape[1]
    tm = min(TM_PLAN, t)
    kt = pl.BlockSpec((TOP_K, tm), lambda i: (0, i))
    return pl.pallas_call(
        _plan_kernel,
        grid=(t // tm,),
        in_specs=[pl.BlockSpec(pstart_col.shape, lambda i: (0, 0)), kt, kt],
        out_specs=kt,
        out_shape=jax.ShapeDtypeStruct((TOP_K, t), jnp.int32),
        compiler_params=_cparams(("arbitrary",)),
        name="plan",
    )(pstart_col, top_e, rank)


def _dispatch_kernel(pend_ref, padded_ref, nu_ref, dest_hbm, x_ref, xs_hbm, idx_smem, zero_vmem, sem, zsem):
    i = pl.program_id(0)
    n_blocks = xs_hbm.shape[0] // EXPERT_BLOCK

    def zero_copy(first_row):
        return pltpu.make_async_copy(zero_vmem, _packed(xs_hbm, first_row, EXPERT_BLOCK), zsem)

    @pl.when(i == 0)
    def _():
        zero_vmem[...] = jnp.zeros_like(zero_vmem)

        def fill(e, c):
            @pl.when(padded_ref[e] > 0)
            def _():
                zero_copy(pend_ref[e] - EXPERT_BLOCK).start()
            return c
        lax.fori_loop(0, N_EXPERTS, fill, 0)

        def fill_tail(b, c):
            zero_copy(b * EXPERT_BLOCK).start()
            return c
        lax.fori_loop(nu_ref[0], n_blocks, fill_tail, 0)

        def drain(e, c):
            @pl.when(padded_ref[e] > 0)
            def _():
                zero_copy(0).wait()
            return c
        lax.fori_loop(0, N_EXPERTS, drain, 0)

        def drain_tail(b, c):
            zero_copy(0).wait()
            return c
        lax.fori_loop(nu_ref[0], n_blocks, drain_tail, 0)

    n_idx = TM_DISPATCH * TOP_K
    pltpu.sync_copy(dest_hbm.at[pl.ds(i * n_idx, n_idx)], idx_smem)

    def issue(t, c):
        for k in range(TOP_K):
            pltpu.make_async_copy(_packed(x_ref, t), _packed(xs_hbm, idx_smem[t * TOP_K + k]),
                                  sem).start(priority=k % 2)
        return c
    lax.fori_loop(0, TM_DISPATCH, issue, 0)

    for k in range(TOP_K):
        pltpu.make_async_copy(x_ref, _packed(xs_hbm, 0, TM_DISPATCH), sem).wait()


def _dispatch(dest_flat, x1p, pend, padded, n_used, n_rows):
    t = x1p.shape[0]
    grid_spec = pltpu.PrefetchScalarGridSpec(
        num_scalar_prefetch=3,
        grid=(t // TM_DISPATCH,),
        in_specs=[pl.BlockSpec(memory_space=pl.ANY), _packed_spec(TM_DISPATCH, lambda i, pe, pa, nu: (i, 0))],
        out_specs=pl.BlockSpec(memory_space=pl.ANY),
        scratch_shapes=[pltpu.SMEM((TM_DISPATCH * TOP_K,), jnp.int32),
                        pltpu.VMEM((EXPERT_BLOCK, HALF_D), jnp.uint32),
                        pltpu.SemaphoreType.DMA, pltpu.SemaphoreType.DMA])
    return pl.pallas_call(
        _dispatch_kernel,
        grid_spec=grid_spec,
        out_shape=_packed_shape(n_rows),
        compiler_params=_cparams(("arbitrary",)),
        name="dispatch",
    )(pend, padded, n_used, dest_flat, x1p)


def _experts_kernel(first_ref, nblk_ref, nu_ref, xs_hbm, wg_ref, wu_ref, wd_ref, y_hbm,
                    xbuf, ybuf, wg_b, wu_b, wd_b, in_sem, out_sem):
    nc = EXPERT_BLOCK // EXPERT_SUB
    e = pl.program_id(0)
    n_used = nu_ref[0]
    n_blocks = y_hbm.shape[0] // EXPERT_BLOCK
    first = first_ref[e]

    def slot_rows(ref, slot):
        return _packed(ref, slot * EXPERT_BLOCK, EXPERT_BLOCK)

    def fetch(b):
        return pltpu.make_async_copy(_packed(xs_hbm, b * EXPERT_BLOCK, EXPERT_BLOCK), slot_rows(xbuf, b % 2),
                                     in_sem.at[b % 2])

    def write_back(b):
        return pltpu.make_async_copy(slot_rows(ybuf, b % 2), _packed(y_hbm, b * EXPERT_BLOCK, EXPERT_BLOCK),
                                     out_sem.at[b % 2])

    @pl.when(e == 0)
    def _():
        fetch(0).start()

    @pl.when(nblk_ref[e] > 0)
    def _():
        wg_b[...] = wg_ref[...].astype(jnp.bfloat16)
        wu_b[...] = wu_ref[...].astype(jnp.bfloat16)
        wd_b[...] = wd_ref[...].astype(jnp.bfloat16)

    def block(j, carry):
        b = first + j
        slot = b % 2

        @pl.when(b + 1 < n_used)
        def _():
            fetch(b + 1).start()

        fetch(b).wait()

        @pl.when(b >= 2)
        def _():
            write_back(b - 2).wait()

        base = slot * EXPERT_BLOCK
        xbs = [_unpack_rows(xbuf, base + c * EXPERT_SUB, EXPERT_SUB).astype(jnp.bfloat16) for c in range(nc)]
        gus = [(jnp.dot(xb, wg_b[...], preferred_element_type=jnp.float32),
                jnp.dot(xb, wu_b[...], preferred_element_type=jnp.float32)) for xb in xbs]
        ys = []
        for g, u in gus:
            h = (g * _sigmoid(g) * u).astype(jnp.bfloat16)
            ys.append(jnp.dot(h, wd_b[...], preferred_element_type=jnp.float32))
        for c, y in enumerate(ys):
            _pack_rows(_packed(ybuf, base + c * EXPERT_SUB, EXPERT_SUB), y)
        write_back(b).start()
        return carry

    lax.fori_loop(0, nblk_ref[e], block, 0)

    @pl.when(e == pl.num_programs(0) - 1)
    def _():
        @pl.when(n_used >= 2)
        def _():
            write_back(n_used - 2).wait()
        write_back(n_used - 1).wait()

        xbuf[...] = jnp.zeros_like(xbuf)

        def zero_copy(b):
            return pltpu.make_async_copy(slot_rows(xbuf, 0), _packed(y_hbm, b * EXPERT_BLOCK, EXPERT_BLOCK),
                                         out_sem.at[0])

        def fill(b, c):
            zero_copy(b).start()
            return c
        lax.fori_loop(n_used, n_blocks, fill, 0)

        def drain(b, c):
            zero_copy(b).wait()
            return c
        lax.fori_loop(n_used, n_blocks, drain, 0)


def _experts(layer, first_blk, n_blk, n_used, xs, w_gate_e, w_up_e, w_down_e):
    wsel = lambda e, fb, nb, nu: (layer, e, 0, 0)
    slots = pltpu.VMEM((2 * EXPERT_BLOCK, HALF_D), jnp.uint32)
    grid_spec = pltpu.PrefetchScalarGridSpec(
        num_scalar_prefetch=3,
        grid=(N_EXPERTS,),
        in_specs=[pl.BlockSpec(memory_space=pl.ANY),
                  pl.BlockSpec((None, None, D_MODEL, D_EXPERT), wsel),
                  pl.BlockSpec((None, None, D_MODEL, D_EXPERT), wsel),
                  pl.BlockSpec((None, None, D_EXPERT, D_MODEL), wsel)],
        out_specs=pl.BlockSpec(memory_space=pl.ANY),
        scratch_shapes=[slots, slots,
                        pltpu.VMEM((D_MODEL, D_EXPERT), jnp.bfloat16), pltpu.VMEM((D_MODEL, D_EXPERT), jnp.bfloat16),
                        pltpu.VMEM((D_EXPERT, D_MODEL), jnp.bfloat16),
                        pltpu.SemaphoreType.DMA((2,)), pltpu.SemaphoreType.DMA((2,))])
    return pl.pallas_call(
        _experts_kernel,
        grid_spec=grid_spec,
        out_shape=jax.ShapeDtypeStruct(xs.shape, jnp.uint32),
        compiler_params=_cparams(("arbitrary",)),
        name="experts",
    )(first_blk, n_blk, n_used, xs, w_gate_e, w_up_e, w_down_e)


def _combine_kernel(alpha, dest_hbm, y_hbm, w_ref, x1_ref, wg_ref, wu_ref, wd_ref, g_ref, b_ref,
                    o_ref, idx_smem0, idx_smem1, buf, sem):
    i = pl.program_id(0)
    n = pl.num_programs(0)
    n_idx = TM_COMBINE * TOP_K

    def slab(slot, k):
        return (slot * TOP_K + k) * TM_COMBINE

    def start_gather(step, slot):
        idx_smem = (idx_smem0, idx_smem1)[slot]
        pltpu.sync_copy(dest_hbm.at[pl.ds(step * n_idx, n_idx)], idx_smem)

        def issue(t, c):
            for k in range(TOP_K):
                pltpu.make_async_copy(_packed(y_hbm, idx_smem[t * TOP_K + k]),
                                      _packed(buf, slab(slot, k) + t), sem.at[slot]).start(priority=k % 2)
            return c
        lax.fori_loop(0, TM_COMBINE, issue, 0)

    @pl.when(i == 0)
    def _():
        start_gather(0, 0)

    def run(slot):
        @pl.when(i + 1 < n)
        def _():
            start_gather(i + 1, 1 - slot)

        for k in range(TOP_K):
            pltpu.make_async_copy(_packed(y_hbm, 0, TM_COMBINE), _packed(buf, slab(slot, k), TM_COMBINE),
                                  sem.at[slot]).wait()

        w = w_ref[...]
        routed = jnp.zeros((TM_COMBINE, D_MODEL), jnp.float32)
        for k in range(TOP_K):
            routed = routed + w[:, k:k + 1] * _unpack_rows(buf, slab(slot, k), TM_COMBINE)

        x1 = x1_ref[...]
        xb = x1.astype(jnp.bfloat16)
        gs = jnp.dot(xb, wg_ref[...], preferred_element_type=jnp.float32)
        us = jnp.dot(xb, wu_ref[...], preferred_element_type=jnp.float32)
        hs = gs * _sigmoid(gs) * us
        shared = jnp.dot(hs.astype(jnp.bfloat16), wd_ref[...], preferred_element_type=jnp.float32)
        o_ref[...] = _layer_norm(alpha * x1 + (shared + routed), g_ref[...], b_ref[...])

    for slot in range(2):
        pl.when(i % 2 == slot)(functools.partial(run, slot))


def _combine(alpha, dest_flat, y, w_tk, x1, wg, wu, wd, g, b):
    t = x1.shape[0]
    tm = TM_COMBINE
    full = lambda a: pl.BlockSpec(a.shape, lambda i: (0,) * a.ndim)
    return pl.pallas_call(
        functools.partial(_combine_kernel, alpha),
        grid=(t // tm,),
        in_specs=[pl.BlockSpec(memory_space=pl.ANY), pl.BlockSpec(memory_space=pl.ANY),
                  pl.BlockSpec((tm, TOP_K), lambda i: (i, 0)), pl.BlockSpec((tm, D_MODEL), lambda i: (i, 0)),
                  full(wg), full(wu), full(wd), full(g), full(b)],
        out_specs=pl.BlockSpec((tm, D_MODEL), lambda i: (i, 0)),
        out_shape=jax.ShapeDtypeStruct((t, D_MODEL), jnp.float32),
        scratch_shapes=[pltpu.SMEM((tm * TOP_K,), jnp.int32), pltpu.SMEM((tm * TOP_K,), jnp.int32),
                        pltpu.VMEM((2 * TOP_K * tm, HALF_D), jnp.uint32),
                        pltpu.SemaphoreType.DMA((2,))],
        compiler_params=_cparams(("arbitrary",)),
        name="combine",
    )(dest_flat, y, w_tk, x1, wg, wu, wd, g, b)


def _prep_in_weights(w, b_forget):
    o = np.cumsum((0,) + IN_SIZES)
    group = SWA_Q_HEADS // SWA_KV_HEADS
    qa = w[:, o[0]:o[1]].reshape(D_MODEL, SWA_KV_HEADS, group, HEAD_DIM)
    qa = qa.transpose(0, 2, 1, 3).reshape(D_MODEL, SWA_Q)
    zf = jnp.pad(w[:, o[6]:o[7]], ((0, 0), (0, LANES - FOX_HEADS)))
    w_main = jnp.concatenate([qa, w[:, o[1]:o[6]], w[:, o[7]:o[9]], zf], axis=1).astype(jnp.bfloat16)
    bf_row = jnp.pad(b_forget, (0, LANES - FOX_HEADS)).reshape(1, LANES).astype(jnp.float32)
    return w_main, bf_row


def _prep_proj_a(w):
    group = SWA_Q_HEADS // SWA_KV_HEADS
    w4 = w.reshape(SWA_KV_HEADS, group, HEAD_DIM, D_MODEL).transpose(1, 0, 2, 3)
    return w4.reshape(SWA_Q, D_MODEL).astype(jnp.bfloat16)


def _route_plan(counts, n_tokens):
    n_assign = n_tokens * TOP_K
    n_rows = (n_assign + N_EXPERTS * (EXPERT_BLOCK - 1) + EXPERT_BLOCK - 1) // EXPERT_BLOCK * EXPERT_BLOCK
    padded = (counts + EXPERT_BLOCK - 1) // EXPERT_BLOCK * EXPERT_BLOCK
    pend = jnp.cumsum(padded)
    pstart = pend - padded
    n_used = pend[-1] // EXPERT_BLOCK
    i32 = jnp.int32
    return (pstart.astype(jnp.float32).reshape(N_EXPERTS, 1), (pstart // EXPERT_BLOCK).astype(i32),
            (padded // EXPERT_BLOCK).astype(i32), n_used.reshape(1).astype(i32), pend.astype(i32),
            padded.astype(i32), n_rows)


def kernel(x, w_in, b_forget, attn_sinks, rel_bias, w_proj_a, w_proj_b, w_out, ln1_g, ln1_b,
           w_router, router_bias, w_gate_e, w_up_e, w_down_e, w_gate_s, w_up_s, w_down_s, ln2_g, ln2_b):
    batch, seq_len, d = x.shape
    depth = w_in.shape[0]
    alpha = (2 * depth) ** 0.25
    t = batch * seq_len
    bf16 = jnp.bfloat16
    x2 = x.reshape(t, d)
    bias_tab = _t5_bias_table(rel_bias)
    for l in range(depth):
        w_main, bf_row = _prep_in_weights(w_in[l], b_forget[l])
        qa, ka, va, qf, kf, vf, ga, gb = _in_proj(x2, w_main, bf_row, seq_len)
        ya = _swa(qa, ka, va, attn_sinks[l].astype(jnp.float32), bias_tab, seq_len)
        yb = _fox(qf, kf, vf, batch, seq_len)
        x1, x1p = _merge(alpha, x2, ya, yb, ga, gb, _prep_proj_a(w_proj_a[l]), w_proj_b[l].astype(bf16),
                         w_out[l].astype(bf16), ln1_g[l].reshape(1, d), ln1_b[l].reshape(1, d))
        top_e, w_kt, rank, counts = _router(x1, w_router[l].T.astype(bf16),
                                            router_bias[l].reshape(N_EXPERTS, 1).astype(jnp.float32))
        pstart_col, first_blk, n_blk, n_used, pend, padded, n_rows = _route_plan(counts[:, 0], t)
        dest_flat = _plan(pstart_col, top_e, rank).T.reshape(-1)
        xs = _dispatch(dest_flat, x1p, pend, padded, n_used, n_rows)
        y = _experts(l, first_blk, n_blk, n_used, xs, w_gate_e, w_up_e, w_down_e)
        x2 = _combine(alpha, dest_flat, y, w_kt.T, x1, w_gate_s[l].astype(bf16), w_up_s[l].astype(bf16),
                      w_down_s[l].astype(bf16), ln2_g[l].reshape(1, d), ln2_b[l].reshape(1, d))
    return x2.reshape(batch, seq_len, d)
```

```python
import functools
import math

import jax
import jax.numpy as jnp
import numpy as np
from jax import lax
from jax.experimental import pallas as pl
from jax.experimental.pallas import tpu as pltpu

D_MODEL = 1024
HEAD_DIM = 64
SWA_Q_HEADS = 8
SWA_KV_HEADS = 2
SWA_WINDOW = 128
FOX_HEADS = 8
Q_BLOCK = 128
N_BUCKETS = 32
MAX_DISTANCE = 128
N_EXPERTS = 256
TOP_K = 8
N_GROUPS = 8
TOPK_GROUPS = 4
GROUP_SIZE = N_EXPERTS // N_GROUPS
D_EXPERT = 256
D_SHARED = 256
ROUTED_SCALE = 2.5
LN_EPS = 1e-5

SWA_Q = SWA_Q_HEADS * HEAD_DIM
SWA_KV = SWA_KV_HEADS * HEAD_DIM
FOX_W = FOX_HEADS * HEAD_DIM
IN_SIZES = (SWA_Q, SWA_KV, SWA_KV, FOX_W, FOX_W, FOX_W, FOX_HEADS, D_MODEL, D_MODEL)

LANES = 128
HALF_D = D_MODEL // 2
ROW_WORDS = HALF_D // LANES
VMEM_LIMIT = 56 * 1024 * 1024
HI_MASK = 0xFFFF0000

TM_PROJ = 512
TM_MERGE = 512
TM_ROUTE = 256
TM_DISPATCH = 1024
TM_COMBINE = 512
TM_PLAN = 2048
EXPERT_BLOCK = 512
EXPERT_SUB = 512
ROW_DMA_PRIORITY = 1
SWA_STEP = 4
FOX_TILE = 1024
FOX_DIAG_BANDS = 4

AUG = HEAD_DIM


def _cparams(sem):
    return pltpu.CompilerParams(dimension_semantics=sem, vmem_limit_bytes=VMEM_LIMIT)


def _sigmoid(x):
    return 1.0 / (1.0 + jnp.exp(-x))


def _split3(c):
    hi = c.astype(jnp.bfloat16).astype(jnp.float32)
    r = c - hi
    mid = r.astype(jnp.bfloat16).astype(jnp.float32)
    lo = (r - mid).astype(jnp.bfloat16).astype(jnp.float32)
    return hi, mid, lo


_C_QA = 0
_C_KA = _C_QA + SWA_Q
_C_VA = _C_KA + SWA_KV
_C_QF = _C_VA + SWA_KV
_C_KF = _C_QF + FOX_W
_C_VF = _C_KF + FOX_W
_C_GA = _C_VF + FOX_W
_C_GB = _C_GA + D_MODEL
_C_ZF = _C_GB + D_MODEL
_N_PROJ = _C_ZF + LANES


def _in_proj_kernel(tiles_per_seq, x_ref, w_ref, bf_ref, tri_ref,
                    qa_ref, ka_ref, va_ref, qf_ref, kf_ref, vf_ref, ga_ref, gb_ref, carry_ref):
    i = pl.program_id(0)
    tm = x_ref.shape[0]
    xb = x_ref[...].astype(jnp.bfloat16)

    def proj(c0, n):
        return jnp.dot(xb, w_ref[:, c0:c0 + n], preferred_element_type=jnp.float32)

    scale = HEAD_DIM ** -0.5
    qa_ref[...] = (proj(_C_QA, SWA_Q) * scale).astype(jnp.bfloat16)
    ka_ref[...] = proj(_C_KA, SWA_KV).astype(jnp.bfloat16)
    va_ref[...] = proj(_C_VA, SWA_KV).astype(jnp.bfloat16)
    ga_ref[...] = proj(_C_GA, D_MODEL)
    gb_ref[...] = proj(_C_GB, D_MODEL)

    z = proj(_C_ZF, LANES) + bf_ref[...]
    log_f = jnp.minimum(z, 0.0) - jnp.log(1.0 + jnp.exp(-jnp.abs(z)))

    @pl.when(i % tiles_per_seq == 0)
    def _():
        carry_ref[...] = jnp.zeros_like(carry_ref)

    tri = tri_ref[...]
    h3 = _split3(log_f)
    c = carry_ref[...]
    for part in h3:
        c = c + jnp.dot(tri, part.astype(jnp.bfloat16), preferred_element_type=jnp.float32)
    carry_ref[...] = c[tm - 1:tm, :]

    qf = proj(_C_QF, FOX_W) * scale
    kf = proj(_C_KF, FOX_W)
    vf = proj(_C_VF, FOX_W)
    lane = lax.broadcasted_iota(jnp.int32, (tm, LANES), 1)
    one = jnp.float32(1.0)
    zero = jnp.float32(0.0)
    for h in range(FOX_HEADS):
        pair = (h // 2) * LANES
        qp = qf[:, pair:pair + LANES]
        kp = kf[:, pair:pair + LANES]
        vp = vf[:, pair:pair + LANES]
        if h % 2 == 1:
            qp = pltpu.roll(qp, HEAD_DIM, axis=1)
            kp = pltpu.roll(kp, HEAD_DIM, axis=1)
            vp = pltpu.roll(vp, HEAD_DIM, axis=1)
        ch = jnp.broadcast_to(c[:, h:h + 1], (tm, LANES))
        hi, mid, lo = _split3(ch)
        q_aug = jnp.where(lane == AUG, hi, jnp.where(lane == AUG + 1, mid, jnp.where(
            lane == AUG + 2, lo, jnp.where(lane < AUG + 6, one, zero))))
        k_aug = jnp.where(lane < AUG + 3, one, jnp.where(lane == AUG + 3, -hi, jnp.where(
            lane == AUG + 4, -mid, jnp.where(lane == AUG + 5, -lo, zero))))
        qf_ref[h] = jnp.where(lane < AUG, qp, q_aug).astype(jnp.bfloat16)
        kf_ref[h] = jnp.where(lane < AUG, kp, k_aug).astype(jnp.bfloat16)
        vf_ref[h] = jnp.where(lane < AUG, vp, jnp.where(lane == AUG, one, zero)).astype(jnp.bfloat16)


def _in_proj(x2, w_main, bf_row, seq_len):
    t = x2.shape[0]
    tm = min(TM_PROJ, seq_len)
    tri = jnp.tril(jnp.ones((tm, tm), jnp.bfloat16))
    row = lambda n: pl.BlockSpec((tm, n), lambda i: (i, 0))
    full = lambda a: pl.BlockSpec(a.shape, lambda i: (0,) * a.ndim)
    slab = pl.BlockSpec((FOX_HEADS, tm, LANES), lambda i: (0, i, 0))
    bf16, f32 = jnp.bfloat16, jnp.float32
    slab_shape = jax.ShapeDtypeStruct((FOX_HEADS, t, LANES), bf16)
    out_shape = (
        jax.ShapeDtypeStruct((t, SWA_Q), bf16), jax.ShapeDtypeStruct((t, SWA_KV), bf16),
        jax.ShapeDtypeStruct((t, SWA_KV), bf16), slab_shape, slab_shape, slab_shape,
        jax.ShapeDtypeStruct((t, D_MODEL), f32), jax.ShapeDtypeStruct((t, D_MODEL), f32))
    return pl.pallas_call(
        functools.partial(_in_proj_kernel, seq_len // tm),
        grid=(t // tm,),
        in_specs=[row(D_MODEL), full(w_main), full(bf_row), full(tri)],
        out_specs=(row(SWA_Q), row(SWA_KV), row(SWA_KV), slab, slab, slab, row(D_MODEL), row(D_MODEL)),
        out_shape=out_shape,
        scratch_shapes=[pltpu.VMEM((1, LANES), f32)],
        compiler_params=_cparams(("arbitrary",)),
        name="in_proj",
    )(x2, w_main, bf_row, tri)


def _swa_kernel(steps_per_seq, sink_ref, q_ref, kp_ref, kc_ref, vp_ref, vc_ref, bias_ref, o_ref):
    n = pl.program_id(0)
    kall = jnp.concatenate([kp_ref[...], kc_ref[...]], axis=0)
    vall = jnp.concatenate([vp_ref[...], vc_ref[...]], axis=0)
    qi = lax.broadcasted_iota(jnp.int32, (Q_BLOCK, 2 * Q_BLOCK), 0)
    kj = lax.broadcasted_iota(jnp.int32, (Q_BLOCK, 2 * Q_BLOCK), 1)
    dist = qi + Q_BLOCK - kj
    in_window = (dist >= 0) & (dist < SWA_WINDOW)
    first_key = jnp.where((n % steps_per_seq) > 0, 0, Q_BLOCK)
    lane = lax.broadcasted_iota(jnp.int32, (Q_BLOCK, LANES), 1)
    lo_half = lane < HEAD_DIM
    group = SWA_Q_HEADS // SWA_KV_HEADS
    for b in range(SWA_STEP):
        rows = slice(b * Q_BLOCK, (b + 1) * Q_BLOCK)
        kk = kall[b * Q_BLOCK:(b + 2) * Q_BLOCK]
        vv = vall[b * Q_BLOCK:(b + 2) * Q_BLOCK]
        valid = (in_window & (kj >= first_key)) if b == 0 else in_window
        for j in range(group):
            qs = q_ref[rows, j * LANES:(j + 1) * LANES]
            outs = []
            for kv in range(SWA_KV_HEADS):
                head = kv * group + j
                qm = jnp.where(lo_half if kv == 0 else ~lo_half, qs, jnp.zeros_like(qs))
                s = lax.dot_general(qm, kk, (((1,), (1,)), ((), ())), preferred_element_type=jnp.float32)
                s = jnp.where(valid, s + bias_ref[head], -jnp.inf)
                sink = sink_ref[head]
                m = jnp.maximum(jnp.max(s, axis=-1, keepdims=True), sink)
                p = jnp.exp(s - m)
                denom = jnp.sum(p, axis=-1, keepdims=True) + jnp.exp(sink - m)
                pn = (p / denom).astype(jnp.bfloat16)
                outs.append(jnp.dot(pn, vv, preferred_element_type=jnp.float32))
            o_ref[rows, j * LANES:(j + 1) * LANES] = jnp.where(lo_half, outs[0], outs[1]).astype(o_ref.dtype)


def _swa(qa, ka, va, sinks, bias_tab, seq_len):
    t = qa.shape[0]
    tm = SWA_STEP * Q_BLOCK
    assert seq_len % tm == 0
    cur = lambda n, s: (n, 0)
    prev = lambda n, s: (jnp.maximum(n * SWA_STEP - 1, 0), 0)
    grid_spec = pltpu.PrefetchScalarGridSpec(
        num_scalar_prefetch=1,
        grid=(t // tm,),
        in_specs=[pl.BlockSpec((tm, SWA_Q), cur),
                  pl.BlockSpec((Q_BLOCK, SWA_KV), prev), pl.BlockSpec((tm, SWA_KV), cur),
                  pl.BlockSpec((Q_BLOCK, SWA_KV), prev), pl.BlockSpec((tm, SWA_KV), cur),
                  pl.BlockSpec(bias_tab.shape, lambda n, s: (0, 0, 0))],
        out_specs=pl.BlockSpec((tm, SWA_Q), cur))
    return pl.pallas_call(
        functools.partial(_swa_kernel, seq_len // tm),
        grid_spec=grid_spec,
        out_shape=jax.ShapeDtypeStruct((t, SWA_Q), jnp.bfloat16),
        compiler_params=_cparams(("arbitrary",)),
        name="swa",
    )(sinks, qa, ka, ka, va, va, bias_tab)


def _t5_bias_table(rel_bias):
    i = np.arange(Q_BLOCK)[:, None]
    j = np.arange(2 * Q_BLOCK)[None, :]
    n = jnp.asarray(np.maximum(i + Q_BLOCK - j, 0))
    max_exact = N_BUCKETS // 2
    nf = jnp.maximum(n, 1).astype(jnp.float32)
    large = max_exact + (jnp.log(nf / max_exact) / math.log(MAX_DISTANCE / max_exact)
                         * (N_BUCKETS - max_exact)).astype(jnp.int32)
    large = jnp.minimum(large, N_BUCKETS - 1)
    bucket = jnp.where(n < max_exact, n, large)
    onehot = bucket[None] == jnp.arange(N_BUCKETS)[:, None, None]
    rb = rel_bias.astype(jnp.float32)
    return jnp.sum(jnp.where(onehot[:, None], rb[:, :, None, None], 0.0), axis=0)


def _fox_kernel(q_ref, k_ref, v_ref, o_ref):
    qi = pl.program_id(2)
    tq = q_ref.shape[1]
    qs = (q_ref[0], q_ref[1])

    def update(s, m, acc, v):
        m_new = jnp.maximum(m, jnp.max(s, axis=-1, keepdims=True))
        p = jnp.exp(s - m_new).astype(jnp.bfloat16)
        return m_new, jnp.exp(m - m_new) * acc + jnp.dot(p, v, preferred_element_type=jnp.float32)

    def scores(q, k):
        return lax.dot_general(q, k, (((1,), (1,)), ((), ())), preferred_element_type=jnp.float32)

    def step(j, carry):
        start = pl.multiple_of(j * tq, tq)
        return tuple(update(scores(qs[hh], k_ref[hh, pl.ds(start, tq), :]), *carry[hh],
                            v_ref[hh, pl.ds(start, tq), :]) for hh in range(2))

    def diagonal(carry):
        start = pl.multiple_of(qi * tq, tq)
        band = tq // FOX_DIAG_BANDS
        new = []
        for hh in range(2):
            m, acc = carry[hh]
            parts = []
            for r in range(FOX_DIAG_BANDS):
                lo, hi = r * band, (r + 1) * band
                s = scores(qs[hh][lo:hi], k_ref[hh, pl.ds(start, hi), :])
                row = lax.broadcasted_iota(jnp.int32, (band, hi), 0) + lo
                col = lax.broadcasted_iota(jnp.int32, (band, hi), 1)
                s = jnp.where(col <= row, s, -jnp.inf)
                parts.append(update(s, m[lo:hi], acc[lo:hi], v_ref[hh, pl.ds(start, hi), :]))
            new.append((jnp.concatenate([p[0] for p in parts], axis=0),
                        jnp.concatenate([p[1] for p in parts], axis=0)))
        return tuple(new)

    init1 = (jnp.full((tq, 1), -jnp.inf, jnp.float32), jnp.zeros((tq, LANES), jnp.float32))
    carry = diagonal(lax.fori_loop(0, qi, step, (init1, init1)))
    outs = [acc / acc[:, AUG:AUG + 1] for _, acc in carry]
    lane = lax.broadcasted_iota(jnp.int32, (tq, LANES), 1)
    o_ref[...] = jnp.where(lane < HEAD_DIM, outs[0], pltpu.roll(outs[1], HEAD_DIM, axis=1)).astype(o_ref.dtype)


def _fox(qf, kf, vf, batch, seq_len):
    t = qf.shape[1]
    tq = min(FOX_TILE, seq_len)
    assert seq_len % tq == 0 and tq % FOX_DIAG_BANDS == 0
    nq = seq_len // tq
    pairs = FOX_HEADS // 2
    seq = pl.BlockSpec((2, seq_len, LANES), lambda b, hp, i: (hp, b, 0))
    return pl.pallas_call(
        _fox_kernel,
        grid=(batch, pairs, nq),
        in_specs=[pl.BlockSpec((2, tq, LANES), lambda b, hp, i: (hp, b * nq + i, 0)), seq, seq],
        out_specs=pl.BlockSpec((tq, LANES), lambda b, hp, i: (b * nq + i, hp)),
        out_shape=jax.ShapeDtypeStruct((t, FOX_W), jnp.bfloat16),
        compiler_params=_cparams(("arbitrary", "arbitrary", "arbitrary")),
        name="fox",
    )(qf, kf, vf)


def _layer_norm(h, g, b):
    mu = jnp.mean(h, axis=-1, keepdims=True)
    d = h - mu
    var = jnp.mean(d * d, axis=-1, keepdims=True)
    return d * lax.rsqrt(var + LN_EPS) * g + b


def _pack_rows(ref, x):
    rows = x.shape[0]
    bits = pltpu.bitcast(x.astype(jnp.bfloat16).astype(jnp.float32), jnp.uint32)
    for c in range(ROW_WORDS):
        lo = bits[:, c * LANES:(c + 1) * LANES] >> 16
        hi = bits[:, HALF_D + c * LANES:HALF_D + (c + 1) * LANES] & jnp.uint32(HI_MASK)
        ref[pl.ds(c, rows, stride=ROW_WORDS), :] = lo | hi


def _unpack_rows(ref, first, rows):
    los, his = [], []
    for c in range(ROW_WORDS):
        w = ref[pl.ds(first * ROW_WORDS + c, rows, stride=ROW_WORDS), :]
        los.append(pltpu.bitcast(w << 16, jnp.float32))
        his.append(pltpu.bitcast(w & jnp.uint32(HI_MASK), jnp.float32))
    return jnp.concatenate(los + his, axis=1)


def _packed(ref, first, rows=1):
    return ref.at[pl.ds(pl.multiple_of(first * ROW_WORDS, ROW_WORDS), rows * ROW_WORDS), :]


def _packed_spec(rows, index_map):
    return pl.BlockSpec((rows * ROW_WORDS, LANES), index_map)


def _packed_shape(rows):
    return jax.ShapeDtypeStruct((rows * ROW_WORDS, LANES), jnp.uint32)


def _merge_kernel(alpha, x_ref, ya_ref, yb_ref, ga_ref, gb_ref, wa_ref, wb_ref, wo_ref, g_ref, b_ref,
                  x1_ref, x1p_ref):
    pa = jnp.dot(ya_ref[...], wa_ref[...], preferred_element_type=jnp.float32)
    pb = jnp.dot(yb_ref[...], wb_ref[...], preferred_element_type=jnp.float32)
    mix = _sigmoid(ga_ref[...]) * pa + _sigmoid(gb_ref[...]) * pb
    out = jnp.dot(mix.astype(jnp.bfloat16), wo_ref[...], preferred_element_type=jnp.float32)
    x1 = _layer_norm(alpha * x_ref[...] + out, g_ref[...], b_ref[...])
    x1_ref[...] = x1
    _pack_rows(x1p_ref, x1)


def _merge(alpha, x2, ya, yb, ga, gb, wa, wb, wo, g, b):
    t = x2.shape[0]
    tm = TM_MERGE
    row = lambda n: pl.BlockSpec((tm, n), lambda i: (i, 0))
    full = lambda a: pl.BlockSpec(a.shape, lambda i: (0,) * a.ndim)
    return pl.pallas_call(
        functools.partial(_merge_kernel, alpha),
        grid=(t // tm,),
        in_specs=[row(D_MODEL), row(SWA_Q), row(FOX_W), row(D_MODEL), row(D_MODEL),
                  full(wa), full(wb), full(wo), full(g), full(b)],
        out_specs=(row(D_MODEL), _packed_spec(tm, lambda i: (i, 0))),
        out_shape=(jax.ShapeDtypeStruct((t, D_MODEL), jnp.float32), _packed_shape(t)),
        compiler_params=_cparams(("arbitrary",)),
        name="merge",
    )(x2, ya, yb, ga, gb, wa, wb, wo, g, b)


def _first_argmax(v, idx):
    m = jnp.max(v, axis=0, keepdims=True)
    first = jnp.min(jnp.where(v == m, idx, v.shape[0]), axis=0, keepdims=True)
    return m, first


def _router_kernel(x_ref, wr_ref, rb_ref, tri_ref, e_ref, w_ref, r_ref, cnt_ref, carry_ref):
    i = pl.program_id(0)
    tm = x_ref.shape[0]

    @pl.when(i == 0)
    def _():
        carry_ref[...] = jnp.zeros_like(carry_ref)

    xb = x_ref[...].astype(jnp.bfloat16)
    logits = lax.dot_general(wr_ref[...], xb, (((1,), (1,)), ((), ())),
                             preferred_element_type=jnp.float32)
    scores = _sigmoid(logits)
    biased = scores + rb_ref[...]

    in_group = lax.broadcasted_iota(jnp.int32, (GROUP_SIZE, tm), 0)
    grows = []
    for g in range(N_GROUPS):
        blk = biased[g * GROUP_SIZE:(g + 1) * GROUP_SIZE, :]
        m1, f1 = _first_argmax(blk, in_group)
        m2 = jnp.max(jnp.where(in_group == f1, -jnp.inf, blk), axis=0, keepdims=True)
        grows.append(m1 + m2)
    gscore = jnp.concatenate(grows, axis=0)

    gidx = lax.broadcasted_iota(jnp.int32, (N_GROUPS, tm), 0)
    gwork = gscore
    gsel = jnp.zeros((N_GROUPS, tm), jnp.float32)
    for _ in range(TOPK_GROUPS):
        _, first = _first_argmax(gwork, gidx)
        hit = gidx == first
        gsel = jnp.where(hit, 1.0, gsel)
        gwork = jnp.where(hit, -jnp.inf, gwork)

    eidx = lax.broadcasted_iota(jnp.int32, (N_EXPERTS, tm), 0)
    work = jnp.concatenate(
        [jnp.where(jnp.broadcast_to(gsel[g:g + 1, :], (GROUP_SIZE, tm)) > 0.0,
                   biased[g * GROUP_SIZE:(g + 1) * GROUP_SIZE, :], -jnp.inf) for g in range(N_GROUPS)], axis=0)
    picks = []
    sel = jnp.zeros((N_EXPERTS, tm), jnp.bool_)
    for _ in range(TOP_K):
        _, first = _first_argmax(work, eidx)
        hit = eidx == first
        picks.append((first, hit))
        sel = sel | hit
        work = jnp.where(hit, -jnp.inf, work)

    wsum = jnp.sum(jnp.where(sel, scores, 0.0), axis=0, keepdims=True)
    sel_b = jnp.where(sel, 1.0, 0.0).astype(jnp.bfloat16)
    before = carry_ref[...] + jnp.dot(sel_b, tri_ref[...], preferred_element_type=jnp.float32)
    e_rows, w_rows, r_rows = [], [], []
    for first, hit in picks:
        e_rows.append(first)
        sc = jnp.sum(jnp.where(hit, scores, 0.0), axis=0, keepdims=True)
        w_rows.append(sc / wsum * ROUTED_SCALE)
        r_rows.append(jnp.sum(jnp.where(hit, before, 0.0), axis=0, keepdims=True).astype(jnp.int32))
    e_ref[...] = jnp.concatenate(e_rows, axis=0)
    w_ref[...] = jnp.concatenate(w_rows, axis=0)
    r_ref[...] = jnp.concatenate(r_rows, axis=0)
    total = carry_ref[...] + jnp.sum(sel_b.astype(jnp.float32), axis=1, keepdims=True)
    carry_ref[...] = total
    cnt_ref[...] = jnp.broadcast_to(total, cnt_ref.shape).astype(jnp.int32)


def _router(x1, wr_t, rb_col):
    t = x1.shape[0]
    tm = TM_ROUTE
    tri = jnp.triu(jnp.ones((tm, tm), jnp.bfloat16), k=1)
    full = lambda a: pl.BlockSpec(a.shape, lambda i: (0,) * a.ndim)
    kt = pl.BlockSpec((TOP_K, tm), lambda i: (0, i))
    return pl.pallas_call(
        _router_kernel,
        grid=(t // tm,),
        in_specs=[pl.BlockSpec((tm, D_MODEL), lambda i: (i, 0)), full(wr_t), full(rb_col), full(tri)],
        out_specs=(kt, kt, kt, pl.BlockSpec((N_EXPERTS, LANES), lambda i: (0, 0))),
        out_shape=(jax.ShapeDtypeStruct((TOP_K, t), jnp.int32), jax.ShapeDtypeStruct((TOP_K, t), jnp.float32),
                   jax.ShapeDtypeStruct((TOP_K, t), jnp.int32),
                   jax.ShapeDtypeStruct((N_EXPERTS, LANES), jnp.int32)),
        scratch_shapes=[pltpu.VMEM((N_EXPERTS, 1), jnp.float32)],
        compiler_params=_cparams(("arbitrary",)),
        name="router",
    )(x1, wr_t, rb_col, tri)


def _plan_kernel(ps_ref, e_ref, r_ref, d_ref):
    tm = e_ref.shape[1]
    eidx = lax.broadcasted_iota(jnp.int32, (N_EXPERTS, tm), 0)
    rows = []
    for k in range(TOP_K):
        hit = eidx == e_ref[k:k + 1, :]
        base = jnp.sum(jnp.where(hit, ps_ref[...], 0.0), axis=0, keepdims=True)
        rows.append(base.astype(jnp.int32) + r_ref[k:k + 1, :])
    d_ref[...] = jnp.concatenate(rows, axis=0)


def _plan(pstart_col, top_e, rank):
    t = top_e.shape[1]
    tm = min(TM_PLAN, t)
    kt = pl.BlockSpec((TOP_K, tm), lambda i: (0, i))
    return pl.pallas_call(
        _plan_kernel,
        grid=(t // tm,),
        in_specs=[pl.BlockSpec(pstart_col.shape, lambda i: (0, 0)), kt, kt],
        out_specs=kt,
        out_shape=jax.ShapeDtypeStruct((TOP_K, t), jnp.int32),
        compiler_params=_cparams(("arbitrary",)),
        name="plan",
    )(pstart_col, top_e, rank)


def _dispatch_kernel(pend_ref, padded_ref, nu_ref, dest_hbm, x_ref, xs_hbm, idx_smem, zero_vmem, sem, zsem):
    i = pl.program_id(0)
    n_blocks = xs_hbm.shape[0] // (EXPERT_BLOCK * ROW_WORDS)

    def zero_copy(first_row):
        return pltpu.make_async_copy(zero_vmem, _packed(xs_hbm, first_row, EXPERT_BLOCK), zsem)

    @pl.when(i == 0)
    def _():
        zero_vmem[...] = jnp.zeros_like(zero_vmem)

        def fill(e, c):
            @pl.when(padded_ref[e] > 0)
            def _():
                zero_copy(pend_ref[e] - EXPERT_BLOCK).start()
            return c
        lax.fori_loop(0, N_EXPERTS, fill, 0)

        def fill_tail(b, c):
            zero_copy(b * EXPERT_BLOCK).start()
            return c
        lax.fori_loop(nu_ref[0], n_blocks, fill_tail, 0)

        def drain(e, c):
            @pl.when(padded_ref[e] > 0)
            def _():
                zero_copy(0).wait()
            return c
        lax.fori_loop(0, N_EXPERTS, drain, 0)

        def drain_tail(b, c):
            zero_copy(0).wait()
            return c
        lax.fori_loop(nu_ref[0], n_blocks, drain_tail, 0)

    n_idx = TM_DISPATCH * TOP_K
    pltpu.sync_copy(dest_hbm.at[pl.ds(i * n_idx, n_idx)], idx_smem)

    def issue(t, c):
        for k in range(TOP_K):
            pltpu.make_async_copy(_packed(x_ref, t), _packed(xs_hbm, idx_smem[t * TOP_K + k]),
                                  sem).start(priority=k % 2)
        return c
    lax.fori_loop(0, TM_DISPATCH, issue, 0)

    for k in range(TOP_K):
        pltpu.make_async_copy(x_ref, _packed(xs_hbm, 0, TM_DISPATCH), sem).wait()


def _dispatch(dest_flat, x1p, pend, padded, n_used, n_rows):
    t = x1p.shape[0] // ROW_WORDS
    grid_spec = pltpu.PrefetchScalarGridSpec(
        num_scalar_prefetch=3,
        grid=(t // TM_DISPATCH,),
        in_specs=[pl.BlockSpec(memory_space=pl.ANY), _packed_spec(TM_DISPATCH, lambda i, pe, pa, nu: (i, 0))],
        out_specs=pl.BlockSpec(memory_space=pl.ANY),
        scratch_shapes=[pltpu.SMEM((TM_DISPATCH * TOP_K,), jnp.int32),
                        pltpu.VMEM((EXPERT_BLOCK * ROW_WORDS, LANES), jnp.uint32),
                        pltpu.SemaphoreType.DMA, pltpu.SemaphoreType.DMA])
    return pl.pallas_call(
        _dispatch_kernel,
        grid_spec=grid_spec,
        out_shape=_packed_shape(n_rows),
        compiler_params=_cparams(("arbitrary",)),
        name="dispatch",
    )(pend, padded, n_used, dest_flat, x1p)


def _experts_kernel(first_ref, nblk_ref, nu_ref, xs_hbm, wg_ref, wu_ref, wd_ref, y_hbm,
                    xbuf, ybuf, wg_b, wu_b, wd_b, in_sem, out_sem):
    nc = EXPERT_BLOCK // EXPERT_SUB
    e = pl.program_id(0)
    n_used = nu_ref[0]
    n_blocks = y_hbm.shape[0] // (EXPERT_BLOCK * ROW_WORDS)
    first = first_ref[e]

    def slot_rows(ref, slot):
        return _packed(ref, slot * EXPERT_BLOCK, EXPERT_BLOCK)

    def fetch(b):
        return pltpu.make_async_copy(_packed(xs_hbm, b * EXPERT_BLOCK, EXPERT_BLOCK), slot_rows(xbuf, b % 2),
                                     in_sem.at[b % 2])

    def write_back(b):
        return pltpu.make_async_copy(slot_rows(ybuf, b % 2), _packed(y_hbm, b * EXPERT_BLOCK, EXPERT_BLOCK),
                                     out_sem.at[b % 2])

    @pl.when(e == 0)
    def _():
        fetch(0).start(priority=ROW_DMA_PRIORITY)

    @pl.when(nblk_ref[e] > 0)
    def _():
        wg_b[...] = wg_ref[...].astype(jnp.bfloat16)
        wu_b[...] = wu_ref[...].astype(jnp.bfloat16)
        wd_b[...] = wd_ref[...].astype(jnp.bfloat16)

    def block(j, carry):
        b = first + j
        slot = b % 2

        @pl.when(b + 1 < n_used)
        def _():
            fetch(b + 1).start(priority=ROW_DMA_PRIORITY)

        fetch(b).wait()

        @pl.when(b >= 2)
        def _():
            write_back(b - 2).wait()

        base = slot * EXPERT_BLOCK
        xbs = [_unpack_rows(xbuf, base + c * EXPERT_SUB, EXPERT_SUB).astype(jnp.bfloat16) for c in range(nc)]
        gus = [(jnp.dot(xb, wg_b[...], preferred_element_type=jnp.float32),
                jnp.dot(xb, wu_b[...], preferred_element_type=jnp.float32)) for xb in xbs]
        ys = []
        for g, u in gus:
            h = (g * _sigmoid(g) * u).astype(jnp.bfloat16)
            ys.append(jnp.dot(h, wd_b[...], preferred_element_type=jnp.float32))
        for c, y in enumerate(ys):
            _pack_rows(_packed(ybuf, base + c * EXPERT_SUB, EXPERT_SUB), y)
        write_back(b).start(priority=ROW_DMA_PRIORITY)
        return carry

    lax.fori_loop(0, nblk_ref[e], block, 0)

    @pl.when(e == pl.num_programs(0) - 1)
    def _():
        @pl.when(n_used >= 2)
        def _():
            write_back(n_used - 2).wait()
        write_back(n_used - 1).wait()

        xbuf[...] = jnp.zeros_like(xbuf)

        def zero_copy(b):
            return pltpu.make_async_copy(slot_rows(xbuf, 0), _packed(y_hbm, b * EXPERT_BLOCK, EXPERT_BLOCK),
                                         out_sem.at[0])

        def fill(b, c):
            zero_copy(b).start()
            return c
        lax.fori_loop(n_used, n_blocks, fill, 0)

        def drain(b, c):
            zero_copy(b).wait()
            return c
        lax.fori_loop(n_used, n_blocks, drain, 0)


def _experts(layer, first_blk, n_blk, n_used, xs, w_gate_e, w_up_e, w_down_e):
    wsel = lambda e, fb, nb, nu: (layer, e, 0, 0)
    slots = pltpu.VMEM((2 * EXPERT_BLOCK * ROW_WORDS, LANES), jnp.uint32)
    grid_spec = pltpu.PrefetchScalarGridSpec(
        num_scalar_prefetch=3,
        grid=(N_EXPERTS,),
        in_specs=[pl.BlockSpec(memory_space=pl.ANY),
                  pl.BlockSpec((None, None, D_MODEL, D_EXPERT), wsel),
                  pl.BlockSpec((None, None, D_MODEL, D_EXPERT), wsel),
                  pl.BlockSpec((None, None, D_EXPERT, D_MODEL), wsel)],
        out_specs=pl.BlockSpec(memory_space=pl.ANY),
        scratch_shapes=[slots, slots,
                        pltpu.VMEM((D_MODEL, D_EXPERT), jnp.bfloat16), pltpu.VMEM((D_MODEL, D_EXPERT), jnp.bfloat16),
                        pltpu.VMEM((D_EXPERT, D_MODEL), jnp.bfloat16),
                        pltpu.SemaphoreType.DMA((2,)), pltpu.SemaphoreType.DMA((2,))])
    return pl.pallas_call(
        _experts_kernel,
        grid_spec=grid_spec,
        out_shape=jax.ShapeDtypeStruct(xs.shape, jnp.uint32),
        compiler_params=_cparams(("arbitrary",)),
        name="experts",
    )(first_blk, n_blk, n_used, xs, w_gate_e, w_up_e, w_down_e)


def _combine_kernel(alpha, dest_hbm, y_hbm, w_ref, x1_ref, wg_ref, wu_ref, wd_ref, g_ref, b_ref,
                    o_ref, idx_smem0, idx_smem1, buf, sem):
    i = pl.program_id(0)
    n = pl.num_programs(0)
    n_idx = TM_COMBINE * TOP_K

    def slab(slot, k):
        return (slot * TOP_K + k) * TM_COMBINE

    def start_gather(step, slot):
        idx_smem = (idx_smem0, idx_smem1)[slot]
        pltpu.sync_copy(dest_hbm.at[pl.ds(step * n_idx, n_idx)], idx_smem)

        def issue(t, c):
            for k in range(TOP_K):
                pltpu.make_async_copy(_packed(y_hbm, idx_smem[t * TOP_K + k]),
                                      _packed(buf, slab(slot, k) + t), sem.at[slot]).start(priority=k % 2)
            return c
        lax.fori_loop(0, TM_COMBINE, issue, 0)

    @pl.when(i == 0)
    def _():
        start_gather(0, 0)

    def run(slot):
        @pl.when(i + 1 < n)
        def _():
            start_gather(i + 1, 1 - slot)

        for k in range(TOP_K):
            pltpu.make_async_copy(_packed(y_hbm, 0, TM_COMBINE), _packed(buf, slab(slot, k), TM_COMBINE),
                                  sem.at[slot]).wait()

        w = w_ref[...]
        routed = jnp.zeros((TM_COMBINE, D_MODEL), jnp.float32)
        for k in range(TOP_K):
            routed = routed + w[:, k:k + 1] * _unpack_rows(buf, slab(slot, k), TM_COMBINE)

        x1 = x1_ref[...]
        xb = x1.astype(jnp.bfloat16)
        gs = jnp.dot(xb, wg_ref[...], preferred_element_type=jnp.float32)
        us = jnp.dot(xb, wu_ref[...], preferred_element_type=jnp.float32)
        hs = gs * _sigmoid(gs) * us
        shared = jnp.dot(hs.astype(jnp.bfloat16), wd_ref[...], preferred_element_type=jnp.float32)
        o_ref[...] = _layer_norm(alpha * x1 + (shared + routed), g_ref[...], b_ref[...])

    for slot in range(2):
        pl.when(i % 2 == slot)(functools.partial(run, slot))


def _combine(alpha, dest_flat, y, w_tk, x1, wg, wu, wd, g, b):
    t = x1.shape[0]
    tm = TM_COMBINE
    full = lambda a: pl.BlockSpec(a.shape, lambda i: (0,) * a.ndim)
    return pl.pallas_call(
        functools.partial(_combine_kernel, alpha),
        grid=(t // tm,),
        in_specs=[pl.BlockSpec(memory_space=pl.ANY), pl.BlockSpec(memory_space=pl.ANY),
                  pl.BlockSpec((tm, TOP_K), lambda i: (i, 0)), pl.BlockSpec((tm, D_MODEL), lambda i: (i, 0)),
                  full(wg), full(wu), full(wd), full(g), full(b)],
        out_specs=pl.BlockSpec((tm, D_MODEL), lambda i: (i, 0)),
        out_shape=jax.ShapeDtypeStruct((t, D_MODEL), jnp.float32),
        scratch_shapes=[pltpu.SMEM((tm * TOP_K,), jnp.int32), pltpu.SMEM((tm * TOP_K,), jnp.int32),
                        pltpu.VMEM((2 * TOP_K * tm * ROW_WORDS, LANES), jnp.uint32),
                        pltpu.SemaphoreType.DMA((2,))],
        compiler_params=_cparams(("arbitrary",)),
        name="combine",
    )(dest_flat, y, w_tk, x1, wg, wu, wd, g, b)


def _prep_in_weights(w, b_forget):
    o = np.cumsum((0,) + IN_SIZES)
    group = SWA_Q_HEADS // SWA_KV_HEADS
    qa = w[:, o[0]:o[1]].reshape(D_MODEL, SWA_KV_HEADS, group, HEAD_DIM)
    qa = qa.transpose(0, 2, 1, 3).reshape(D_MODEL, SWA_Q)
    zf = jnp.pad(w[:, o[6]:o[7]], ((0, 0), (0, LANES - FOX_HEADS)))
    w_main = jnp.concatenate([qa, w[:, o[1]:o[6]], w[:, o[7]:o[9]], zf], axis=1).astype(jnp.bfloat16)
    bf_row = jnp.pad(b_forget, (0, LANES - FOX_HEADS)).reshape(1, LANES).astype(jnp.float32)
    return w_main, bf_row


def _prep_proj_a(w):
    group = SWA_Q_HEADS // SWA_KV_HEADS
    w4 = w.reshape(SWA_KV_HEADS, group, HEAD_DIM, D_MODEL).transpose(1, 0, 2, 3)
    return w4.reshape(SWA_Q, D_MODEL).astype(jnp.bfloat16)


def _route_plan(counts, n_tokens):
    n_assign = n_tokens * TOP_K
    n_rows = (n_assign + N_EXPERTS * (EXPERT_BLOCK - 1) + EXPERT_BLOCK - 1) // EXPERT_BLOCK * EXPERT_BLOCK
    padded = (counts + EXPERT_BLOCK - 1) // EXPERT_BLOCK * EXPERT_BLOCK
    pend = jnp.cumsum(padded)
    pstart = pend - padded
    n_used = pend[-1] // EXPERT_BLOCK
    i32 = jnp.int32
    return (pstart.astype(jnp.float32).reshape(N_EXPERTS, 1), (pstart // EXPERT_BLOCK).astype(i32),
            (padded // EXPERT_BLOCK).astype(i32), n_used.reshape(1).astype(i32), pend.astype(i32),
            padded.astype(i32), n_rows)


def kernel(x, w_in, b_forget, attn_sinks, rel_bias, w_proj_a, w_proj_b, w_out, ln1_g, ln1_b,
           w_router, router_bias, w_gate_e, w_up_e, w_down_e, w_gate_s, w_up_s, w_down_s, ln2_g, ln2_b):
    batch, seq_len, d = x.shape
    depth = w_in.shape[0]
    alpha = (2 * depth) ** 0.25
    t = batch * seq_len
    bf16 = jnp.bfloat16
    x2 = x.reshape(t, d)
    bias_tab = _t5_bias_table(rel_bias)
    for l in range(depth):
        w_main, bf_row = _prep_in_weights(w_in[l], b_forget[l])
        qa, ka, va, qf, kf, vf, ga, gb = _in_proj(x2, w_main, bf_row, seq_len)
        ya = _swa(qa, ka, va, attn_sinks[l].astype(jnp.float32), bias_tab, seq_len)
        yb = _fox(qf, kf, vf, batch, seq_len)
        x1, x1p = _merge(alpha, x2, ya, yb, ga, gb, _prep_proj_a(w_proj_a[l]), w_proj_b[l].astype(bf16),
                         w_out[l].astype(bf16), ln1_g[l].reshape(1, d), ln1_b[l].reshape(1, d))
        top_e, w_kt, rank, counts = _router(x1, w_router[l].T.astype(bf16),
                                            router_bias[l].reshape(N_EXPERTS, 1).astype(jnp.float32))
        pstart_col, first_blk, n_blk, n_used, pend, padded, n_rows = _route_plan(counts[:, 0], t)
        dest_flat = _plan(pstart_col, top_e, rank).T.reshape(-1)
        xs = _dispatch(dest_flat, x1p, pend, padded, n_used, n_rows)
        y = _experts(l, first_blk, n_blk, n_used, xs, w_gate_e, w_up_e, w_down_e)
        x2 = _combine(alpha, dest_flat, y, w_kt.T, x1, w_gate_s[l].astype(bf16), w_up_s[l].astype(bf16),
                      w_down_s[l].astype(bf16), ln2_g[l].reshape(1, d), ln2_b[l].reshape(1, d))
    return x2.reshape(batch, seq_len, d)
```

```python
import functools
import math

import jax
import jax.numpy as jnp
import numpy as np
from jax import lax
from jax.experimental import pallas as pl
from jax.experimental.pallas import tpu as pltpu

D_MODEL = 1024
HEAD_DIM = 64
SWA_Q_HEADS = 8
SWA_KV_HEADS = 2
SWA_WINDOW = 128
FOX_HEADS = 8
Q_BLOCK = 128
N_BUCKETS = 32
MAX_DISTANCE = 128
N_EXPERTS = 256
TOP_K = 8
N_GROUPS = 8
TOPK_GROUPS = 4
GROUP_SIZE = N_EXPERTS // N_GROUPS
D_EXPERT = 256
D_SHARED = 256
ROUTED_SCALE = 2.5
LN_EPS = 1e-5

SWA_Q = SWA_Q_HEADS * HEAD_DIM
SWA_KV = SWA_KV_HEADS * HEAD_DIM
FOX_W = FOX_HEADS * HEAD_DIM
IN_SIZES = (SWA_Q, SWA_KV, SWA_KV, FOX_W, FOX_W, FOX_W, FOX_HEADS, D_MODEL, D_MODEL)

LANES = 128
HALF_D = D_MODEL // 2
ROW_WORDS = HALF_D // LANES
VMEM_LIMIT = 56 * 1024 * 1024
HI_MASK = 0xFFFF0000

TM_PROJ = 512
TM_MERGE = 512
TM_ROUTE = 256
TM_DISPATCH = 1024
TM_COMBINE = 512
TM_PLAN = 2048
EXPERT_BLOCK = 512
EXPERT_SUB = 512
EXPERT_SLOTS = 4
SWA_STEP = 4
FOX_TILE = 1024
FOX_DIAG_BANDS = 2

AUG = HEAD_DIM


def _cparams(sem):
    return pltpu.CompilerParams(dimension_semantics=sem, vmem_limit_bytes=VMEM_LIMIT)


def _sigmoid(x):
    return 1.0 / (1.0 + jnp.exp(-x))


def _split3(c):
    hi = c.astype(jnp.bfloat16).astype(jnp.float32)
    r = c - hi
    mid = r.astype(jnp.bfloat16).astype(jnp.float32)
    lo = (r - mid).astype(jnp.bfloat16).astype(jnp.float32)
    return hi, mid, lo


_C_QA = 0
_C_KA = _C_QA + SWA_Q
_C_VA = _C_KA + SWA_KV
_C_QF = _C_VA + SWA_KV
_C_KF = _C_QF + FOX_W
_C_VF = _C_KF + FOX_W
_C_GA = _C_VF + FOX_W
_C_GB = _C_GA + D_MODEL
_C_ZF = _C_GB + D_MODEL
_N_PROJ = _C_ZF + LANES


def _in_proj_kernel(tiles_per_seq, x_ref, w_ref, bf_ref, tri_ref,
                    qa_ref, ka_ref, va_ref, qf_ref, kf_ref, vf_ref, ga_ref, gb_ref, carry_ref):
    i = pl.program_id(0)
    tm = x_ref.shape[0]
    xb = x_ref[...].astype(jnp.bfloat16)

    def proj(c0, n):
        return jnp.dot(xb, w_ref[:, c0:c0 + n], preferred_element_type=jnp.float32)

    scale = HEAD_DIM ** -0.5
    qa_ref[...] = (proj(_C_QA, SWA_Q) * scale).astype(jnp.bfloat16)
    ka_ref[...] = proj(_C_KA, SWA_KV).astype(jnp.bfloat16)
    va_ref[...] = proj(_C_VA, SWA_KV).astype(jnp.bfloat16)
    ga_ref[...] = proj(_C_GA, D_MODEL)
    gb_ref[...] = proj(_C_GB, D_MODEL)

    z = proj(_C_ZF, LANES) + bf_ref[...]
    log_f = jnp.minimum(z, 0.0) - jnp.log(1.0 + jnp.exp(-jnp.abs(z)))

    @pl.when(i % tiles_per_seq == 0)
    def _():
        carry_ref[...] = jnp.zeros_like(carry_ref)

    tri = tri_ref[...]
    h3 = _split3(log_f)
    c = carry_ref[...]
    for part in h3:
        c = c + jnp.dot(tri, part.astype(jnp.bfloat16), preferred_element_type=jnp.float32)
    carry_ref[...] = c[tm - 1:tm, :]

    qf = proj(_C_QF, FOX_W) * scale
    kf = proj(_C_KF, FOX_W)
    vf = proj(_C_VF, FOX_W)
    lane = lax.broadcasted_iota(jnp.int32, (tm, LANES), 1)
    one = jnp.float32(1.0)
    zero = jnp.float32(0.0)
    for h in range(FOX_HEADS):
        pair = (h // 2) * LANES
        qp = qf[:, pair:pair + LANES]
        kp = kf[:, pair:pair + LANES]
        vp = vf[:, pair:pair + LANES]
        if h % 2 == 1:
            qp = pltpu.roll(qp, HEAD_DIM, axis=1)
            kp = pltpu.roll(kp, HEAD_DIM, axis=1)
            vp = pltpu.roll(vp, HEAD_DIM, axis=1)
        ch = jnp.broadcast_to(c[:, h:h + 1], (tm, LANES))
        hi, mid, lo = _split3(ch)
        q_aug = jnp.where(lane == AUG, hi, jnp.where(lane == AUG + 1, mid, jnp.where(
            lane == AUG + 2, lo, jnp.where(lane < AUG + 6, one, zero))))
        k_aug = jnp.where(lane < AUG + 3, one, jnp.where(lane == AUG + 3, -hi, jnp.where(
            lane == AUG + 4, -mid, jnp.where(lane == AUG + 5, -lo, zero))))
        qf_ref[h] = jnp.where(lane < AUG, qp, q_aug).astype(jnp.bfloat16)
        kf_ref[h] = jnp.where(lane < AUG, kp, k_aug).astype(jnp.bfloat16)
        vf_ref[h] = jnp.where(lane < AUG, vp, jnp.where(lane == AUG, one, zero)).astype(jnp.bfloat16)


def _in_proj(x2, w_main, bf_row, seq_len):
    t = x2.shape[0]
    tm = min(TM_PROJ, seq_len)
    tri = jnp.tril(jnp.ones((tm, tm), jnp.bfloat16))
    row = lambda n: pl.BlockSpec((tm, n), lambda i: (i, 0))
    full = lambda a: pl.BlockSpec(a.shape, lambda i: (0,) * a.ndim)
    slab = pl.BlockSpec((FOX_HEADS, tm, LANES), lambda i: (0, i, 0))
    bf16, f32 = jnp.bfloat16, jnp.float32
    slab_shape = jax.ShapeDtypeStruct((FOX_HEADS, t, LANES), bf16)
    out_shape = (
        jax.ShapeDtypeStruct((t, SWA_Q), bf16), jax.ShapeDtypeStruct((t, SWA_KV), bf16),
        jax.ShapeDtypeStruct((t, SWA_KV), bf16), slab_shape, slab_shape, slab_shape,
        jax.ShapeDtypeStruct((t, D_MODEL), f32), jax.ShapeDtypeStruct((t, D_MODEL), f32))
    return pl.pallas_call(
        functools.partial(_in_proj_kernel, seq_len // tm),
        grid=(t // tm,),
        in_specs=[row(D_MODEL), full(w_main), full(bf_row), full(tri)],
        out_specs=(row(SWA_Q), row(SWA_KV), row(SWA_KV), slab, slab, slab, row(D_MODEL), row(D_MODEL)),
        out_shape=out_shape,
        scratch_shapes=[pltpu.VMEM((1, LANES), f32)],
        compiler_params=_cparams(("arbitrary",)),
        name="in_proj",
    )(x2, w_main, bf_row, tri)


def _swa_kernel(steps_per_seq, sink_ref, q_ref, kp_ref, kc_ref, vp_ref, vc_ref, bias_ref, o_ref):
    n = pl.program_id(0)
    kall = jnp.concatenate([kp_ref[...], kc_ref[...]], axis=0)
    vall = jnp.concatenate([vp_ref[...], vc_ref[...]], axis=0)
    qi = lax.broadcasted_iota(jnp.int32, (Q_BLOCK, 2 * Q_BLOCK), 0)
    kj = lax.broadcasted_iota(jnp.int32, (Q_BLOCK, 2 * Q_BLOCK), 1)
    dist = qi + Q_BLOCK - kj
    in_window = (dist >= 0) & (dist < SWA_WINDOW)
    first_key = jnp.where((n % steps_per_seq) > 0, 0, Q_BLOCK)
    lane = lax.broadcasted_iota(jnp.int32, (Q_BLOCK, LANES), 1)
    lo_half = lane < HEAD_DIM
    group = SWA_Q_HEADS // SWA_KV_HEADS
    for b in range(SWA_STEP):
        rows = slice(b * Q_BLOCK, (b + 1) * Q_BLOCK)
        kk = kall[b * Q_BLOCK:(b + 2) * Q_BLOCK]
        vv = vall[b * Q_BLOCK:(b + 2) * Q_BLOCK]
        valid = (in_window & (kj >= first_key)) if b == 0 else in_window
        for j in range(group):
            qs = q_ref[rows, j * LANES:(j + 1) * LANES]
            outs = []
            for kv in range(SWA_KV_HEADS):
                head = kv * group + j
                qm = jnp.where(lo_half if kv == 0 else ~lo_half, qs, jnp.zeros_like(qs))
                s = lax.dot_general(qm, kk, (((1,), (1,)), ((), ())), preferred_element_type=jnp.float32)
                s = jnp.where(valid, s + bias_ref[head], -jnp.inf)
                sink = sink_ref[head]
                m = jnp.maximum(jnp.max(s, axis=-1, keepdims=True), sink)
                p = jnp.exp(s - m)
                denom = jnp.sum(p, axis=-1, keepdims=True) + jnp.exp(sink - m)
                pn = (p / denom).astype(jnp.bfloat16)
                outs.append(jnp.dot(pn, vv, preferred_element_type=jnp.float32))
            o_ref[rows, j * LANES:(j + 1) * LANES] = jnp.where(lo_half, outs[0], outs[1]).astype(o_ref.dtype)


def _swa(qa, ka, va, sinks, bias_tab, seq_len):
    t = qa.shape[0]
    tm = SWA_STEP * Q_BLOCK
    assert seq_len % tm == 0
    cur = lambda n, s: (n, 0)
    prev = lambda n, s: (jnp.maximum(n * SWA_STEP - 1, 0), 0)
    grid_spec = pltpu.PrefetchScalarGridSpec(
        num_scalar_prefetch=1,
        grid=(t // tm,),
        in_specs=[pl.BlockSpec((tm, SWA_Q), cur),
                  pl.BlockSpec((Q_BLOCK, SWA_KV), prev), pl.BlockSpec((tm, SWA_KV), cur),
                  pl.BlockSpec((Q_BLOCK, SWA_KV), prev), pl.BlockSpec((tm, SWA_KV), cur),
                  pl.BlockSpec(bias_tab.shape, lambda n, s: (0, 0, 0))],
        out_specs=pl.BlockSpec((tm, SWA_Q), cur))
    return pl.pallas_call(
        functools.partial(_swa_kernel, seq_len // tm),
        grid_spec=grid_spec,
        out_shape=jax.ShapeDtypeStruct((t, SWA_Q), jnp.bfloat16),
        compiler_params=_cparams(("arbitrary",)),
        name="swa",
    )(sinks, qa, ka, ka, va, va, bias_tab)


def _t5_bias_table(rel_bias):
    i = np.arange(Q_BLOCK)[:, None]
    j = np.arange(2 * Q_BLOCK)[None, :]
    n = jnp.asarray(np.maximum(i + Q_BLOCK - j, 0))
    max_exact = N_BUCKETS // 2
    nf = jnp.maximum(n, 1).astype(jnp.float32)
    large = max_exact + (jnp.log(nf / max_exact) / math.log(MAX_DISTANCE / max_exact)
                         * (N_BUCKETS - max_exact)).astype(jnp.int32)
    large = jnp.minimum(large, N_BUCKETS - 1)
    bucket = jnp.where(n < max_exact, n, large)
    onehot = bucket[None] == jnp.arange(N_BUCKETS)[:, None, None]
    rb = rel_bias.astype(jnp.float32)
    return jnp.sum(jnp.where(onehot[:, None], rb[:, :, None, None], 0.0), axis=0)


def _fox_kernel(q_ref, k_ref, v_ref, o_ref):
    qi = pl.program_id(2)
    tq = q_ref.shape[1]
    qs = (q_ref[0], q_ref[1])

    def update(s, m, acc, v):
        m_new = jnp.maximum(m, jnp.max(s, axis=-1, keepdims=True))
        p = jnp.exp(s - m_new).astype(jnp.bfloat16)
        return m_new, jnp.exp(m - m_new) * acc + jnp.dot(p, v, preferred_element_type=jnp.float32)

    def scores(q, k):
        return lax.dot_general(q, k, (((1,), (1,)), ((), ())), preferred_element_type=jnp.float32)

    def step(j, carry):
        start = pl.multiple_of(j * tq, tq)
        return tuple(update(scores(qs[hh], k_ref[hh, pl.ds(start, tq), :]), *carry[hh],
                            v_ref[hh, pl.ds(start, tq), :]) for hh in range(2))

    def diagonal(carry):
        start = pl.multiple_of(qi * tq, tq)
        band = tq // FOX_DIAG_BANDS
        new = []
        for hh in range(2):
            m, acc = carry[hh]
            parts = []
            for r in range(FOX_DIAG_BANDS):
                lo, hi = r * band, (r + 1) * band
                s = scores(qs[hh][lo:hi], k_ref[hh, pl.ds(start, hi), :])
                row = lax.broadcasted_iota(jnp.int32, (band, hi), 0) + lo
                col = lax.broadcasted_iota(jnp.int32, (band, hi), 1)
                s = jnp.where(col <= row, s, -jnp.inf)
                parts.append(update(s, m[lo:hi], acc[lo:hi], v_ref[hh, pl.ds(start, hi), :]))
            new.append((jnp.concatenate([p[0] for p in parts], axis=0),
                        jnp.concatenate([p[1] for p in parts], axis=0)))
        return tuple(new)

    init1 = (jnp.full((tq, 1), -jnp.inf, jnp.float32), jnp.zeros((tq, LANES), jnp.float32))
    carry = diagonal(lax.fori_loop(0, qi, step, (init1, init1)))
    outs = [acc / acc[:, AUG:AUG + 1] for _, acc in carry]
    lane = lax.broadcasted_iota(jnp.int32, (tq, LANES), 1)
    o_ref[...] = jnp.where(lane < HEAD_DIM, outs[0], pltpu.roll(outs[1], HEAD_DIM, axis=1)).astype(o_ref.dtype)


def _fox(qf, kf, vf, batch, seq_len):
    t = qf.shape[1]
    tq = min(FOX_TILE, seq_len)
    assert seq_len % tq == 0 and tq % FOX_DIAG_BANDS == 0
    nq = seq_len // tq
    pairs = FOX_HEADS // 2
    seq = pl.BlockSpec((2, seq_len, LANES), lambda b, hp, i: (hp, b, 0))
    return pl.pallas_call(
        _fox_kernel,
        grid=(batch, pairs, nq),
        in_specs=[pl.BlockSpec((2, tq, LANES), lambda b, hp, i: (hp, b * nq + i, 0)), seq, seq],
        out_specs=pl.BlockSpec((tq, LANES), lambda b, hp, i: (b * nq + i, hp)),
        out_shape=jax.ShapeDtypeStruct((t, FOX_W), jnp.bfloat16),
        compiler_params=_cparams(("arbitrary", "arbitrary", "arbitrary")),
        name="fox",
    )(qf, kf, vf)


def _layer_norm(h, g, b):
    mu = jnp.mean(h, axis=-1, keepdims=True)
    d = h - mu
    var = jnp.mean(d * d, axis=-1, keepdims=True)
    return d * lax.rsqrt(var + LN_EPS) * g + b


def _pack_rows(ref, x):
    rows = x.shape[0]
    bits = pltpu.bitcast(x.astype(jnp.bfloat16).astype(jnp.float32), jnp.uint32)
    for c in range(ROW_WORDS):
        lo = bits[:, c * LANES:(c + 1) * LANES] >> 16
        hi = bits[:, HALF_D + c * LANES:HALF_D + (c + 1) * LANES] & jnp.uint32(HI_MASK)
        ref[pl.ds(c, rows, stride=ROW_WORDS), :] = lo | hi


def _unpack_rows(ref, first, rows):
    los, his = [], []
    for c in range(ROW_WORDS):
        w = ref[pl.ds(first * ROW_WORDS + c, rows, stride=ROW_WORDS), :]
        los.append(pltpu.bitcast(w << 16, jnp.float32))
        his.append(pltpu.bitcast(w & jnp.uint32(HI_MASK), jnp.float32))
    return jnp.concatenate(los + his, axis=1)


def _packed(ref, first, rows=1):
    return ref.at[pl.ds(pl.multiple_of(first * ROW_WORDS, ROW_WORDS), rows * ROW_WORDS), :]


def _packed_spec(rows, index_map):
    return pl.BlockSpec((rows * ROW_WORDS, LANES), index_map)


def _packed_shape(rows):
    return jax.ShapeDtypeStruct((rows * ROW_WORDS, LANES), jnp.uint32)


def _merge_kernel(alpha, x_ref, ya_ref, yb_ref, ga_ref, gb_ref, wa_ref, wb_ref, wo_ref, g_ref, b_ref,
                  x1_ref, x1p_ref):
    pa = jnp.dot(ya_ref[...], wa_ref[...], preferred_element_type=jnp.float32)
    pb = jnp.dot(yb_ref[...], wb_ref[...], preferred_element_type=jnp.float32)
    mix = _sigmoid(ga_ref[...]) * pa + _sigmoid(gb_ref[...]) * pb
    out = jnp.dot(mix.astype(jnp.bfloat16), wo_ref[...], preferred_element_type=jnp.float32)
    x1 = _layer_norm(alpha * x_ref[...] + out, g_ref[...], b_ref[...])
    x1_ref[...] = x1
    _pack_rows(x1p_ref, x1)


def _merge(alpha, x2, ya, yb, ga, gb, wa, wb, wo, g, b):
    t = x2.shape[0]
    tm = TM_MERGE
    row = lambda n: pl.BlockSpec((tm, n), lambda i: (i, 0))
    full = lambda a: pl.BlockSpec(a.shape, lambda i: (0,) * a.ndim)
    return pl.pallas_call(
        functools.partial(_merge_kernel, alpha),
        grid=(t // tm,),
        in_specs=[row(D_MODEL), row(SWA_Q), row(FOX_W), row(D_MODEL), row(D_MODEL),
                  full(wa), full(wb), full(wo), full(g), full(b)],
        out_specs=(row(D_MODEL), _packed_spec(tm, lambda i: (i, 0))),
        out_shape=(jax.ShapeDtypeStruct((t, D_MODEL), jnp.float32), _packed_shape(t)),
        compiler_params=_cparams(("arbitrary",)),
        name="merge",
    )(x2, ya, yb, ga, gb, wa, wb, wo, g, b)


def _first_argmax(v, idx):
    m = jnp.max(v, axis=0, keepdims=True)
    first = jnp.min(jnp.where(v == m, idx, v.shape[0]), axis=0, keepdims=True)
    return m, first


def _router_kernel(x_ref, wr_ref, rb_ref, tri_ref, e_ref, w_ref, r_ref, cnt_ref, carry_ref):
    i = pl.program_id(0)
    tm = x_ref.shape[0]

    @pl.when(i == 0)
    def _():
        carry_ref[...] = jnp.zeros_like(carry_ref)

    xb = x_ref[...].astype(jnp.bfloat16)
    logits = lax.dot_general(wr_ref[...], xb, (((1,), (1,)), ((), ())),
                             preferred_element_type=jnp.float32)
    scores = _sigmoid(logits)
    biased = scores + rb_ref[...]

    in_group = lax.broadcasted_iota(jnp.int32, (GROUP_SIZE, tm), 0)
    grows = []
    for g in range(N_GROUPS):
        blk = biased[g * GROUP_SIZE:(g + 1) * GROUP_SIZE, :]
        m1, f1 = _first_argmax(blk, in_group)
        m2 = jnp.max(jnp.where(in_group == f1, -jnp.inf, blk), axis=0, keepdims=True)
        grows.append(m1 + m2)
    gscore = jnp.concatenate(grows, axis=0)

    gidx = lax.broadcasted_iota(jnp.int32, (N_GROUPS, tm), 0)
    gwork = gscore
    gsel = jnp.zeros((N_GROUPS, tm), jnp.float32)
    for _ in range(TOPK_GROUPS):
        _, first = _first_argmax(gwork, gidx)
        hit = gidx == first
        gsel = jnp.where(hit, 1.0, gsel)
        gwork = jnp.where(hit, -jnp.inf, gwork)

    eidx = lax.broadcasted_iota(jnp.int32, (N_EXPERTS, tm), 0)
    work = jnp.concatenate(
        [jnp.where(jnp.broadcast_to(gsel[g:g + 1, :], (GROUP_SIZE, tm)) > 0.0,
                   biased[g * GROUP_SIZE:(g + 1) * GROUP_SIZE, :], -jnp.inf) for g in range(N_GROUPS)], axis=0)
    picks = []
    sel = jnp.zeros((N_EXPERTS, tm), jnp.bool_)
    for _ in range(TOP_K):
        _, first = _first_argmax(work, eidx)
        hit = eidx == first
        picks.append((first, hit))
        sel = sel | hit
        work = jnp.where(hit, -jnp.inf, work)

    wsum = jnp.sum(jnp.where(sel, scores, 0.0), axis=0, keepdims=True)
    sel_b = jnp.where(sel, 1.0, 0.0).astype(jnp.bfloat16)
    before = carry_ref[...] + jnp.dot(sel_b, tri_ref[...], preferred_element_type=jnp.float32)
    e_rows, w_rows, r_rows = [], [], []
    for first, hit in picks:
        e_rows.append(first)
        sc = jnp.sum(jnp.where(hit, scores, 0.0), axis=0, keepdims=True)
        w_rows.append(sc / wsum * ROUTED_SCALE)
        r_rows.append(jnp.sum(jnp.where(hit, before, 0.0), axis=0, keepdims=True).astype(jnp.int32))
    e_ref[...] = jnp.concatenate(e_rows, axis=0)
    w_ref[...] = jnp.concatenate(w_rows, axis=0)
    r_ref[...] = jnp.concatenate(r_rows, axis=0)
    total = carry_ref[...] + jnp.sum(sel_b.astype(jnp.float32), axis=1, keepdims=True)
    carry_ref[...] = total
    cnt_ref[...] = jnp.broadcast_to(total, cnt_ref.shape).astype(jnp.int32)


def _router(x1, wr_t, rb_col):
    t = x1.shape[0]
    tm = TM_ROUTE
    tri = jnp.triu(jnp.ones((tm, tm), jnp.bfloat16), k=1)
    full = lambda a: pl.BlockSpec(a.shape, lambda i: (0,) * a.ndim)
    kt = pl.BlockSpec((TOP_K, tm), lambda i: (0, i))
    return pl.pallas_call(
        _router_kernel,
        grid=(t // tm,),
        in_specs=[pl.BlockSpec((tm, D_MODEL), lambda i: (i, 0)), full(wr_t), full(rb_col), full(tri)],
        out_specs=(kt, kt, kt, pl.BlockSpec((N_EXPERTS, LANES), lambda i: (0, 0))),
        out_shape=(jax.ShapeDtypeStruct((TOP_K, t), jnp.int32), jax.ShapeDtypeStruct((TOP_K, t), jnp.float32),
                   jax.ShapeDtypeStruct((TOP_K, t), jnp.int32),
                   jax.ShapeDtypeStruct((N_EXPERTS, LANES), jnp.int32)),
        scratch_shapes=[pltpu.VMEM((N_EXPERTS, 1), jnp.float32)],
        compiler_params=_cparams(("arbitrary",)),
        name="router",
    )(x1, wr_t, rb_col, tri)


def _plan_kernel(ps_ref, e_ref, r_ref, d_ref):
    tm = e_ref.shape[1]
    eidx = lax.broadcasted_iota(jnp.int32, (N_EXPERTS, tm), 0)
    rows = []
    for k in range(TOP_K):
        hit = eidx == e_ref[k:k + 1, :]
        base = jnp.sum(jnp.where(hit, ps_ref[...], 0.0), axis=0, keepdims=True)
        rows.append(base.astype(jnp.int32) + r_ref[k:k + 1, :])
    d_ref[...] = jnp.concatenate(rows, axis=0)


def _plan(pstart_col, top_e, rank):
    t = top_e.shape[1]
    tm = min(TM_PLAN, t)
    kt = pl.BlockSpec((TOP_K, tm), lambda i: (0, i))
    return pl.pallas_call(
        _plan_kernel,
        grid=(t // tm,),
        in_specs=[pl.BlockSpec(pstart_col.shape, lambda i: (0, 0)), kt, kt],
        out_specs=kt,
        out_shape=jax.ShapeDtypeStruct((TOP_K, t), jnp.int32),
        compiler_params=_cparams(("arbitrary",)),
        name="plan",
    )(pstart_col, top_e, rank)


def _dispatch_kernel(pend_ref, padded_ref, nu_ref, dest_hbm, x_ref, xs_hbm, idx_smem, zero_vmem, sem, zsem):
    i = pl.program_id(0)
    n_blocks = xs_hbm.shape[0] // (EXPERT_BLOCK * ROW_WORDS)

    def zero_copy(first_row):
        return pltpu.make_async_copy(zero_vmem, _packed(xs_hbm, first_row, EXPERT_BLOCK), zsem)

    @pl.when(i == 0)
    def _():
        zero_vmem[...] = jnp.zeros_like(zero_vmem)

        def fill(e, c):
            @pl.when(padded_ref[e] > 0)
            def _():
                zero_copy(pend_ref[e] - EXPERT_BLOCK).start()
            return c
        lax.fori_loop(0, N_EXPERTS, fill, 0)

        def fill_tail(b, c):
            zero_copy(b * EXPERT_BLOCK).start()
            return c
        lax.fori_loop(nu_ref[0], n_blocks, fill_tail, 0)

        def drain(e, c):
            @pl.when(padded_ref[e] > 0)
            def _():
                zero_copy(0).wait()
            return c
        lax.fori_loop(0, N_EXPERTS, drain, 0)

        def drain_tail(b, c):
            zero_copy(0).wait()
            return c
        lax.fori_loop(nu_ref[0], n_blocks, drain_tail, 0)

    n_idx = TM_DISPATCH * TOP_K
    pltpu.sync_copy(dest_hbm.at[pl.ds(i * n_idx, n_idx)], idx_smem)

    def issue(t, c):
        for k in range(TOP_K):
            pltpu.make_async_copy(_packed(x_ref, t), _packed(xs_hbm, idx_smem[t * TOP_K + k]),
                                  sem).start(priority=k % 2)
        return c
    lax.fori_loop(0, TM_DISPATCH, issue, 0)

    for k in range(TOP_K):
        pltpu.make_async_copy(x_ref, _packed(xs_hbm, 0, TM_DISPATCH), sem).wait()


def _dispatch(dest_flat, x1p, pend, padded, n_used, n_rows):
    t = x1p.shape[0] // ROW_WORDS
    grid_spec = pltpu.PrefetchScalarGridSpec(
        num_scalar_prefetch=3,
        grid=(t // TM_DISPATCH,),
        in_specs=[pl.BlockSpec(memory_space=pl.ANY), _packed_spec(TM_DISPATCH, lambda i, pe, pa, nu: (i, 0))],
        out_specs=pl.BlockSpec(memory_space=pl.ANY),
        scratch_shapes=[pltpu.SMEM((TM_DISPATCH * TOP_K,), jnp.int32),
                        pltpu.VMEM((EXPERT_BLOCK * ROW_WORDS, LANES), jnp.uint32),
                        pltpu.SemaphoreType.DMA, pltpu.SemaphoreType.DMA])
    return pl.pallas_call(
        _dispatch_kernel,
        grid_spec=grid_spec,
        out_shape=_packed_shape(n_rows),
        compiler_params=_cparams(("arbitrary",)),
        name="dispatch",
    )(pend, padded, n_used, dest_flat, x1p)


def _experts_kernel(first_ref, nblk_ref, nu_ref, xs_hbm, wg_ref, wu_ref, wd_ref, y_hbm,
                    xbuf, ybuf, wg_b, wu_b, wd_b, in_sem, out_sem):
    nc = EXPERT_BLOCK // EXPERT_SUB
    e = pl.program_id(0)
    n_used = nu_ref[0]
    n_blocks = y_hbm.shape[0] // (EXPERT_BLOCK * ROW_WORDS)
    first = first_ref[e]

    def slot_rows(ref, slot):
        return _packed(ref, slot * EXPERT_BLOCK, EXPERT_BLOCK)

    def fetch(b):
        return pltpu.make_async_copy(_packed(xs_hbm, b * EXPERT_BLOCK, EXPERT_BLOCK),
                                     slot_rows(xbuf, b % EXPERT_SLOTS), in_sem.at[b % EXPERT_SLOTS])

    def write_back(b):
        return pltpu.make_async_copy(slot_rows(ybuf, b % EXPERT_SLOTS),
                                     _packed(y_hbm, b * EXPERT_BLOCK, EXPERT_BLOCK), out_sem.at[b % EXPERT_SLOTS])

    @pl.when(e == 0)
    def _():
        for b in range(EXPERT_SLOTS - 1):
            @pl.when(b < n_used)
            def _(b=b):
                fetch(b).start()

    @pl.when(nblk_ref[e] > 0)
    def _():
        wg_b[...] = wg_ref[...].astype(jnp.bfloat16)
        wu_b[...] = wu_ref[...].astype(jnp.bfloat16)
        wd_b[...] = wd_ref[...].astype(jnp.bfloat16)

    def block(j, carry):
        b = first + j
        slot = b % EXPERT_SLOTS

        @pl.when(b + EXPERT_SLOTS - 1 < n_used)
        def _():
            fetch(b + EXPERT_SLOTS - 1).start()

        fetch(b).wait()

        @pl.when(b >= EXPERT_SLOTS)
        def _():
            write_back(b - EXPERT_SLOTS).wait()

        base = slot * EXPERT_BLOCK
        xbs = [_unpack_rows(xbuf, base + c * EXPERT_SUB, EXPERT_SUB).astype(jnp.bfloat16) for c in range(nc)]
        gus = [(jnp.dot(xb, wg_b[...], preferred_element_type=jnp.float32),
                jnp.dot(xb, wu_b[...], preferred_element_type=jnp.float32)) for xb in xbs]
        ys = []
        for g, u in gus:
            h = (g * _sigmoid(g) * u).astype(jnp.bfloat16)
            ys.append(jnp.dot(h, wd_b[...], preferred_element_type=jnp.float32))
        for c, y in enumerate(ys):
            _pack_rows(_packed(ybuf, base + c * EXPERT_SUB, EXPERT_SUB), y)
        write_back(b).start()
        return carry

    lax.fori_loop(0, nblk_ref[e], block, 0)

    @pl.when(e == pl.num_programs(0) - 1)
    def _():
        for back in range(EXPERT_SLOTS, 0, -1):
            @pl.when(n_used >= back)
            def _(back=back):
                write_back(n_used - back).wait()

        xbuf[...] = jnp.zeros_like(xbuf)

        def zero_copy(b):
            return pltpu.make_async_copy(slot_rows(xbuf, 0), _packed(y_hbm, b * EXPERT_BLOCK, EXPERT_BLOCK),
                                         out_sem.at[0])

        def fill(b, c):
            zero_copy(b).start()
            return c
        lax.fori_loop(n_used, n_blocks, fill, 0)

        def drain(b, c):
            zero_copy(b).wait()
            return c
        lax.fori_loop(n_used, n_blocks, drain, 0)


def _experts(layer, first_blk, n_blk, n_used, xs, w_gate_e, w_up_e, w_down_e):
    wsel = lambda e, fb, nb, nu: (layer, e, 0, 0)
    slots = pltpu.VMEM((EXPERT_SLOTS * EXPERT_BLOCK * ROW_WORDS, LANES), jnp.uint32)
    grid_spec = pltpu.PrefetchScalarGridSpec(
        num_scalar_prefetch=3,
        grid=(N_EXPERTS,),
        in_specs=[pl.BlockSpec(memory_space=pl.ANY),
                  pl.BlockSpec((None, None, D_MODEL, D_EXPERT), wsel),
                  pl.BlockSpec((None, None, D_MODEL, D_EXPERT), wsel),
                  pl.BlockSpec((None, None, D_EXPERT, D_MODEL), wsel)],
        out_specs=pl.BlockSpec(memory_space=pl.ANY),
        scratch_shapes=[slots, slots,
                        pltpu.VMEM((D_MODEL, D_EXPERT), jnp.bfloat16), pltpu.VMEM((D_MODEL, D_EXPERT), jnp.bfloat16),
                        pltpu.VMEM((D_EXPERT, D_MODEL), jnp.bfloat16),
                        pltpu.SemaphoreType.DMA((EXPERT_SLOTS,)), pltpu.SemaphoreType.DMA((EXPERT_SLOTS,))])
    return pl.pallas_call(
        _experts_kernel,
        grid_spec=grid_spec,
        out_shape=jax.ShapeDtypeStruct(xs.shape, jnp.uint32),
        compiler_params=_cparams(("arbitrary",)),
        name="experts",
    )(first_blk, n_blk, n_used, xs, w_gate_e, w_up_e, w_down_e)


def _combine_kernel(alpha, dest_hbm, y_hbm, w_ref, x1_ref, wg_ref, wu_ref, wd_ref, g_ref, b_ref,
                    o_ref, idx_smem0, idx_smem1, buf, sem):
    i = pl.program_id(0)
    n = pl.num_programs(0)
    n_idx = TM_COMBINE * TOP_K

    def slab(slot, k):
        return (slot * TOP_K + k) * TM_COMBINE

    def start_gather(step, slot):
        idx_smem = (idx_smem0, idx_smem1)[slot]
        pltpu.sync_copy(dest_hbm.at[pl.ds(step * n_idx, n_idx)], idx_smem)

        def issue(t, c):
            for k in range(TOP_K):
                pltpu.make_async_copy(_packed(y_hbm, idx_smem[t * TOP_K + k]),
                                      _packed(buf, slab(slot, k) + t), sem.at[slot]).start(priority=k % 2)
            return c
        lax.fori_loop(0, TM_COMBINE, issue, 0)

    @pl.when(i == 0)
    def _():
        start_gather(0, 0)

    def run(slot):
        @pl.when(i + 1 < n)
        def _():
            start_gather(i + 1, 1 - slot)

        for k in range(TOP_K):
            pltpu.make_async_copy(_packed(y_hbm, 0, TM_COMBINE), _packed(buf, slab(slot, k), TM_COMBINE),
                                  sem.at[slot]).wait()

        w = w_ref[...]
        routed = jnp.zeros((TM_COMBINE, D_MODEL), jnp.float32)
        for k in range(TOP_K):
            routed = routed + w[:, k:k + 1] * _unpack_rows(buf, slab(slot, k), TM_COMBINE)

        x1 = x1_ref[...]
        xb = x1.astype(jnp.bfloat16)
        gs = jnp.dot(xb, wg_ref[...], preferred_element_type=jnp.float32)
        us = jnp.dot(xb, wu_ref[...], preferred_element_type=jnp.float32)
        hs = gs * _sigmoid(gs) * us
        shared = jnp.dot(hs.astype(jnp.bfloat16), wd_ref[...], preferred_element_type=jnp.float32)
        o_ref[...] = _layer_norm(alpha * x1 + (shared + routed), g_ref[...], b_ref[...])

    for slot in range(2):
        pl.when(i % 2 == slot)(functools.partial(run, slot))


def _combine(alpha, dest_flat, y, w_tk, x1, wg, wu, wd, g, b):
    t = x1.shape[0]
    tm = TM_COMBINE
    full = lambda a: pl.BlockSpec(a.shape, lambda i: (0,) * a.ndim)
    return pl.pallas_call(
        functools.partial(_combine_kernel, alpha),
        grid=(t // tm,),
        in_specs=[pl.BlockSpec(memory_space=pl.ANY), pl.BlockSpec(memory_space=pl.ANY),
                  pl.BlockSpec((tm, TOP_K), lambda i: (i, 0)), pl.BlockSpec((tm, D_MODEL), lambda i: (i, 0)),
                  full(wg), full(wu), full(wd), full(g), full(b)],
        out_specs=pl.BlockSpec((tm, D_MODEL), lambda i: (i, 0)),
        out_shape=jax.ShapeDtypeStruct((t, D_MODEL), jnp.float32),
        scratch_shapes=[pltpu.SMEM((tm * TOP_K,), jnp.int32), pltpu.SMEM((tm * TOP_K,), jnp.int32),
                        pltpu.VMEM((2 * TOP_K * tm * ROW_WORDS, LANES), jnp.uint32),
                        pltpu.SemaphoreType.DMA((2,))],
        compiler_params=_cparams(("arbitrary",)),
        name="combine",
    )(dest_flat, y, w_tk, x1, wg, wu, wd, g, b)


def _prep_in_weights(w, b_forget):
    o = np.cumsum((0,) + IN_SIZES)
    group = SWA_Q_HEADS // SWA_KV_HEADS
    qa = w[:, o[0]:o[1]].reshape(D_MODEL, SWA_KV_HEADS, group, HEAD_DIM)
    qa = qa.transpose(0, 2, 1, 3).reshape(D_MODEL, SWA_Q)
    zf = jnp.pad(w[:, o[6]:o[7]], ((0, 0), (0, LANES - FOX_HEADS)))
    w_main = jnp.concatenate([qa, w[:, o[1]:o[6]], w[:, o[7]:o[9]], zf], axis=1).astype(jnp.bfloat16)
    bf_row = jnp.pad(b_forget, (0, LANES - FOX_HEADS)).reshape(1, LANES).astype(jnp.float32)
    return w_main, bf_row


def _prep_proj_a(w):
    group = SWA_Q_HEADS // SWA_KV_HEADS
    w4 = w.reshape(SWA_KV_HEADS, group, HEAD_DIM, D_MODEL).transpose(1, 0, 2, 3)
    return w4.reshape(SWA_Q, D_MODEL).astype(jnp.bfloat16)


def _route_plan(counts, n_tokens):
    n_assign = n_tokens * TOP_K
    n_rows = (n_assign + N_EXPERTS * (EXPERT_BLOCK - 1) + EXPERT_BLOCK - 1) // EXPERT_BLOCK * EXPERT_BLOCK
    padded = (counts + EXPERT_BLOCK - 1) // EXPERT_BLOCK * EXPERT_BLOCK
    pend = jnp.cumsum(padded)
    pstart = pend - padded
    n_used = pend[-1] // EXPERT_BLOCK
    i32 = jnp.int32
    return (pstart.astype(jnp.float32).reshape(N_EXPERTS, 1), (pstart // EXPERT_BLOCK).astype(i32),
            (padded // EXPERT_BLOCK).astype(i32), n_used.reshape(1).astype(i32), pend.astype(i32),
            padded.astype(i32), n_rows)


def kernel(x, w_in, b_forget, attn_sinks, rel_bias, w_proj_a, w_proj_b, w_out, ln1_g, ln1_b,
           w_router, router_bias, w_gate_e, w_up_e, w_down_e, w_gate_s, w_up_s, w_down_s, ln2_g, ln2_b):
    batch, seq_len, d = x.shape
    depth = w_in.shape[0]
    alpha = (2 * depth) ** 0.25
    t = batch * seq_len
    bf16 = jnp.bfloat16
    x2 = x.reshape(t, d)
    bias_tab = _t5_bias_table(rel_bias)
    for l in range(depth):
        w_main, bf_row = _prep_in_weights(w_in[l], b_forget[l])
        qa, ka, va, qf, kf, vf, ga, gb = _in_proj(x2, w_main, bf_row, seq_len)
        ya = _swa(qa, ka, va, attn_sinks[l].astype(jnp.float32), bias_tab, seq_len)
        yb = _fox(qf, kf, vf, batch, seq_len)
        x1, x1p = _merge(alpha, x2, ya, yb, ga, gb, _prep_proj_a(w_proj_a[l]), w_proj_b[l].astype(bf16),
                         w_out[l].astype(bf16), ln1_g[l].reshape(1, d), ln1_b[l].reshape(1, d))
        top_e, w_kt, rank, counts = _router(x1, w_router[l].T.astype(bf16),
                                            router_bias[l].reshape(N_EXPERTS, 1).astype(jnp.float32))
        pstart_col, first_blk, n_blk, n_used, pend, padded, n_rows = _route_plan(counts[:, 0], t)
        dest_flat = _plan(pstart_col, top_e, rank).T.reshape(-1)
        xs = _dispatch(dest_flat, x1p, pend, padded, n_used, n_rows)
        y = _experts(l, first_blk, n_blk, n_used, xs, w_gate_e, w_up_e, w_down_e)
        x2 = _combine(alpha, dest_flat, y, w_kt.T, x1, w_gate_s[l].astype(bf16), w_up_s[l].astype(bf16),
                      w_down_s[l].astype(bf16), ln2_g[l].reshape(1, d), ln2_b[l].reshape(1, d))
    return x2.reshape(batch, seq_len, d)
```

```python
import functools
import math

import jax
import jax.numpy as jnp
import numpy as np
from jax import lax
from jax.experimental import pallas as pl
from jax.experimental.pallas import tpu as pltpu

D_MODEL = 1024
HEAD_DIM = 64
SWA_Q_HEADS = 8
SWA_KV_HEADS = 2
SWA_WINDOW = 128
FOX_HEADS = 8
Q_BLOCK = 128
N_BUCKETS = 32
MAX_DISTANCE = 128
N_EXPERTS = 256
TOP_K = 8
N_GROUPS = 8
TOPK_GROUPS = 4
GROUP_SIZE = N_EXPERTS // N_GROUPS
D_EXPERT = 256
D_SHARED = 256
ROUTED_SCALE = 2.5
LN_EPS = 1e-5

SWA_Q = SWA_Q_HEADS * HEAD_DIM
SWA_KV = SWA_KV_HEADS * HEAD_DIM
FOX_W = FOX_HEADS * HEAD_DIM
IN_SIZES = (SWA_Q, SWA_KV, SWA_KV, FOX_W, FOX_W, FOX_W, FOX_HEADS, D_MODEL, D_MODEL)

LANES = 128
HALF_D = D_MODEL // 2
ROW_WORDS = HALF_D // LANES
VMEM_LIMIT = 56 * 1024 * 1024
HI_MASK = 0xFFFF0000

TM_PROJ = 512
TM_MERGE = 512
TM_ROUTE = 256
TM_DISPATCH = 1024
TM_COMBINE = 512
TM_PLAN = 2048
EXPERT_BLOCK = 512
EXPERT_SUB = 512
EXPERT_SLOTS = 6
SWA_STEP = 4
FOX_TILE = 1024
FOX_DIAG_BANDS = 2

AUG = HEAD_DIM


def _cparams(sem):
    return pltpu.CompilerParams(dimension_semantics=sem, vmem_limit_bytes=VMEM_LIMIT)


def _sigmoid(x):
    return 1.0 / (1.0 + jnp.exp(-x))


def _split3(c):
    hi = c.astype(jnp.bfloat16).astype(jnp.float32)
    r = c - hi
    mid = r.astype(jnp.bfloat16).astype(jnp.float32)
    lo = (r - mid).astype(jnp.bfloat16).astype(jnp.float32)
    return hi, mid, lo


_C_QA = 0
_C_KA = _C_QA + SWA_Q
_C_VA = _C_KA + SWA_KV
_C_QF = _C_VA + SWA_KV
_C_KF = _C_QF + FOX_W
_C_VF = _C_KF + FOX_W
_C_GA = _C_VF + FOX_W
_C_GB = _C_GA + D_MODEL
_C_ZF = _C_GB + D_MODEL
_N_PROJ = _C_ZF + LANES


def _in_proj_kernel(tiles_per_seq, x_ref, w_ref, bf_ref, tri_ref,
                    qa_ref, ka_ref, va_ref, qf_ref, kf_ref, vf_ref, ga_ref, gb_ref, carry_ref):
    i = pl.program_id(0)
    tm = x_ref.shape[0]
    xb = x_ref[...].astype(jnp.bfloat16)

    def proj(c0, n):
        return jnp.dot(xb, w_ref[:, c0:c0 + n], preferred_element_type=jnp.float32)

    scale = HEAD_DIM ** -0.5
    qa_ref[...] = (proj(_C_QA, SWA_Q) * scale).astype(jnp.bfloat16)
    ka_ref[...] = proj(_C_KA, SWA_KV).astype(jnp.bfloat16)
    va_ref[...] = proj(_C_VA, SWA_KV).astype(jnp.bfloat16)
    ga_ref[...] = proj(_C_GA, D_MODEL)
    gb_ref[...] = proj(_C_GB, D_MODEL)

    z = proj(_C_ZF, LANES) + bf_ref[...]
    log_f = jnp.minimum(z, 0.0) - jnp.log(1.0 + jnp.exp(-jnp.abs(z)))

    @pl.when(i % tiles_per_seq == 0)
    def _():
        carry_ref[...] = jnp.zeros_like(carry_ref)

    tri = tri_ref[...]
    h3 = _split3(log_f)
    c = carry_ref[...]
    for part in h3:
        c = c + jnp.dot(tri, part.astype(jnp.bfloat16), preferred_element_type=jnp.float32)
    carry_ref[...] = c[tm - 1:tm, :]

    qf = proj(_C_QF, FOX_W) * scale
    kf = proj(_C_KF, FOX_W)
    vf = proj(_C_VF, FOX_W)
    lane = lax.broadcasted_iota(jnp.int32, (tm, LANES), 1)
    one = jnp.float32(1.0)
    zero = jnp.float32(0.0)
    for h in range(FOX_HEADS):
        pair = (h // 2) * LANES
        qp = qf[:, pair:pair + LANES]
        kp = kf[:, pair:pair + LANES]
        vp = vf[:, pair:pair + LANES]
        if h % 2 == 1:
            qp = pltpu.roll(qp, HEAD_DIM, axis=1)
            kp = pltpu.roll(kp, HEAD_DIM, axis=1)
            vp = pltpu.roll(vp, HEAD_DIM, axis=1)
        ch = jnp.broadcast_to(c[:, h:h + 1], (tm, LANES))
        hi, mid, lo = _split3(ch)
        q_aug = jnp.where(lane == AUG, hi, jnp.where(lane == AUG + 1, mid, jnp.where(
            lane == AUG + 2, lo, jnp.where(lane < AUG + 6, one, zero))))
        k_aug = jnp.where(lane < AUG + 3, one, jnp.where(lane == AUG + 3, -hi, jnp.where(
            lane == AUG + 4, -mid, jnp.where(lane == AUG + 5, -lo, zero))))
        qf_ref[h] = jnp.where(lane < AUG, qp, q_aug).astype(jnp.bfloat16)
        kf_ref[h] = jnp.where(lane < AUG, kp, k_aug).astype(jnp.bfloat16)
        vf_ref[h] = jnp.where(lane < AUG, vp, jnp.where(lane == AUG, one, zero)).astype(jnp.bfloat16)


def _in_proj(x2, w_main, bf_row, seq_len):
    t = x2.shape[0]
    tm = min(TM_PROJ, seq_len)
    tri = jnp.tril(jnp.ones((tm, tm), jnp.bfloat16))
    row = lambda n: pl.BlockSpec((tm, n), lambda i: (i, 0))
    full = lambda a: pl.BlockSpec(a.shape, lambda i: (0,) * a.ndim)
    slab = pl.BlockSpec((FOX_HEADS, tm, LANES), lambda i: (0, i, 0))
    bf16, f32 = jnp.bfloat16, jnp.float32
    slab_shape = jax.ShapeDtypeStruct((FOX_HEADS, t, LANES), bf16)
    out_shape = (
        jax.ShapeDtypeStruct((t, SWA_Q), bf16), jax.ShapeDtypeStruct((t, SWA_KV), bf16),
        jax.ShapeDtypeStruct((t, SWA_KV), bf16), slab_shape, slab_shape, slab_shape,
        jax.ShapeDtypeStruct((t, D_MODEL), f32), jax.ShapeDtypeStruct((t, D_MODEL), f32))
    return pl.pallas_call(
        functools.partial(_in_proj_kernel, seq_len // tm),
        grid=(t // tm,),
        in_specs=[row(D_MODEL), full(w_main), full(bf_row), full(tri)],
        out_specs=(row(SWA_Q), row(SWA_KV), row(SWA_KV), slab, slab, slab, row(D_MODEL), row(D_MODEL)),
        out_shape=out_shape,
        scratch_shapes=[pltpu.VMEM((1, LANES), f32)],
        compiler_params=_cparams(("arbitrary",)),
        name="in_proj",
    )(x2, w_main, bf_row, tri)


def _swa_kernel(steps_per_seq, sink_ref, q_ref, kp_ref, kc_ref, vp_ref, vc_ref, bias_ref, o_ref):
    n = pl.program_id(0)
    kall = jnp.concatenate([kp_ref[...], kc_ref[...]], axis=0)
    vall = jnp.concatenate([vp_ref[...], vc_ref[...]], axis=0)
    qi = lax.broadcasted_iota(jnp.int32, (Q_BLOCK, 2 * Q_BLOCK), 0)
    kj = lax.broadcasted_iota(jnp.int32, (Q_BLOCK, 2 * Q_BLOCK), 1)
    dist = qi + Q_BLOCK - kj
    in_window = (dist >= 0) & (dist < SWA_WINDOW)
    first_key = jnp.where((n % steps_per_seq) > 0, 0, Q_BLOCK)
    lane = lax.broadcasted_iota(jnp.int32, (Q_BLOCK, LANES), 1)
    lo_half = lane < HEAD_DIM
    group = SWA_Q_HEADS // SWA_KV_HEADS
    for b in range(SWA_STEP):
        rows = slice(b * Q_BLOCK, (b + 1) * Q_BLOCK)
        kk = kall[b * Q_BLOCK:(b + 2) * Q_BLOCK]
        vv = vall[b * Q_BLOCK:(b + 2) * Q_BLOCK]
        valid = (in_window & (kj >= first_key)) if b == 0 else in_window
        for j in range(group):
            qs = q_ref[rows, j * LANES:(j + 1) * LANES]
            outs = []
            for kv in range(SWA_KV_HEADS):
                head = kv * group + j
                qm = jnp.where(lo_half if kv == 0 else ~lo_half, qs, jnp.zeros_like(qs))
                s = lax.dot_general(qm, kk, (((1,), (1,)), ((), ())), preferred_element_type=jnp.float32)
                s = jnp.where(valid, s + bias_ref[head], -jnp.inf)
                sink = sink_ref[head]
                m = jnp.maximum(jnp.max(s, axis=-1, keepdims=True), sink)
                p = jnp.exp(s - m)
                denom = jnp.sum(p, axis=-1, keepdims=True) + jnp.exp(sink - m)
                pn = (p / denom).astype(jnp.bfloat16)
                outs.append(jnp.dot(pn, vv, preferred_element_type=jnp.float32))
            o_ref[rows, j * LANES:(j + 1) * LANES] = jnp.where(lo_half, outs[0], outs[1]).astype(o_ref.dtype)


def _swa(qa, ka, va, sinks, bias_tab, seq_len):
    t = qa.shape[0]
    tm = SWA_STEP * Q_BLOCK
    assert seq_len % tm == 0
    cur = lambda n, s: (n, 0)
    prev = lambda n, s: (jnp.maximum(n * SWA_STEP - 1, 0), 0)
    grid_spec = pltpu.PrefetchScalarGridSpec(
        num_scalar_prefetch=1,
        grid=(t // tm,),
        in_specs=[pl.BlockSpec((tm, SWA_Q), cur),
                  pl.BlockSpec((Q_BLOCK, SWA_KV), prev), pl.BlockSpec((tm, SWA_KV), cur),
                  pl.BlockSpec((Q_BLOCK, SWA_KV), prev), pl.BlockSpec((tm, SWA_KV), cur),
                  pl.BlockSpec(bias_tab.shape, lambda n, s: (0, 0, 0))],
        out_specs=pl.BlockSpec((tm, SWA_Q), cur))
    return pl.pallas_call(
        functools.partial(_swa_kernel, seq_len // tm),
        grid_spec=grid_spec,
        out_shape=jax.ShapeDtypeStruct((t, SWA_Q), jnp.bfloat16),
        compiler_params=_cparams(("arbitrary",)),
        name="swa",
    )(sinks, qa, ka, ka, va, va, bias_tab)


def _t5_bias_table(rel_bias):
    i = np.arange(Q_BLOCK)[:, None]
    j = np.arange(2 * Q_BLOCK)[None, :]
    n = jnp.asarray(np.maximum(i + Q_BLOCK - j, 0))
    max_exact = N_BUCKETS // 2
    nf = jnp.maximum(n, 1).astype(jnp.float32)
    large = max_exact + (jnp.log(nf / max_exact) / math.log(MAX_DISTANCE / max_exact)
                         * (N_BUCKETS - max_exact)).astype(jnp.int32)
    large = jnp.minimum(large, N_BUCKETS - 1)
    bucket = jnp.where(n < max_exact, n, large)
    onehot = bucket[None] == jnp.arange(N_BUCKETS)[:, None, None]
    rb = rel_bias.astype(jnp.float32)
    return jnp.sum(jnp.where(onehot[:, None], rb[:, :, None, None], 0.0), axis=0)


def _fox_kernel(q_ref, k_ref, v_ref, o_ref):
    qi = pl.program_id(2)
    tq = q_ref.shape[1]
    qs = (q_ref[0], q_ref[1])

    def update(s, m, acc, v):
        m_new = jnp.maximum(m, jnp.max(s, axis=-1, keepdims=True))
        p = jnp.exp(s - m_new).astype(jnp.bfloat16)
        return m_new, jnp.exp(m - m_new) * acc + jnp.dot(p, v, preferred_element_type=jnp.float32)

    def scores(q, k):
        return lax.dot_general(q, k, (((1,), (1,)), ((), ())), preferred_element_type=jnp.float32)

    def step(j, carry):
        start = pl.multiple_of(j * tq, tq)
        return tuple(update(scores(qs[hh], k_ref[hh, pl.ds(start, tq), :]), *carry[hh],
                            v_ref[hh, pl.ds(start, tq), :]) for hh in range(2))

    def diagonal(carry):
        start = pl.multiple_of(qi * tq, tq)
        band = tq // FOX_DIAG_BANDS
        new = []
        for hh in range(2):
            m, acc = carry[hh]
            parts = []
            for r in range(FOX_DIAG_BANDS):
                lo, hi = r * band, (r + 1) * band
                s = scores(qs[hh][lo:hi], k_ref[hh, pl.ds(start, hi), :])
                row = lax.broadcasted_iota(jnp.int32, (band, hi), 0) + lo
                col = lax.broadcasted_iota(jnp.int32, (band, hi), 1)
                s = jnp.where(col <= row, s, -jnp.inf)
                parts.append(update(s, m[lo:hi], acc[lo:hi], v_ref[hh, pl.ds(start, hi), :]))
            new.append((jnp.concatenate([p[0] for p in parts], axis=0),
                        jnp.concatenate([p[1] for p in parts], axis=0)))
        return tuple(new)

    init1 = (jnp.full((tq, 1), -jnp.inf, jnp.float32), jnp.zeros((tq, LANES), jnp.float32))
    carry = diagonal(lax.fori_loop(0, qi, step, (init1, init1)))
    outs = [acc / acc[:, AUG:AUG + 1] for _, acc in carry]
    lane = lax.broadcasted_iota(jnp.int32, (tq, LANES), 1)
    o_ref[...] = jnp.where(lane < HEAD_DIM, outs[0], pltpu.roll(outs[1], HEAD_DIM, axis=1)).astype(o_ref.dtype)


def _fox(qf, kf, vf, batch, seq_len):
    t = qf.shape[1]
    tq = min(FOX_TILE, seq_len)
    assert seq_len % tq == 0 and tq % FOX_DIAG_BANDS == 0
    nq = seq_len // tq
    pairs = FOX_HEADS // 2
    seq = pl.BlockSpec((2, seq_len, LANES), lambda b, hp, i: (hp, b, 0))
    return pl.pallas_call(
        _fox_kernel,
        grid=(batch, pairs, nq),
        in_specs=[pl.BlockSpec((2, tq, LANES), lambda b, hp, i: (hp, b * nq + i, 0)), seq, seq],
        out_specs=pl.BlockSpec((tq, LANES), lambda b, hp, i: (b * nq + i, hp)),
        out_shape=jax.ShapeDtypeStruct((t, FOX_W), jnp.bfloat16),
        compiler_params=_cparams(("arbitrary", "arbitrary", "arbitrary")),
        name="fox",
    )(qf, kf, vf)


def _layer_norm(h, g, b):
    mu = jnp.mean(h, axis=-1, keepdims=True)
    d = h - mu
    var = jnp.mean(d * d, axis=-1, keepdims=True)
    return d * lax.rsqrt(var + LN_EPS) * g + b


def _pack_rows(ref, x):
    rows = x.shape[0]
    bits = pltpu.bitcast(x.astype(jnp.bfloat16).astype(jnp.float32), jnp.uint32)
    for c in range(ROW_WORDS):
        lo = bits[:, c * LANES:(c + 1) * LANES] >> 16
        hi = bits[:, HALF_D + c * LANES:HALF_D + (c + 1) * LANES] & jnp.uint32(HI_MASK)
        ref[pl.ds(c, rows, stride=ROW_WORDS), :] = lo | hi


def _unpack_rows(ref, first, rows):
    los, his = [], []
    for c in range(ROW_WORDS):
        w = ref[pl.ds(first * ROW_WORDS + c, rows, stride=ROW_WORDS), :]
        los.append(pltpu.bitcast(w << 16, jnp.float32))
        his.append(pltpu.bitcast(w & jnp.uint32(HI_MASK), jnp.float32))
    return jnp.concatenate(los + his, axis=1)


def _packed(ref, first, rows=1):
    return ref.at[pl.ds(pl.multiple_of(first * ROW_WORDS, ROW_WORDS), rows * ROW_WORDS), :]


def _packed_spec(rows, index_map):
    return pl.BlockSpec((rows * ROW_WORDS, LANES), index_map)


def _packed_shape(rows):
    return jax.ShapeDtypeStruct((rows * ROW_WORDS, LANES), jnp.uint32)


def _merge_kernel(alpha, x_ref, ya_ref, yb_ref, ga_ref, gb_ref, wa_ref, wb_ref, wo_ref, g_ref, b_ref,
                  x1_ref, x1p_ref):
    pa = jnp.dot(ya_ref[...], wa_ref[...], preferred_element_type=jnp.float32)
    pb = jnp.dot(yb_ref[...], wb_ref[...], preferred_element_type=jnp.float32)
    mix = _sigmoid(ga_ref[...]) * pa + _sigmoid(gb_ref[...]) * pb
    out = jnp.dot(mix.astype(jnp.bfloat16), wo_ref[...], preferred_element_type=jnp.float32)
    x1 = _layer_norm(alpha * x_ref[...] + out, g_ref[...], b_ref[...])
    x1_ref[...] = x1
    _pack_rows(x1p_ref, x1)


def _merge(alpha, x2, ya, yb, ga, gb, wa, wb, wo, g, b):
    t = x2.shape[0]
    tm = TM_MERGE
    row = lambda n: pl.BlockSpec((tm, n), lambda i: (i, 0))
    full = lambda a: pl.BlockSpec(a.shape, lambda i: (0,) * a.ndim)
    return pl.pallas_call(
        functools.partial(_merge_kernel, alpha),
        grid=(t // tm,),
        in_specs=[row(D_MODEL), row(SWA_Q), row(FOX_W), row(D_MODEL), row(D_MODEL),
                  full(wa), full(wb), full(wo), full(g), full(b)],
        out_specs=(row(D_MODEL), _packed_spec(tm, lambda i: (i, 0))),
        out_shape=(jax.ShapeDtypeStruct((t, D_MODEL), jnp.float32), _packed_shape(t)),
        compiler_params=_cparams(("arbitrary",)),
        name="merge",
    )(x2, ya, yb, ga, gb, wa, wb, wo, g, b)


def _first_argmax(v, idx):
    m = jnp.max(v, axis=0, keepdims=True)
    first = jnp.min(jnp.where(v == m, idx, v.shape[0]), axis=0, keepdims=True)
    return m, first


def _router_kernel(x_ref, wr_ref, rb_ref, tri_ref, e_ref, w_ref, r_ref, cnt_ref, carry_ref):
    i = pl.program_id(0)
    tm = x_ref.shape[0]

    @pl.when(i == 0)
    def _():
        carry_ref[...] = jnp.zeros_like(carry_ref)

    xb = x_ref[...].astype(jnp.bfloat16)
    logits = lax.dot_general(wr_ref[...], xb, (((1,), (1,)), ((), ())),
                             preferred_element_type=jnp.float32)
    scores = _sigmoid(logits)
    biased = scores + rb_ref[...]

    in_group = lax.broadcasted_iota(jnp.int32, (GROUP_SIZE, tm), 0)
    grows = []
    for g in range(N_GROUPS):
        blk = biased[g * GROUP_SIZE:(g + 1) * GROUP_SIZE, :]
        m1, f1 = _first_argmax(blk, in_group)
        m2 = jnp.max(jnp.where(in_group == f1, -jnp.inf, blk), axis=0, keepdims=True)
        grows.append(m1 + m2)
    gscore = jnp.concatenate(grows, axis=0)

    gidx = lax.broadcasted_iota(jnp.int32, (N_GROUPS, tm), 0)
    gwork = gscore
    gsel = jnp.zeros((N_GROUPS, tm), jnp.float32)
    for _ in range(TOPK_GROUPS):
        _, first = _first_argmax(gwork, gidx)
        hit = gidx == first
        gsel = jnp.where(hit, 1.0, gsel)
        gwork = jnp.where(hit, -jnp.inf, gwork)

    eidx = lax.broadcasted_iota(jnp.int32, (N_EXPERTS, tm), 0)
    work = jnp.concatenate(
        [jnp.where(jnp.broadcast_to(gsel[g:g + 1, :], (GROUP_SIZE, tm)) > 0.0,
                   biased[g * GROUP_SIZE:(g + 1) * GROUP_SIZE, :], -jnp.inf) for g in range(N_GROUPS)], axis=0)
    picks = []
    sel = jnp.zeros((N_EXPERTS, tm), jnp.bool_)
    for _ in range(TOP_K):
        _, first = _first_argmax(work, eidx)
        hit = eidx == first
        picks.append((first, hit))
        sel = sel | hit
        work = jnp.where(hit, -jnp.inf, work)

    wsum = jnp.sum(jnp.where(sel, scores, 0.0), axis=0, keepdims=True)
    sel_b = jnp.where(sel, 1.0, 0.0).astype(jnp.bfloat16)
    before = carry_ref[...] + jnp.dot(sel_b, tri_ref[...], preferred_element_type=jnp.float32)
    e_rows, w_rows, r_rows = [], [], []
    for first, hit in picks:
        e_rows.append(first)
        sc = jnp.sum(jnp.where(hit, scores, 0.0), axis=0, keepdims=True)
        w_rows.append(sc / wsum * ROUTED_SCALE)
        r_rows.append(jnp.sum(jnp.where(hit, before, 0.0), axis=0, keepdims=True).astype(jnp.int32))
    e_ref[...] = jnp.concatenate(e_rows, axis=0)
    w_ref[...] = jnp.concatenate(w_rows, axis=0)
    r_ref[...] = jnp.concatenate(r_rows, axis=0)
    total = carry_ref[...] + jnp.sum(sel_b.astype(jnp.float32), axis=1, keepdims=True)
    carry_ref[...] = total
    cnt_ref[...] = jnp.broadcast_to(total, cnt_ref.shape).astype(jnp.int32)


def _router(x1, wr_t, rb_col):
    t = x1.shape[0]
    tm = TM_ROUTE
    tri = jnp.triu(jnp.ones((tm, tm), jnp.bfloat16), k=1)
    full = lambda a: pl.BlockSpec(a.shape, lambda i: (0,) * a.ndim)
    kt = pl.BlockSpec((TOP_K, tm), lambda i: (0, i))
    return pl.pallas_call(
        _router_kernel,
        grid=(t // tm,),
        in_specs=[pl.BlockSpec((tm, D_MODEL), lambda i: (i, 0)), full(wr_t), full(rb_col), full(tri)],
        out_specs=(kt, kt, kt, pl.BlockSpec((N_EXPERTS, LANES), lambda i: (0, 0))),
        out_shape=(jax.ShapeDtypeStruct((TOP_K, t), jnp.int32), jax.ShapeDtypeStruct((TOP_K, t), jnp.float32),
                   jax.ShapeDtypeStruct((TOP_K, t), jnp.int32),
                   jax.ShapeDtypeStruct((N_EXPERTS, LANES), jnp.int32)),
        scratch_shapes=[pltpu.VMEM((N_EXPERTS, 1), jnp.float32)],
        compiler_params=_cparams(("arbitrary",)),
        name="router",
    )(x1, wr_t, rb_col, tri)


def _plan_kernel(ps_ref, e_ref, r_ref, d_ref):
    tm = e_ref.shape[1]
    eidx = lax.broadcasted_iota(jnp.int32, (N_EXPERTS, tm), 0)
    rows = []
    for k in range(TOP_K):
        hit = eidx == e_ref[k:k + 1, :]
        base = jnp.sum(jnp.where(hit, ps_ref[...], 0.0), axis=0, keepdims=True)
        rows.append(base.astype(jnp.int32) + r_ref[k:k + 1, :])
    d_ref[...] = jnp.concatenate(rows, axis=0)


def _plan(pstart_col, top_e, rank):
    t = top_e.shape[1]
    tm = min(TM_PLAN, t)
    kt = pl.BlockSpec((TOP_K, tm), lambda i: (0, i))
    return pl.pallas_call(
        _plan_kernel,
        grid=(t // tm,),
        in_specs=[pl.BlockSpec(pstart_col.shape, lambda i: (0, 0)), kt, kt],
        out_specs=kt,
        out_shape=jax.ShapeDtypeStruct((TOP_K, t), jnp.int32),
        compiler_params=_cparams(("arbitrary",)),
        name="plan",
    )(pstart_col, top_e, rank)


def _dispatch_kernel(pend_ref, padded_ref, nu_ref, dest_hbm, x_ref, xs_hbm, idx_smem, zero_vmem, sem, zsem):
    i = pl.program_id(0)
    n_blocks = xs_hbm.shape[0] // (EXPERT_BLOCK * ROW_WORDS)

    def zero_copy(first_row):
        return pltpu.make_async_copy(zero_vmem, _packed(xs_hbm, first_row, EXPERT_BLOCK), zsem)

    @pl.when(i == 0)
    def _():
        zero_vmem[...] = jnp.zeros_like(zero_vmem)

        def fill(e, c):
            @pl.when(padded_ref[e] > 0)
            def _():
                zero_copy(pend_ref[e] - EXPERT_BLOCK).start()
            return c
        lax.fori_loop(0, N_EXPERTS, fill, 0)

        def fill_tail(b, c):
            zero_copy(b * EXPERT_BLOCK).start()
            return c
        lax.fori_loop(nu_ref[0], n_blocks, fill_tail, 0)

        def drain(e, c):
            @pl.when(padded_ref[e] > 0)
            def _():
                zero_copy(0).wait()
            return c
        lax.fori_loop(0, N_EXPERTS, drain, 0)

        def drain_tail(b, c):
            zero_copy(0).wait()
            return c
        lax.fori_loop(nu_ref[0], n_blocks, drain_tail, 0)

    n_idx = TM_DISPATCH * TOP_K
    pltpu.sync_copy(dest_hbm.at[pl.ds(i * n_idx, n_idx)], idx_smem)

    def issue(t, c):
        for k in range(TOP_K):
            pltpu.make_async_copy(_packed(x_ref, t), _packed(xs_hbm, idx_smem[t * TOP_K + k]),
                                  sem).start(priority=k % 2)
        return c
    lax.fori_loop(0, TM_DISPATCH, issue, 0)

    for k in range(TOP_K):
        pltpu.make_async_copy(x_ref, _packed(xs_hbm, 0, TM_DISPATCH), sem).wait()


def _dispatch(dest_flat, x1p, pend, padded, n_used, n_rows):
    t = x1p.shape[0] // ROW_WORDS
    grid_spec = pltpu.PrefetchScalarGridSpec(
        num_scalar_prefetch=3,
        grid=(t // TM_DISPATCH,),
        in_specs=[pl.BlockSpec(memory_space=pl.ANY), _packed_spec(TM_DISPATCH, lambda i, pe, pa, nu: (i, 0))],
        out_specs=pl.BlockSpec(memory_space=pl.ANY),
        scratch_shapes=[pltpu.SMEM((TM_DISPATCH * TOP_K,), jnp.int32),
                        pltpu.VMEM((EXPERT_BLOCK * ROW_WORDS, LANES), jnp.uint32),
                        pltpu.SemaphoreType.DMA, pltpu.SemaphoreType.DMA])
    return pl.pallas_call(
        _dispatch_kernel,
        grid_spec=grid_spec,
        out_shape=_packed_shape(n_rows),
        compiler_params=_cparams(("arbitrary",)),
        name="dispatch",
    )(pend, padded, n_used, dest_flat, x1p)


def _experts_kernel(first_ref, nblk_ref, nu_ref, xs_hbm, wg_ref, wu_ref, wd_ref, y_hbm,
                    xbuf, ybuf, wg_b, wu_b, wd_b, in_sem, out_sem):
    nc = EXPERT_BLOCK // EXPERT_SUB
    e = pl.program_id(0)
    n_used = nu_ref[0]
    n_blocks = y_hbm.shape[0] // (EXPERT_BLOCK * ROW_WORDS)
    first = first_ref[e]

    def slot_rows(ref, slot):
        return _packed(ref, slot * EXPERT_BLOCK, EXPERT_BLOCK)

    def fetch(b):
        return pltpu.make_async_copy(_packed(xs_hbm, b * EXPERT_BLOCK, EXPERT_BLOCK),
                                     slot_rows(xbuf, b % EXPERT_SLOTS), in_sem.at[b % EXPERT_SLOTS])

    def write_back(b):
        return pltpu.make_async_copy(slot_rows(ybuf, b % EXPERT_SLOTS),
                                     _packed(y_hbm, b * EXPERT_BLOCK, EXPERT_BLOCK), out_sem.at[b % EXPERT_SLOTS])

    @pl.when(e == 0)
    def _():
        for b in range(EXPERT_SLOTS - 1):
            @pl.when(b < n_used)
            def _(b=b):
                fetch(b).start()

    @pl.when(nblk_ref[e] > 0)
    def _():
        wg_b[...] = wg_ref[...].astype(jnp.bfloat16)
        wu_b[...] = wu_ref[...].astype(jnp.bfloat16)
        wd_b[...] = wd_ref[...].astype(jnp.bfloat16)

    def block(j, carry):
        b = first + j
        slot = b % EXPERT_SLOTS

        @pl.when(b + EXPERT_SLOTS - 1 < n_used)
        def _():
            fetch(b + EXPERT_SLOTS - 1).start()

        fetch(b).wait()

        @pl.when(b >= EXPERT_SLOTS)
        def _():
            write_back(b - EXPERT_SLOTS).wait()

        base = slot * EXPERT_BLOCK
        xbs = [_unpack_rows(xbuf, base + c * EXPERT_SUB, EXPERT_SUB).astype(jnp.bfloat16) for c in range(nc)]
        gus = [(jnp.dot(xb, wg_b[...], preferred_element_type=jnp.float32),
                jnp.dot(xb, wu_b[...], preferred_element_type=jnp.float32)) for xb in xbs]
        ys = []
        for g, u in gus:
            h = (g * _sigmoid(g) * u).astype(jnp.bfloat16)
            ys.append(jnp.dot(h, wd_b[...], preferred_element_type=jnp.float32))
        for c, y in enumerate(ys):
            _pack_rows(_packed(ybuf, base + c * EXPERT_SUB, EXPERT_SUB), y)
        write_back(b).start()
        return carry

    lax.fori_loop(0, nblk_ref[e], block, 0)

    @pl.when(e == pl.num_programs(0) - 1)
    def _():
        for back in range(EXPERT_SLOTS, 0, -1):
            @pl.when(n_used >= back)
            def _(back=back):
                write_back(n_used - back).wait()

        xbuf[...] = jnp.zeros_like(xbuf)

        def zero_copy(b):
            return pltpu.make_async_copy(slot_rows(xbuf, 0), _packed(y_hbm, b * EXPERT_BLOCK, EXPERT_BLOCK),
                                         out_sem.at[0])

        def fill(b, c):
            zero_copy(b).start()
            return c
        lax.fori_loop(n_used, n_blocks, fill, 0)

        def drain(b, c):
            zero_copy(b).wait()
            return c
        lax.fori_loop(n_used, n_blocks, drain, 0)


def _experts(layer, first_blk, n_blk, n_used, xs, w_gate_e, w_up_e, w_down_e):
    wsel = lambda e, fb, nb, nu: (layer, e, 0, 0)
    slots = pltpu.VMEM((EXPERT_SLOTS * EXPERT_BLOCK * ROW_WORDS, LANES), jnp.uint32)
    grid_spec = pltpu.PrefetchScalarGridSpec(
        num_scalar_prefetch=3,
        grid=(N_EXPERTS,),
        in_specs=[pl.BlockSpec(memory_space=pl.ANY),
                  pl.BlockSpec((None, None, D_MODEL, D_EXPERT), wsel),
                  pl.BlockSpec((None, None, D_MODEL, D_EXPERT), wsel),
                  pl.BlockSpec((None, None, D_EXPERT, D_MODEL), wsel)],
        out_specs=pl.BlockSpec(memory_space=pl.ANY),
        scratch_shapes=[slots, slots,
                        pltpu.VMEM((D_MODEL, D_EXPERT), jnp.bfloat16), pltpu.VMEM((D_MODEL, D_EXPERT), jnp.bfloat16),
                        pltpu.VMEM((D_EXPERT, D_MODEL), jnp.bfloat16),
                        pltpu.SemaphoreType.DMA((EXPERT_SLOTS,)), pltpu.SemaphoreType.DMA((EXPERT_SLOTS,))])
    return pl.pallas_call(
        _experts_kernel,
        grid_spec=grid_spec,
        out_shape=jax.ShapeDtypeStruct(xs.shape, jnp.uint32),
        compiler_params=_cparams(("arbitrary",)),
        name="experts",
    )(first_blk, n_blk, n_used, xs, w_gate_e, w_up_e, w_down_e)


def _combine_kernel(alpha, dest_hbm, y_hbm, w_ref, x1_ref, wg_ref, wu_ref, wd_ref, g_ref, b_ref,
                    o_ref, idx_smem0, idx_smem1, buf, sem):
    i = pl.program_id(0)
    n = pl.num_programs(0)
    n_idx = TM_COMBINE * TOP_K

    def slab(slot, k):
        return (slot * TOP_K + k) * TM_COMBINE

    def start_gather(step, slot):
        idx_smem = (idx_smem0, idx_smem1)[slot]
        pltpu.sync_copy(dest_hbm.at[pl.ds(step * n_idx, n_idx)], idx_smem)

        def issue(t, c):
            for k in range(TOP_K):
                pltpu.make_async_copy(_packed(y_hbm, idx_smem[t * TOP_K + k]),
                                      _packed(buf, slab(slot, k) + t), sem.at[slot]).start(priority=k % 2)
            return c
        lax.fori_loop(0, TM_COMBINE, issue, 0)

    @pl.when(i == 0)
    def _():
        start_gather(0, 0)

    def run(slot):
        @pl.when(i + 1 < n)
        def _():
            start_gather(i + 1, 1 - slot)

        for k in range(TOP_K):
            pltpu.make_async_copy(_packed(y_hbm, 0, TM_COMBINE), _packed(buf, slab(slot, k), TM_COMBINE),
                                  sem.at[slot]).wait()

        w = w_ref[...]
        routed = jnp.zeros((TM_COMBINE, D_MODEL), jnp.float32)
        for k in range(TOP_K):
            routed = routed + w[:, k:k + 1] * _unpack_rows(buf, slab(slot, k), TM_COMBINE)

        x1 = x1_ref[...]
        xb = x1.astype(jnp.bfloat16)
        gs = jnp.dot(xb, wg_ref[...], preferred_element_type=jnp.float32)
        us = jnp.dot(xb, wu_ref[...], preferred_element_type=jnp.float32)
        hs = gs * _sigmoid(gs) * us
        shared = jnp.dot(hs.astype(jnp.bfloat16), wd_ref[...], preferred_element_type=jnp.float32)
        o_ref[...] = _layer_norm(alpha * x1 + (shared + routed), g_ref[...], b_ref[...])

    for slot in range(2):
        pl.when(i % 2 == slot)(functools.partial(run, slot))


def _combine(alpha, dest_flat, y, w_tk, x1, wg, wu, wd, g, b):
    t = x1.shape[0]
    tm = TM_COMBINE
    full = lambda a: pl.BlockSpec(a.shape, lambda i: (0,) * a.ndim)
    return pl.pallas_call(
        functools.partial(_combine_kernel, alpha),
        grid=(t // tm,),
        in_specs=[pl.BlockSpec(memory_space=pl.ANY), pl.BlockSpec(memory_space=pl.ANY),
                  pl.BlockSpec((tm, TOP_K), lambda i: (i, 0)), pl.BlockSpec((tm, D_MODEL), lambda i: (i, 0)),
                  full(wg), full(wu), full(wd), full(g), full(b)],
        out_specs=pl.BlockSpec((tm, D_MODEL), lambda i: (i, 0)),
        out_shape=jax.ShapeDtypeStruct((t, D_MODEL), jnp.float32),
        scratch_shapes=[pltpu.SMEM((tm * TOP_K,), jnp.int32), pltpu.SMEM((tm * TOP_K,), jnp.int32),
                        pltpu.VMEM((2 * TOP_K * tm * ROW_WORDS, LANES), jnp.uint32),
                        pltpu.SemaphoreType.DMA((2,))],
        compiler_params=_cparams(("arbitrary",)),
        name="combine",
    )(dest_flat, y, w_tk, x1, wg, wu, wd, g, b)


def _prep_in_weights(w, b_forget):
    o = np.cumsum((0,) + IN_SIZES)
    group = SWA_Q_HEADS // SWA_KV_HEADS
    qa = w[:, o[0]:o[1]].reshape(D_MODEL, SWA_KV_HEADS, group, HEAD_DIM)
    qa = qa.transpose(0, 2, 1, 3).reshape(D_MODEL, SWA_Q)
    zf = jnp.pad(w[:, o[6]:o[7]], ((0, 0), (0, LANES - FOX_HEADS)))
    w_main = jnp.concatenate([qa, w[:, o[1]:o[6]], w[:, o[7]:o[9]], zf], axis=1).astype(jnp.bfloat16)
    bf_row = jnp.pad(b_forget, (0, LANES - FOX_HEADS)).reshape(1, LANES).astype(jnp.float32)
    return w_main, bf_row


def _prep_proj_a(w):
    group = SWA_Q_HEADS // SWA_KV_HEADS
    w4 = w.reshape(SWA_KV_HEADS, group, HEAD_DIM, D_MODEL).transpose(1, 0, 2, 3)
    return w4.reshape(SWA_Q, D_MODEL).astype(jnp.bfloat16)


def _route_plan(counts, n_tokens):
    n_assign = n_tokens * TOP_K
    n_rows = (n_assign + N_EXPERTS * (EXPERT_BLOCK - 1) + EXPERT_BLOCK - 1) // EXPERT_BLOCK * EXPERT_BLOCK
    padded = (counts + EXPERT_BLOCK - 1) // EXPERT_BLOCK * EXPERT_BLOCK
    pend = jnp.cumsum(padded)
    pstart = pend - padded
    n_used = pend[-1] // EXPERT_BLOCK
    i32 = jnp.int32
    return (pstart.astype(jnp.float32).reshape(N_EXPERTS, 1), (pstart // EXPERT_BLOCK).astype(i32),
            (padded // EXPERT_BLOCK).astype(i32), n_used.reshape(1).astype(i32), pend.astype(i32),
            padded.astype(i32), n_rows)


def kernel(x, w_in, b_forget, attn_sinks, rel_bias, w_proj_a, w_proj_b, w_out, ln1_g, ln1_b,
           w_router, router_bias, w_gate_e, w_up_e, w_down_e, w_gate_s, w_up_s, w_down_s, ln2_g, ln2_b):
    batch, seq_len, d = x.shape
    depth = w_in.shape[0]
    alpha = (2 * depth) ** 0.25
    t = batch * seq_len
    bf16 = jnp.bfloat16
    x2 = x.reshape(t, d)
    bias_tab = _t5_bias_table(rel_bias)
    for l in range(depth):
        w_main, bf_row = _prep_in_weights(w_in[l], b_forget[l])
        qa, ka, va, qf, kf, vf, ga, gb = _in_proj(x2, w_main, bf_row, seq_len)
        ya = _swa(qa, ka, va, attn_sinks[l].astype(jnp.float32), bias_tab, seq_len)
        yb = _fox(qf, kf, vf, batch, seq_len)
        x1, x1p = _merge(alpha, x2, ya, yb, ga, gb, _prep_proj_a(w_proj_a[l]), w_proj_b[l].astype(bf16),
                         w_out[l].astype(bf16), ln1_g[l].reshape(1, d), ln1_b[l].reshape(1, d))
        top_e, w_kt, rank, counts = _router(x1, w_router[l].T.astype(bf16),
                                            router_bias[l].reshape(N_EXPERTS, 1).astype(jnp.float32))
        pstart_col, first_blk, n_blk, n_used, pend, padded, n_rows = _route_plan(counts[:, 0], t)
        dest_flat = _plan(pstart_col, top_e, rank).T.reshape(-1)
        xs = _dispatch(dest_flat, x1p, pend, padded, n_used, n_rows)
        y = _experts(l, first_blk, n_blk, n_used, xs, w_gate_e, w_up_e, w_down_e)
        x2 = _combine(alpha, dest_flat, y, w_kt.T, x1, w_gate_s[l].astype(bf16), w_up_s[l].astype(bf16),
                      w_down_s[l].astype(bf16), ln2_g[l].reshape(1, d), ln2_b[l].reshape(1, d))
    return x2.reshape(batch, seq_len, d)
```
